```python
import math, functools
import jax, jax.numpy as jnp
from jax import lax
import numpy as np

D_MODEL = 2048
BATCH = 4
SEQ = 2048
DEPTH = 4
DEC_BATCH = 8
DEC_SEQ = 4
PAST_LEN = 16384
PAGE_SIZE = 128

HEAD_DIM = 128
FOX_HEADS = 8
GDN_HEADS = 8
HGRN_HEADS = 8
FOX_WIDTH = FOX_HEADS * HEAD_DIM
GDN_WIDTH = GDN_HEADS * HEAD_DIM
HGRN_WIDTH = HGRN_HEADS * HEAD_DIM
CONV_W = 4
D_FF = 4 * D_MODEL
Q_BLOCK = 128
CHUNK = 64
EPS = 1e-6
NEG_BIG = -1e30
LB_FLOOR = 1e-30
FOX_SCALE = HEAD_DIM ** -0.5
GDN_SCALE = HEAD_DIM ** -0.5
FOX_F_BIAS_INIT = 3.0
IN_SPLITS = (FOX_WIDTH, FOX_WIDTH, FOX_WIDTH, FOX_HEADS,
             GDN_WIDTH, GDN_WIDTH, GDN_WIDTH, GDN_HEADS, GDN_HEADS, GDN_WIDTH,
             HGRN_WIDTH, HGRN_WIDTH, HGRN_WIDTH, HGRN_WIDTH,
             D_MODEL, D_MODEL, D_MODEL)
IN_WIDTH = sum(IN_SPLITS)

kernel_name = "fox_gdn_hgrn2_gated_parallel_decoder_step"


def split_columns(proj):
    out, start = [], 0
    for size in IN_SPLITS:
        out.append(proj[..., start:start + size])
        start += size
    return out


def rmsnorm(x, g):
    xf = x.astype(jnp.float32)
    var = jnp.mean(xf * xf, axis=-1, keepdims=True)
    return (xf * lax.rsqrt(var + EPS)).astype(x.dtype) * g.astype(x.dtype)


def l2norm(x):
    xf = x.astype(jnp.float32)
    return xf * lax.rsqrt(jnp.sum(xf * xf, axis=-1, keepdims=True) + EPS)


def causal_conv_silu(x, buf, w):
    T = x.shape[1]
    xp = jnp.concatenate([buf.astype(x.dtype), x], axis=1)
    y = xp[:, 0:T] * w[0]
    for i in range(1, CONV_W):
        y = y + xp[:, i:i + T] * w[i]
    return jax.nn.silu(y), xp[:, -(CONV_W - 1):]


def to_chunks(x, C):
    B, T, H = x.shape[:3]
    x = x.reshape(B, T // C, C, H, *x.shape[3:])
    return jnp.moveaxis(x, (1, 3), (0, 2))


def from_chunks(x):
    n, B, H, C, d = x.shape
    return jnp.moveaxis(x, (0, 2), (1, 3)).reshape(B, n * C, H, d)


def fox_prompt(q, k, v, logf):
    B, T, H, _ = q.shape
    nb = T // Q_BLOCK
    c = jnp.cumsum(logf, axis=1).transpose(0, 2, 1)
    qb = jnp.moveaxis(q.reshape(B, nb, Q_BLOCK, H, HEAD_DIM), 1, 0)
    cb = jnp.moveaxis(c.reshape(B, H, nb, Q_BLOCK), 2, 0)
    k_pos = jnp.arange(T)

    def block(args):
        qi, ci, bi = args
        s = jnp.einsum('bqhd,bkhd->bhqk', qi, k).astype(jnp.float32) * FOX_SCALE
        s = s + ci[..., :, None] - c[:, :, None, :]
        q_pos = bi * Q_BLOCK + jnp.arange(Q_BLOCK)
        s = jnp.where(q_pos[:, None] >= k_pos[None, :], s, NEG_BIG)
        p = jax.nn.softmax(s, axis=-1).astype(v.dtype)
        return jnp.einsum('bhqk,bkhd->bqhd', p, v)

    o = lax.map(block, (qb, cb, jnp.arange(nb)))
    return jnp.moveaxis(o, 0, 1).reshape(B, T, H, HEAD_DIM)


def fox_sample(q, k, v, logf, cache_k, cache_v, cache_logf, page_table):
    B, Tn, H, _ = q.shape
    past = page_table.shape[1] * PAGE_SIZE
    keys = jnp.concatenate([cache_k[page_table].reshape(B, past, H, HEAD_DIM).astype(k.dtype), k], axis=1)
    vals = jnp.concatenate([cache_v[page_table].reshape(B, past, H, HEAD_DIM).astype(v.dtype), v], axis=1)
    lf = jnp.concatenate([cache_logf[page_table].reshape(B, past, H).astype(jnp.float32), logf], axis=1)
    L = past + Tn
    rev = (lax.cumsum(lf, axis=1, reverse=True) - lf).transpose(0, 2, 1)
    s = jnp.einsum('bqhd,bkhd->bhqk', q, keys).astype(jnp.float32) * FOX_SCALE
    s = s + rev[:, :, None, :] - rev[:, :, past:, None]
    mask = jnp.arange(L)[None, :] <= (past + jnp.arange(Tn))[:, None]
    s = jnp.where(mask, s, NEG_BIG)
    p = jax.nn.softmax(s, axis=-1).astype(vals.dtype)
    return jnp.einsum('bhqk,bkhd->bqhd', p, vals)


def gated_delta_chunked(q, k, v, g, beta, s0):
    T = q.shape[1]
    C = math.gcd(T, CHUNK)
    f32 = jnp.float32
    qc, kc, vc = (to_chunks(t.astype(f32), C) for t in (q, k, v))
    gc, bc = (to_chunks(t.astype(f32)[..., None], C)[..., 0] for t in (g, beta))
    lower = jnp.tril(jnp.ones((C, C), dtype=bool))
    strict = jnp.tril(jnp.ones((C, C), dtype=bool), -1)
    dv = v.shape[-1]

    def step(S, inp):
        qi, ki, vi, gi, bi = inp
        G = jnp.cumsum(gi, axis=-1)
        decay = jnp.exp(jnp.where(lower, G[..., :, None] - G[..., None, :], NEG_BIG))
        kb = ki * bi[..., None]
        M = jnp.where(strict, jnp.einsum('bhid,bhjd->bhij', kb, ki) * decay, 0.0)
        rhs = jnp.concatenate([vi * bi[..., None], kb * jnp.exp(G)[..., None]], axis=-1)
        sol = lax.linalg.triangular_solve(M, rhs, left_side=True, lower=True, unit_diagonal=True)
        u, w = sol[..., :dv], sol[..., dv:]
        v_new = u - jnp.einsum('bhcd,bhde->bhce', w, S)
        attn = jnp.einsum('bhid,bhjd->bhij', qi, ki) * decay
        o = jnp.einsum('bhcd,bhde->bhce', qi * jnp.exp(G)[..., None], S) + jnp.einsum('bhij,bhje->bhie', attn, v_new)
        gl = G[..., -1]
        S = S * jnp.exp(gl)[..., None, None] + jnp.einsum('bhcd,bhce->bhde', ki * jnp.exp(gl[..., None] - G)[..., None], v_new)
        return S, o

    S, o = lax.scan(step, s0.astype(f32), (qc, kc, vc, gc, bc))
    return from_chunks(o), S


def hgrn2_chunked(q, k, v, logf, s0):
    T = q.shape[1]
    C = math.gcd(T, CHUNK)
    f32 = jnp.float32
    qc, kc, vc, gc = (to_chunks(t.astype(f32), C) for t in (q, k, v, logf))
    lower = jnp.tril(jnp.ones((C, C), dtype=bool))[..., None]

    def step(S, inp):
        qi, ki, vi, gi = inp
        G = jnp.cumsum(gi, axis=2)
        decay = jnp.exp(jnp.where(lower, G[:, :, :, None, :] - G[:, :, None, :, :], NEG_BIG))
        attn = jnp.einsum('bhid,bhjd,bhijd->bhij', qi, ki, decay)
        o = jnp.einsum('bhcd,bhde->bhce', qi * jnp.exp(G), S) + jnp.einsum('bhij,bhje->bhie', attn, vi)
        gl = G[:, :, -1]
        S = S * jnp.exp(gl)[..., None] + jnp.einsum('bhcd,bhce->bhde', ki * jnp.exp(gl[:, :, None] - G), vi)
        return S, o

    S, o = lax.scan(step, s0.astype(f32), (qc, kc, vc, gc))
    return from_chunks(o), S


def token_mixers(h, lw, fox_attend, gdn_s0, conv_buf0, hgrn_s0):
    (w_in, f_bias, conv_w, a_log, dt_bias, gdn_g, lb, hgrn_g, w_fo, w_go, w_ho, w_o) = lw
    B, T, _ = h.shape
    f32 = jnp.float32
    heads = lambda t: t.reshape(B, T, -1, HEAD_DIM)
    proj = jnp.einsum('btd,de->bte', h, w_in)
    (fq, fk, fv, ff, gq, gk, gv, ga, gb, gz, hq, hf, hi, hg, gate_a, gate_b, gate_c) = split_columns(proj)
    fox_logf = jax.nn.log_sigmoid((ff + f_bias).astype(f32))
    k_a, v_a = heads(fk), heads(fv)
    o_a = fox_attend(heads(fq), k_a, v_a, fox_logf).reshape(B, T, FOX_WIDTH)
    qkv, conv_buf = causal_conv_silu(jnp.concatenate([gq, gk, gv], axis=-1), conv_buf0, conv_w)
    cq, ck, cv = jnp.split(qkv, 3, axis=-1)
    beta = jax.nn.sigmoid(gb.astype(f32))
    g_b = -jnp.exp(a_log.astype(f32)) * jax.nn.softplus(ga.astype(f32) + dt_bias.astype(f32))
    o_b, gdn_s = gated_delta_chunked(l2norm(heads(cq)) * GDN_SCALE, l2norm(heads(ck)), heads(cv), g_b, beta, gdn_s0)
    o_b = (rmsnorm(o_b.astype(h.dtype), gdn_g) * jax.nn.silu(heads(gz))).reshape(B, T, GDN_WIDTH)
    hf32 = hf.astype(f32)
    log_f = jnp.logaddexp(jnp.log(jnp.maximum(lb, LB_FLOOR)), jnp.log1p(-lb) + jax.nn.log_sigmoid(hf32))
    k_c = (1.0 - lb) * jax.nn.sigmoid(-hf32)
    o_c, hgrn_s = hgrn2_chunked(heads(jax.nn.silu(hq)), heads(k_c), heads(hi), heads(log_f), hgrn_s0)
    o_c = (rmsnorm(o_c.astype(h.dtype), hgrn_g) * jax.nn.silu(heads(hg))).reshape(B, T, HGRN_WIDTH)
    merged = (jax.nn.sigmoid(gate_a) * (o_a @ w_fo)
              + jax.nn.sigmoid(gate_b) * (o_b @ w_go)
              + jax.nn.sigmoid(gate_c) * (o_c @ w_ho))
    return (merged @ w_o, k_a, v_a, fox_logf.astype(h.dtype),
            gdn_s.astype(h.dtype), conv_buf, hgrn_s.astype(h.dtype))


def sq_relu_mlp(h, w_up, w_down):
    u = jax.nn.relu(h @ w_up)
    return (u * u) @ w_down


def setup_inputs(seed: int = 0) -> dict:
    key = jax.random.key(seed)
    ks = jax.random.split(key, 32)
    f32 = jnp.float32
    n_pages = PAST_LEN // PAGE_SIZE
    n_used = DEC_BATCH * n_pages
    n_pool = n_used + n_used // 4
    nrm = lambda k, shape, scale: scale * jax.random.normal(k, shape, f32)
    gain = lambda k, shape: 1.0 + 0.05 * jax.random.normal(k, shape, f32)
    dt = jnp.exp(jax.random.uniform(ks[17], (DEPTH, GDN_HEADS), f32, math.log(1e-3), math.log(1e-1)))
    return {
        "x_prompt": nrm(ks[0], (BATCH, SEQ, D_MODEL), 1.0),
        "x_sample": nrm(ks[1], (DEC_BATCH, DEC_SEQ, D_MODEL), 1.0),
        "cache_fox_k": nrm(ks[2], (DEPTH, n_pool, PAGE_SIZE, FOX_HEADS, HEAD_DIM), 1.0),
        "cache_fox_v": nrm(ks[3], (DEPTH, n_pool, PAGE_SIZE, FOX_HEADS, HEAD_DIM), 1.0),
        "cache_fox_logf": jax.nn.log_sigmoid(FOX_F_BIAS_INIT + jax.random.normal(ks[4], (DEPTH, n_pool, PAGE_SIZE, FOX_HEADS), f32)),
        "state_gdn": nrm(ks[5], (DEPTH, DEC_BATCH, GDN_HEADS, HEAD_DIM, HEAD_DIM), 0.5),
        "state_gdn_conv": nrm(ks[6], (DEPTH, DEC_BATCH, CONV_W - 1, 3 * GDN_WIDTH), 1.0),
        "state_hgrn": nrm(ks[7], (DEPTH, DEC_BATCH, HGRN_HEADS, HEAD_DIM, HEAD_DIM), 1.0),
        "page_table": jax.random.permutation(ks[8], n_pool)[:n_used].reshape(DEC_BATCH, n_pages).astype(jnp.int32),
        "norm_mix_pre": gain(ks[9], (DEPTH, D_MODEL)),
        "norm_mix_post": gain(ks[10], (DEPTH, D_MODEL)),
        "norm_mlp_pre": gain(ks[11], (DEPTH, D_MODEL)),
        "norm_mlp_post": gain(ks[12], (DEPTH, D_MODEL)),
        "w_in": nrm(ks[13], (DEPTH, D_MODEL, IN_WIDTH), D_MODEL ** -0.5),
        "fox_f_bias": FOX_F_BIAS_INIT + nrm(ks[14], (DEPTH, FOX_HEADS), 0.5),
        "gdn_conv_w": nrm(ks[15], (DEPTH, CONV_W, 3 * GDN_WIDTH), CONV_W ** -0.5),
        "gdn_a_log": jnp.log(jax.random.uniform(ks[16], (DEPTH, GDN_HEADS), f32, 1.0, 16.0)),
        "gdn_dt_bias": dt + jnp.log(-jnp.expm1(-dt)),
        "gdn_norm": gain(ks[18], (DEPTH, HEAD_DIM)),
        "hgrn_lower_bounds": nrm(ks[19], (DEPTH, HGRN_WIDTH), 0.1),
        "hgrn_norm": gain(ks[20], (DEPTH, HEAD_DIM)),
        "w_fox_o": nrm(ks[21], (DEPTH, FOX_WIDTH, D_MODEL), FOX_WIDTH ** -0.5),
        "w_gdn_o": nrm(ks[22], (DEPTH, GDN_WIDTH, D_MODEL), GDN_WIDTH ** -0.5),
        "w_hgrn_o": nrm(ks[23], (DEPTH, HGRN_WIDTH, D_MODEL), HGRN_WIDTH ** -0.5),
        "w_out": nrm(ks[24], (DEPTH, D_MODEL, D_MODEL), D_MODEL ** -0.5),
        "w_up": nrm(ks[25], (DEPTH, D_MODEL, D_FF), D_MODEL ** -0.5),
        "w_down": nrm(ks[26], (DEPTH, D_FF, D_MODEL), D_FF ** -0.5),
    }


def reference(x_prompt, x_sample, cache_fox_k, cache_fox_v, cache_fox_logf, state_gdn, state_gdn_conv, state_hgrn,
              page_table, norm_mix_pre, norm_mix_post, norm_mlp_pre, norm_mlp_post, w_in, fox_f_bias, gdn_conv_w,
              gdn_a_log, gdn_dt_bias, gdn_norm, hgrn_lower_bounds, hgrn_norm, w_fox_o, w_gdn_o, w_hgrn_o, w_out,
              w_up, w_down):
    f32 = jnp.float32
    lb_p = jax.nn.softmax(hgrn_lower_bounds.astype(f32), axis=0)
    lb_all = jnp.cumsum(lb_p, axis=0) - lb_p[0]
    xp, xs = x_prompt, x_sample
    Bp = xp.shape[0]
    fkp, fvp, flp, fks, fvs, fls = [], [], [], [], [], []
    gsp, gss, gcp, gcs, hsp, hss = [], [], [], [], [], []
    for l in range(DEPTH):
        lw = (w_in[l], fox_f_bias[l], gdn_conv_w[l], gdn_a_log[l], gdn_dt_bias[l], gdn_norm[l], lb_all[l],
              hgrn_norm[l], w_fox_o[l], w_gdn_o[l], w_hgrn_o[l], w_out[l])
        mp, k_p, v_p, lf_p, gs_p, gc_p, hs_p = token_mixers(
            rmsnorm(xp, norm_mix_pre[l]), lw, fox_prompt,
            jnp.zeros((Bp, GDN_HEADS, HEAD_DIM, HEAD_DIM), f32),
            jnp.zeros((Bp, CONV_W - 1, 3 * GDN_WIDTH), xp.dtype),
            jnp.zeros((Bp, HGRN_HEADS, HEAD_DIM, HEAD_DIM), f32))
        xp = xp + rmsnorm(mp, norm_mix_post[l])
        xp = xp + rmsnorm(sq_relu_mlp(rmsnorm(xp, norm_mlp_pre[l]), w_up[l], w_down[l]), norm_mlp_post[l])
        fox_s = functools.partial(fox_sample, cache_k=cache_fox_k[l], cache_v=cache_fox_v[l],
                                  cache_logf=cache_fox_logf[l], page_table=page_table)
        ms, k_s, v_s, lf_s, gs_s, gc_s, hs_s = token_mixers(
            rmsnorm(xs, norm_mix_pre[l]), lw, fox_s, state_gdn[l], state_gdn_conv[l], state_hgrn[l])
        xs = xs + rmsnorm(ms, norm_mix_post[l])
        xs = xs + rmsnorm(sq_relu_mlp(rmsnorm(xs, norm_mlp_pre[l]), w_up[l], w_down[l]), norm_mlp_post[l])
        fkp.append(k_p); fvp.append(v_p); flp.append(lf_p)
        fks.append(k_s); fvs.append(v_s); fls.append(lf_s)
        gsp.append(gs_p); gss.append(gs_s); gcp.append(gc_p); gcs.append(gc_s)
        hsp.append(hs_p); hss.append(hs_s)
    return (xp, xs,
            jnp.stack(fkp), jnp.stack(fvp), jnp.stack(flp),
            jnp.stack(fks), jnp.stack(fvs), jnp.stack(fls),
            jnp.stack(gsp), jnp.stack(gss),
            jnp.stack(gcp), jnp.stack(gcs),
            jnp.stack(hsp), jnp.stack(hss))
```

```python
import functools

import jax
import jax.numpy as jnp
from jax import lax
from jax.experimental import pallas as pl
from jax.experimental.pallas import tpu as pltpu

F32, BF16 = jnp.float32, jnp.bfloat16
HEAD_DIM = 128
CONV_W = 4
CHUNK = 64
EPS = 1e-6
NEG_BIG = -1e30
LB_FLOOR = 1e-30
LANES = 128
SUBLANES = 8
V7X_VMEM_BYTES = 64 * 2**20
VMEM_RESERVE_BYTES = 6 * 2**20
KERNEL_TEMP_BYTES = 16 * 2**20
DECODE_PAGES_PER_STEP = 4
FOX_Q_TILES = (512, 256, 128)


def _nbytes(shape, dtype):
    n = 1
    for s in shape:
        n *= s
    return n * jnp.dtype(dtype).itemsize


def _params(semantics, blocks, scratch=()):
    need = 2 * sum(_nbytes(s, d) for s, d in blocks) + sum(_nbytes(s, d) for s, d in scratch) + KERNEL_TEMP_BYTES
    return pltpu.CompilerParams(dimension_semantics=semantics,
                                vmem_limit_bytes=min(need, V7X_VMEM_BYTES - VMEM_RESERVE_BYTES))


def _dot(a, b):
    return jnp.dot(a, b, preferred_element_type=F32)


def _dot_nt(a, b):
    return lax.dot_general(a, b, (((1,), (1,)), ((), ())), preferred_element_type=F32)


def _split3(x):
    x1 = x.astype(BF16)
    r = x - x1.astype(F32)
    x2 = r.astype(BF16)
    x3 = (r - x2.astype(F32)).astype(BF16)
    return x1, x2, x3


def _dot01(m01, x):
    x1, x2, x3 = _split3(x)
    return _dot(m01, x1) + (_dot(m01, x2) + _dot(m01, x3))


def _dot_hi(a, b):
    a1 = a.astype(BF16)
    a2 = (a - a1.astype(F32)).astype(BF16)
    b1 = b.astype(BF16)
    b2 = (b - b1.astype(F32)).astype(BF16)
    return _dot(a1, b1) + (_dot(a1, b2) + _dot(a2, b1))


def _log_sigmoid(x):
    return -(jnp.maximum(-x, 0.0) + jnp.log1p(jnp.exp(-jnp.abs(x))))


def _softplus(x):
    return jnp.maximum(x, 0.0) + jnp.log1p(jnp.exp(-jnp.abs(x)))


def _silu(x):
    return x * jax.nn.sigmoid(x)


def _tril01(n, strict=False):
    r = lax.broadcasted_iota(jnp.int32, (n, n), 0)
    c = lax.broadcasted_iota(jnp.int32, (n, n), 1)
    return (r > c) if strict else (r >= c)


def _lower_bounds_kernel(x_ref, o_ref):
    x = x_ref[...]
    e = jnp.exp(x - jnp.max(x, axis=0, keepdims=True))
    p = e / jnp.sum(e, axis=0, keepdims=True)
    run = jnp.zeros_like(p[0:1])
    for l in range(x.shape[0]):
        run = run + p[l:l + 1]
        o_ref[l:l + 1, :] = run - p[0:1]


def _lower_bounds(x):
    return pl.pallas_call(_lower_bounds_kernel, out_shape=jax.ShapeDtypeStruct(x.shape, F32),
                          name="hgrn_lower_bounds")(x.astype(F32))


def _norm_matmul_kernel(x_ref, g_ref, w_ref, o_ref, xn_ref, *, act):
    @pl.when(pl.program_id(1) == 0)
    def _():
        x = x_ref[...]
        var = jnp.mean(x * x, axis=-1, keepdims=True)
        xn_ref[...] = ((x * lax.rsqrt(var + EPS)) * g_ref[...]).astype(BF16)

    y = _dot(xn_ref[...], w_ref[...])
    if act == "relu2":
        y = jnp.maximum(y, 0.0)
        y = y * y
    o_ref[...] = y.astype(o_ref.dtype)


def _norm_matmul(x, g, w, layer, *, tm, tn, act=None, out_dtype=F32, name):
    n, d = x.shape
    wn = w.shape[-1]
    blocks = [((tm, d), F32), ((1, d), F32), ((d, tn), BF16), ((tm, tn), out_dtype)]
    scratch = [((tm, d), BF16)]
    return pl.pallas_call(
        functools.partial(_norm_matmul_kernel, act=act),
        grid=(n // tm, wn // tn),
        in_specs=[pl.BlockSpec((tm, d), lambda i, j: (i, 0)),
                  pl.BlockSpec((None, 1, d), lambda i, j: (layer, 0, 0)),
                  pl.BlockSpec((None, d, tn), lambda i, j: (layer, 0, j))],
        out_specs=pl.BlockSpec((tm, tn), lambda i, j: (i, j)),
        out_shape=jax.ShapeDtypeStruct((n, wn), out_dtype),
        scratch_shapes=[pltpu.VMEM(s, t) for s, t in scratch],
        compiler_params=_params(("parallel", "arbitrary"), blocks, scratch),
        name=name)(x, g, w)


def _matmul_norm_res_kernel(a_ref, w_ref, x_ref, g_ref, o_ref, acc_ref):
    k = pl.program_id(1)

    @pl.when(k == 0)
    def _():
        acc_ref[...] = jnp.zeros_like(acc_ref)

    acc_ref[...] += _dot(a_ref[...], w_ref[...])

    @pl.when(k == pl.num_programs(1) - 1)
    def _():
        y = acc_ref[...]
        var = jnp.mean(y * y, axis=-1, keepdims=True)
        o_ref[...] = x_ref[...] + (y * lax.rsqrt(var + EPS)) * g_ref[...]


def _matmul_norm_res(a, w, x, g, layer, *, tm, tk, name):
    n, kdim = a.shape
    d = x.shape[1]
    blocks = [((tm, tk), BF16), ((tk, d), BF16), ((tm, d), F32), ((1, d), F32), ((tm, d), F32)]
    scratch = [((tm, d), F32)]
    return pl.pallas_call(
        _matmul_norm_res_kernel,
        grid=(n // tm, kdim // tk),
        in_specs=[pl.BlockSpec((tm, tk), lambda i, k: (i, k)),
                  pl.BlockSpec((None, tk, d), lambda i, k: (layer, k, 0)),
                  pl.BlockSpec((tm, d), lambda i, k: (i, 0)),
                  pl.BlockSpec((None, 1, d), lambda i, k: (layer, 0, 0))],
        out_specs=pl.BlockSpec((tm, d), lambda i, k: (i, 0)),
        out_shape=jax.ShapeDtypeStruct((n, d), F32),
        scratch_shapes=[pltpu.VMEM(s, t) for s, t in scratch],
        compiler_params=_params(("parallel", "arbitrary"), blocks, scratch),
        name=name)(a, w, x, g)


def _merge_kernel(oa_ref, ob_ref, oc_ref, ga_ref, gb_ref, gc_ref, wa_ref, wb_ref, wc_ref, o_ref):
    m = jax.nn.sigmoid(ga_ref[...]) * _dot(oa_ref[...], wa_ref[...])
    m = m + jax.nn.sigmoid(gb_ref[...]) * _dot(ob_ref[...], wb_ref[...])
    m = m + jax.nn.sigmoid(gc_ref[...]) * _dot(oc_ref[...], wc_ref[...])
    o_ref[...] = m.astype(o_ref.dtype)


def _merge(oa, ob, oc, proj, lay, wa, wb, wc, layer, *, tm, tn):
    n = oa.shape[0]
    d = wa.shape[-1]
    col = {k: lay[k][0] // tn for k in ("gate_a", "gate_b", "gate_c")}
    o_specs = [pl.BlockSpec((tm, o.shape[1]), lambda i, j: (i, 0)) for o in (oa, ob, oc)]
    g_specs = [pl.BlockSpec((tm, tn), functools.partial(lambda i, j, c: (i, c + j), c=col[k]))
               for k in ("gate_a", "gate_b", "gate_c")]
    w_specs = [pl.BlockSpec((None, w.shape[1], tn), lambda i, j: (layer, 0, j)) for w in (wa, wb, wc)]
    blocks = ([((tm, o.shape[1]), BF16) for o in (oa, ob, oc)] + [((tm, tn), F32)] * 3
              + [((w.shape[1], tn), BF16) for w in (wa, wb, wc)] + [((tm, tn), BF16)])
    return pl.pallas_call(
        _merge_kernel,
        grid=(n // tm, d // tn),
        in_specs=o_specs + g_specs + w_specs,
        out_specs=pl.BlockSpec((tm, tn), lambda i, j: (i, j)),
        out_shape=jax.ShapeDtypeStruct((n, d), BF16),
        compiler_params=_params(("parallel", "parallel"), blocks),
        name="gated_merge")(oa, ob, oc, proj, proj, proj, wa, wb, wc)


def _fox_gate_kernel(ff_ref, b_ref, lf_ref, c_ref, ct_ref):
    t = ff_ref.shape[0]
    lf_ref[...] = _log_sigmoid(ff_ref[...] + b_ref[...])
    tri = _tril01(LANES).astype(BF16)
    carry = jnp.zeros((1, LANES), F32)
    for blk in range(t // LANES):
        rows = slice(blk * LANES, (blk + 1) * LANES)
        cb = _dot01(tri, lf_ref[rows, :]) + carry
        c_ref[rows, :] = cb
        ct_ref[:, rows] = cb.T
        carry = cb[LANES - 1:LANES, :]


def _fox_gate(proj, lay, bias, layer, bx, tx):
    n = proj.shape[0]
    col = lay["ff"][0] // LANES
    blocks = [((tx, LANES), F32)] * 4 + [((LANES, tx), F32)]
    return pl.pallas_call(
        _fox_gate_kernel,
        grid=(bx,),
        in_specs=[pl.BlockSpec((tx, LANES), lambda b: (b, col)),
                  pl.BlockSpec((None, 1, LANES), lambda b: (layer, 0, 0))],
        out_specs=[pl.BlockSpec((tx, LANES), lambda b: (b, 0)),
                   pl.BlockSpec((tx, LANES), lambda b: (b, 0)),
                   pl.BlockSpec((None, LANES, tx), lambda b: (b, 0, 0))],
        out_shape=[jax.ShapeDtypeStruct((n, LANES), F32), jax.ShapeDtypeStruct((n, LANES), F32),
                   jax.ShapeDtypeStruct((bx, LANES, tx), F32)],
        compiler_params=_params(("parallel",), blocks),
        name="fox_gate")(proj, bias)


def _fox_attn_kernel(q_ref, k_ref, v_ref, ct_ref, cr_ref, o_ref, m_ref, l_ref, acc_ref, *, heads, scale):
    i, j = pl.program_id(1), pl.program_id(2)
    tq, tk = q_ref.shape[0], k_ref.shape[0]

    @pl.when(j == 0)
    def _():
        m_ref[...] = jnp.full_like(m_ref, NEG_BIG)
        l_ref[...] = jnp.zeros_like(l_ref)
        acc_ref[...] = jnp.zeros_like(acc_ref)

    @pl.when(j <= i)
    def _():
        q_pos = i * tq + lax.broadcasted_iota(jnp.int32, (tq, tk), 0)
        k_pos = j * tk + lax.broadcasted_iota(jnp.int32, (tq, tk), 1)
        causal = q_pos >= k_pos
        for h in range(heads):
            hs = slice(h * HEAD_DIM, (h + 1) * HEAD_DIM)
            s = _dot_nt(q_ref[:, hs].astype(BF16), k_ref[:, hs].astype(BF16)) * scale
            s = s + ct_ref[:, h:h + 1] - cr_ref[h:h + 1, :]
            s = jnp.where(causal, s, NEG_BIG)
            m_prev = m_ref[:, h:h + 1]
            m_new = jnp.maximum(m_prev, jnp.max(s, axis=-1, keepdims=True))
            alpha = jnp.exp(m_prev - m_new)
            p = jnp.exp(s - m_new)
            l_ref[:, h:h + 1] = alpha * l_ref[:, h:h + 1] + jnp.sum(p, axis=-1, keepdims=True)
            acc_ref[:, hs] = alpha * acc_ref[:, hs] + _dot(p.astype(BF16), v_ref[:, hs].astype(BF16))
            m_ref[:, h:h + 1] = m_new

    @pl.when(j == i)
    def _():
        for h in range(heads):
            hs = slice(h * HEAD_DIM, (h + 1) * HEAD_DIM)
            o_ref[:, hs] = (acc_ref[:, hs] / l_ref[:, h:h + 1]).astype(o_ref.dtype)


def _fox_attn(proj, lay, c_tok, c_row, bx, tx, *, tq):
    n = proj.shape[0]
    fw = lay["fq"][1]
    heads = fw // HEAD_DIM
    nq = tx // tq
    qc, kc, vc = (lay[k][0] // fw for k in ("fq", "fk", "fv"))
    blocks = [((tq, fw), F32)] * 3 + [((tq, LANES), F32), ((LANES, tq), F32), ((tq, fw), BF16)]
    scratch = [((tq, LANES), F32), ((tq, LANES), F32), ((tq, fw), F32)]
    return pl.pallas_call(
        functools.partial(_fox_attn_kernel, heads=heads, scale=HEAD_DIM ** -0.5),
        grid=(bx, nq, nq),
        in_specs=[pl.BlockSpec((tq, fw), lambda b, i, j: (b * nq + i, qc)),
                  pl.BlockSpec((tq, fw), lambda b, i, j: (b * nq + jnp.minimum(j, i), kc)),
                  pl.BlockSpec((tq, fw), lambda b, i, j: (b * nq + jnp.minimum(j, i), vc)),
                  pl.BlockSpec((tq, LANES), lambda b, i, j: (b * nq + i, 0)),
                  pl.BlockSpec((None, LANES, tq), lambda b, i, j: (b, 0, jnp.minimum(j, i)))],
        out_specs=pl.BlockSpec((tq, fw), lambda b, i, j: (b * nq + i, 0)),
        out_shape=jax.ShapeDtypeStruct((n, fw), BF16),
        scratch_shapes=[pltpu.VMEM(s, t) for s, t in scratch],
        compiler_params=_params(("parallel", "parallel", "arbitrary"), blocks, scratch),
        name="fox_attn_prompt")(proj, proj, proj, c_tok, c_row)


def _page_suffix(x, lane, sub, heads):
    y = x
    sh = heads
    while sh < LANES:
        y = y + jnp.where(lane < LANES - sh, pltpu.roll(y, LANES - sh, axis=1), 0.0)
        sh *= 2
    t = jnp.where(lane < heads, y, 0.0)
    sh = heads
    while sh < LANES:
        t = t + pltpu.roll(t, sh, axis=1)
        sh *= 2
    z = t
    sh = 1
    while sh < SUBLANES:
        z = z + jnp.where(sub < SUBLANES - sh, pltpu.roll(z, SUBLANES - sh, axis=0), 0.0)
        sh *= 2
    return (y - x) + (z - t), z[0:1, :]


def _fox_decode_kernel(pt_ref, q_ref, kn_ref, vn_ref, ffp_ref, bp_ref, ffc_ref, bc_ref, *rest,
                       pages, heads, n_new, scale):
    del pt_ref
    k_refs, v_refs, lf_refs = rest[:pages], rest[pages:2 * pages], rest[2 * pages:3 * pages]
    o_ref, lfo_ref, m_ref, l_ref, acc_ref, carry_ref, rn_ref = rest[3 * pages:]
    step = pl.program_id(1)
    rows = q_ref.shape[0]
    lane = lax.broadcasted_iota(jnp.int32, (SUBLANES, LANES), 1)
    sub = lax.broadcasted_iota(jnp.int32, (SUBLANES, LANES), 0)
    r_i = lax.broadcasted_iota(jnp.int32, (rows, LANES), 0)
    c_i = lax.broadcasted_iota(jnp.int32, (rows, LANES), 1)
    head_ok = (r_i % heads) == (c_i % heads)
    qb = q_ref[...].astype(BF16)

    def update(scores, values):
        m_prev = m_ref[...]
        m_new = m_prev
        for s in scores:
            m_new = jnp.maximum(m_new, jnp.max(s, axis=-1, keepdims=True))
        alpha = jnp.exp(m_prev - m_new)
        l_new = alpha * l_ref[...]
        acc = alpha * acc_ref[...]
        for s, v in zip(scores, values):
            p = jnp.exp(s - m_new)
            l_new = l_new + jnp.sum(p, axis=-1, keepdims=True)
            acc = acc + _dot(p.astype(BF16), v)
        m_ref[...] = m_new
        l_ref[...] = l_new
        acc_ref[...] = acc

    @pl.when(step == 0)
    def _():
        m_ref[...] = jnp.full_like(m_ref, NEG_BIG)
        l_ref[...] = jnp.zeros_like(l_ref)
        acc_ref[...] = jnp.zeros_like(acc_ref)
        new_ok = (sub == 0) & (lane < n_new * heads)
        lf = jnp.where(new_ok, _log_sigmoid(ffp_ref[...] + bp_ref[...]), 0.0)
        lfo_ref[...] = lf
        gate, total = _page_suffix(lf, lane, sub, heads)
        carry_ref[...] = total
        lfc = _log_sigmoid(ffc_ref[...] + bc_ref[...])
        run = jnp.zeros((heads, 1), F32)
        for t in range(n_new - 1, -1, -1):
            rn_ref[t * heads:(t + 1) * heads, :] = run
            run = run + lfc[t * heads:(t + 1) * heads, :]
        s = _dot_nt(qb, kn_ref[...].astype(BF16)) * scale + gate[0:1, :] - rn_ref[...]
        ok = head_ok & (c_i // heads <= r_i // heads) & (c_i < n_new * heads)
        update([jnp.where(ok, s, NEG_BIG)], [vn_ref[...].astype(BF16)])

    for r in range(pages):
        gate, total = _page_suffix(lf_refs[r][...], lane, sub, heads)
        gate = gate + carry_ref[...]
        carry_ref[...] = carry_ref[...] + total
        scores, values = [], []
        for c in range(SUBLANES):
            ks = slice(c * LANES, (c + 1) * LANES)
            s = _dot_nt(qb, k_refs[r][ks, :].astype(BF16)) * scale + gate[c:c + 1, :] - rn_ref[...]
            scores.append(jnp.where(head_ok, s, NEG_BIG))
            values.append(v_refs[r][ks, :].astype(BF16))
        update(scores, values)

    @pl.when(step == pl.num_programs(1) - 1)
    def _():
        o_ref[...] = (acc_ref[...] / l_ref[...]).astype(o_ref.dtype)


def _fox_decode(q, k_new, v_new, ff_page, bias_page, ff_col, bias_col, cache_k, cache_v, cache_lf, pages_flat,
                n_pages, *, heads, n_new):
    bs, rows, _ = q.shape
    page_rows = cache_k.shape[1]
    assert page_rows == SUBLANES * LANES and heads * (LANES // heads) == LANES and n_pages % DECODE_PAGES_PER_STEP == 0
    pages = DECODE_PAGES_PER_STEP
    n_steps = n_pages // pages

    def page_map(r):
        return lambda b, s, pt: (pt[b * n_pages + (n_pages - 1 - (s * pages + r))], 0, 0)

    in_specs = [pl.BlockSpec((None, rows, HEAD_DIM), lambda b, s, pt: (b, 0, 0)),
                pl.BlockSpec((None, LANES, HEAD_DIM), lambda b, s, pt: (b, 0, 0)),
                pl.BlockSpec((None, LANES, HEAD_DIM), lambda b, s, pt: (b, 0, 0)),
                pl.BlockSpec((None, SUBLANES, LANES), lambda b, s, pt: (b, 0, 0)),
                pl.BlockSpec((SUBLANES, LANES), lambda b, s, pt: (0, 0)),
                pl.BlockSpec((None, rows, 1), lambda b, s, pt: (b, 0, 0)),
                pl.BlockSpec((rows, 1), lambda b, s, pt: (0, 0))]
    in_specs += [pl.BlockSpec((None, page_rows, HEAD_DIM), page_map(r)) for r in range(pages)] * 2
    in_specs += [pl.BlockSpec((None, SUBLANES, LANES), page_map(r)) for r in range(pages)]
    blocks = [((page_rows, HEAD_DIM), F32)] * (2 * pages) + [((LANES, HEAD_DIM), F32)] * 4
    scratch = [((rows, 1), F32), ((rows, 1), F32), ((rows, HEAD_DIM), F32), ((1, LANES), F32), ((rows, 1), F32)]
    grid_spec = pltpu.PrefetchScalarGridSpec(
        num_scalar_prefetch=1, grid=(bs, n_steps), in_specs=in_specs,
        out_specs=[pl.BlockSpec((None, rows, HEAD_DIM), lambda b, s, pt: (b, 0, 0)),
                   pl.BlockSpec((None, SUBLANES, LANES), lambda b, s, pt: (b, 0, 0))],
        scratch_shapes=[pltpu.VMEM(s, t) for s, t in scratch])
    return pl.pallas_call(
        functools.partial(_fox_decode_kernel, pages=pages, heads=heads, n_new=n_new, scale=HEAD_DIM ** -0.5),
        grid_spec=grid_spec,
        out_shape=[jax.ShapeDtypeStruct((bs, rows, HEAD_DIM), BF16),
                   jax.ShapeDtypeStruct((bs, SUBLANES, LANES), F32)],
        compiler_params=_params(("parallel", "arbitrary"), blocks, scratch),
        name="fox_decode")(pages_flat, q, k_new, v_new, ff_page, bias_page, ff_col, bias_col,
                           *([cache_k] * pages), *([cache_v] * pages), *([cache_lf] * pages))


def _unit_lower_solve(m, rhs):
    c = m.shape[0]
    x = rhs - _dot_hi(m, rhs)
    p, power = m, 1
    while 2 * power < c:
        p = _dot_hi(p, p)
        power *= 2
        x = x + _dot_hi(p, x)
    return x


def _gdn_kernel(qkv_ref, ab_ref, gz_ref, cw_ref, al_ref, dt_ref, gn_ref, s0_ref, cb0_ref,
                o_ref, s_out_ref, cb_out_ref, xe_ref, st_ref, *, heads, t_valid):
    c = pl.program_id(1)
    ch = qkv_ref.shape[0]
    width = heads * HEAD_DIM
    tail = CONV_W - 1
    base = SUBLANES
    last_chunk = (t_valid - 1) // ch
    last_len = t_valid - last_chunk * ch

    @pl.when(c == 0)
    def _():
        st_ref[...] = s0_ref[...]
        xe_ref[base - tail:base, :] = cb0_ref[...]

    xe_ref[base:base + ch, :] = qkv_ref[...]
    y = xe_ref[base - tail:base - tail + ch, :] * cw_ref[0:1, :]
    for i in range(1, CONV_W):
        y = y + xe_ref[base - tail + i:base - tail + i + ch, :] * cw_ref[i:i + 1, :]
    y = _silu(y)

    @pl.when(c == last_chunk)
    def _():
        cb_out_ref[...] = xe_ref[base + last_len - tail:base + last_len, :]

    xe_ref[base - tail:base, :] = xe_ref[base + ch - tail:base + ch, :]

    valid = (c * ch + lax.broadcasted_iota(jnp.int32, (ch, 1), 0)) < t_valid
    ab = ab_ref[...]
    g_all = jnp.where(valid, -jnp.exp(al_ref[...]) * _softplus(ab + dt_ref[...]), 0.0)
    beta_all = jnp.where(valid, jax.nn.sigmoid(ab), 0.0)
    cum = _dot01(_tril01(ch).astype(BF16), g_all)
    cum_t = cum.T
    lower = _tril01(ch)
    strict = _tril01(ch, strict=True)

    for h in range(heads):
        gc = cum[:, h:h + 1]
        gr = cum_t[h:h + 1, :]
        beta = beta_all[:, heads + h:heads + h + 1]
        decay = jnp.exp(jnp.where(lower, gc - gr, NEG_BIG))
        q = y[:, h * HEAD_DIM:(h + 1) * HEAD_DIM]
        k = y[:, width + h * HEAD_DIM:width + (h + 1) * HEAD_DIM]
        v = y[:, 2 * width + h * HEAD_DIM:2 * width + (h + 1) * HEAD_DIM]
        q = q * lax.rsqrt(jnp.sum(q * q, axis=-1, keepdims=True) + EPS) * (HEAD_DIM ** -0.5)
        k = k * lax.rsqrt(jnp.sum(k * k, axis=-1, keepdims=True) + EPS)
        kb = k * beta
        k16 = k.astype(BF16)
        m = jnp.where(strict, _dot_nt(kb.astype(BF16), k16) * decay, 0.0)
        eg = jnp.exp(gc)
        sol = _unit_lower_solve(m, jnp.concatenate([v * beta, kb * eg], axis=-1))
        u, w = sol[:, :HEAD_DIM], sol[:, HEAD_DIM:]
        s_old = st_ref[h]
        s16 = s_old.astype(BF16)
        v_new = u - _dot(w.astype(BF16), s16)
        vn16 = v_new.astype(BF16)
        attn = _dot_nt(q.astype(BF16), k16) * decay
        o = _dot((q * eg).astype(BF16), s16) + _dot(attn.astype(BF16), vn16)
        gl = cum[ch - 1:ch, h:h + 1]
        kd = k * jnp.exp(gl - gc)
        st_ref[h] = s_old * jnp.exp(gl) + _dot(kd.T.astype(BF16), vn16)
        on = (o * lax.rsqrt(jnp.mean(o * o, axis=-1, keepdims=True) + EPS)) * gn_ref[...]
        o_ref[:, h * HEAD_DIM:(h + 1) * HEAD_DIM] = (
            on * _silu(gz_ref[:, h * HEAD_DIM:(h + 1) * HEAD_DIM])).astype(o_ref.dtype)

    @pl.when(c == pl.num_programs(1) - 1)
    def _():
        s_out_ref[...] = st_ref[...]


def _gdn(proj, lay, conv_w, a_log, dt_bias, gnorm, s0, cb0, layer, bx, tx, t_valid):
    n = proj.shape[0]
    gw = lay["gz"][1]
    heads = gw // HEAD_DIM
    ch = CHUNK
    nc = tx // ch
    qkv_c, ab_c, gz_c = lay["gqkv"][0] // (3 * gw), lay["gab"][0] // LANES, lay["gz"][0] // gw
    blocks = [((ch, 3 * gw), F32), ((ch, LANES), F32), ((ch, gw), F32), ((CONV_W, 3 * gw), F32),
              ((heads, HEAD_DIM, HEAD_DIM), F32), ((CONV_W - 1, 3 * gw), F32), ((ch, gw), BF16),
              ((heads, HEAD_DIM, HEAD_DIM), F32), ((CONV_W - 1, 3 * gw), F32)]
    scratch = [((SUBLANES + ch, 3 * gw), F32), ((heads, HEAD_DIM, HEAD_DIM), F32)]
    return pl.pallas_call(
        functools.partial(_gdn_kernel, heads=heads, t_valid=t_valid),
        grid=(bx, nc),
        in_specs=[pl.BlockSpec((ch, 3 * gw), lambda b, c: (b * nc + c, qkv_c)),
                  pl.BlockSpec((ch, LANES), lambda b, c: (b * nc + c, ab_c)),
                  pl.BlockSpec((ch, gw), lambda b, c: (b * nc + c, gz_c)),
                  pl.BlockSpec((None, CONV_W, 3 * gw), lambda b, c: (layer, 0, 0)),
                  pl.BlockSpec((None, 1, LANES), lambda b, c: (layer, 0, 0)),
                  pl.BlockSpec((None, 1, LANES), lambda b, c: (layer, 0, 0)),
                  pl.BlockSpec((None, 1, HEAD_DIM), lambda b, c: (layer, 0, 0)),
                  pl.BlockSpec((None, heads, HEAD_DIM, HEAD_DIM), lambda b, c: (b, 0, 0, 0)),
                  pl.BlockSpec((None, CONV_W - 1, 3 * gw), lambda b, c: (b, 0, 0))],
        out_specs=[pl.BlockSpec((ch, gw), lambda b, c: (b * nc + c, 0)),
                   pl.BlockSpec((None, heads, HEAD_DIM, HEAD_DIM), lambda b, c: (b, 0, 0, 0)),
                   pl.BlockSpec((None, CONV_W - 1, 3 * gw), lambda b, c: (b, 0, 0))],
        out_shape=[jax.ShapeDtypeStruct((n, gw), BF16),
                   jax.ShapeDtypeStruct((bx, heads, HEAD_DIM, HEAD_DIM), F32),
                   jax.ShapeDtypeStruct((bx, CONV_W - 1, 3 * gw), F32)],
        scratch_shapes=[pltpu.VMEM(s, t) for s, t in scratch],
        compiler_params=_params(("parallel", "arbitrary"), blocks, scratch),
        name="gated_deltanet")(proj, proj, proj, conv_w, a_log, dt_bias, gnorm, s0, cb0)


def _hgrn_kernel(hq_ref, hf_ref, hi_ref, hg_ref, lb_ref, hn_ref, s0_ref, o_ref, s_out_ref, st_ref,
                 *, heads, t_valid):
    c = pl.program_id(1)
    ch = hq_ref.shape[0]
    nblk = ch // SUBLANES

    @pl.when(c == 0)
    def _():
        for h in range(heads):
            st_ref[h] = s0_ref[h].T

    tok = lax.broadcasted_iota(jnp.int32, (ch, 1), 0)
    valid = (c * ch + tok) < t_valid
    lb = lb_ref[...]
    hf = hf_ref[...]
    a = jnp.log(jnp.maximum(lb, LB_FLOOR))
    b = jnp.log1p(-lb) + _log_sigmoid(hf)
    log_f = jnp.maximum(a, b) + jnp.log1p(jnp.exp(-jnp.abs(a - b)))
    log_f = jnp.where(valid, log_f, 0.0)
    k_all = jnp.where(valid, (1.0 - lb) * jax.nn.sigmoid(-hf), 0.0)
    cum_all = _dot01(_tril01(ch).astype(BF16), log_f)
    q_all = _silu(hq_ref[...])

    ti = lax.broadcasted_iota(jnp.int32, (ch, ch), 0)
    tj = lax.broadcasted_iota(jnp.int32, (ch, ch), 1)
    sub3 = lax.broadcasted_iota(jnp.int32, (nblk, SUBLANES, HEAD_DIM), 1)

    for h in range(heads):
        hs = slice(h * HEAD_DIM, (h + 1) * HEAD_DIM)
        g, q, k, v = cum_all[:, hs], q_all[:, hs], k_all[:, hs], hi_ref[:, hs]
        attn = jnp.zeros((ch, ch), F32)
        half = ch // 2
        while half >= SUBLANES:
            blk = 2 * half
            g3 = g.reshape(ch // blk, blk, HEAD_DIM)
            e = jnp.exp(-jnp.abs(g3 - g3[:, half - 1:half, :])).reshape(ch, HEAD_DIM)
            upper = (tok % blk) >= half
            qa = jnp.where(upper, q * e, 0.0).astype(BF16)
            kb = jnp.where(upper, 0.0, k * e).astype(BF16)
            attn = attn + jnp.where((ti // blk) == (tj // blk), _dot_nt(qa, kb), 0.0)
            half //= 2
        st = st_ref[h]
        o = _dot_nt((q * jnp.exp(g)).astype(BF16), st.astype(BF16)) + _dot(attn.astype(BF16), v.astype(BF16))
        g3, q3, k3, v3 = (z.reshape(nblk, SUBLANES, HEAD_DIM) for z in (g, q, k, v))
        od = jnp.zeros((nblk, SUBLANES, HEAD_DIM), F32)
        for i in range(SUBLANES):
            e = jnp.exp(jnp.where(sub3 <= i, g3[:, i:i + 1, :] - g3, NEG_BIG))
            w = jnp.sum(e * k3 * q3[:, i:i + 1, :], axis=-1, keepdims=True)
            od = od + jnp.where(sub3 == i, jnp.sum(w * v3, axis=1, keepdims=True), 0.0)
        o = o + od.reshape(ch, HEAD_DIM)
        gl = g[ch - 1:ch, :]
        kd = k * jnp.exp(gl - g)
        st_ref[h] = st * jnp.exp(gl) + _dot(v.T.astype(BF16), kd.astype(BF16))
        on = (o * lax.rsqrt(jnp.mean(o * o, axis=-1, keepdims=True) + EPS)) * hn_ref[...]
        o_ref[:, hs] = (on * _silu(hg_ref[:, hs])).astype(o_ref.dtype)

    @pl.when(c == pl.num_programs(1) - 1)
    def _():
        for h in range(heads):
            s_out_ref[h] = st_ref[h].T


def _hgrn(proj, lay, lb, hnorm, s0, layer, bx, tx, t_valid):
    n = proj.shape[0]
    hw = lay["hq"][1]
    heads = hw // HEAD_DIM
    ch = CHUNK
    nc = tx // ch
    cols = [lay[k][0] // hw for k in ("hq", "hf", "hi", "hg")]
    blocks = [((ch, hw), F32)] * 4 + [((heads, HEAD_DIM, HEAD_DIM), F32)] * 2 + [((ch, hw), BF16)]
    scratch = [((heads, HEAD_DIM, HEAD_DIM), F32)]
    in_specs = [pl.BlockSpec((ch, hw), functools.partial(lambda b, c, col: (b * nc + c, col), col=col))
                for col in cols]
    in_specs += [pl.BlockSpec((None, 1, hw), lambda b, c: (layer, 0, 0)),
                 pl.BlockSpec((None, 1, HEAD_DIM), lambda b, c: (layer, 0, 0)),
                 pl.BlockSpec((None, heads, HEAD_DIM, HEAD_DIM), lambda b, c: (b, 0, 0, 0))]
    return pl.pallas_call(
        functools.partial(_hgrn_kernel, heads=heads, t_valid=t_valid),
        grid=(bx, nc),
        in_specs=in_specs,
        out_specs=[pl.BlockSpec((ch, hw), lambda b, c: (b * nc + c, 0)),
                   pl.BlockSpec((None, heads, HEAD_DIM, HEAD_DIM), lambda b, c: (b, 0, 0, 0))],
        out_shape=[jax.ShapeDtypeStruct((n, hw), BF16),
                   jax.ShapeDtypeStruct((bx, heads, HEAD_DIM, HEAD_DIM), F32)],
        scratch_shapes=[pltpu.VMEM(s, t) for s, t in scratch],
        compiler_params=_params(("parallel", "arbitrary"), blocks, scratch),
        name="hgrn2")(proj, proj, proj, proj, lb, hnorm, s0)


def _layout(d_model, fw, gw, hw):
    order = [("gate_a", d_model), ("gate_b", d_model), ("gate_c", d_model), ("fq", fw), ("fk", fw), ("fv", fw),
             ("gqkv", 3 * gw), ("gz", gw), ("hq", hw), ("hf", hw), ("hi", hw), ("hg", hw),
             ("ff", LANES), ("gab", LANES)]
    lay, start = {}, 0
    for name, width in order:
        lay[name] = (start, width)
        start += width
    lay["total"] = (0, start)
    return lay


def _pack_w_in(w_in, lay, d_model, fh, gh, fw, gw, hw):
    splits = [("fq", fw), ("fk", fw), ("fv", fw), ("ff", fh), ("gq", gw), ("gk", gw), ("gv", gw), ("ga", gh),
              ("gb", gh), ("gz", gw), ("hq", hw), ("hf", hw), ("hi", hw), ("hg", hw),
              ("gate_a", d_model), ("gate_b", d_model), ("gate_c", d_model)]
    src, start = {}, 0
    for name, width in splits:
        src[name] = w_in[..., start:start + width].astype(BF16)
        start += width
    assert start == w_in.shape[-1]
    zeros = lambda width: jnp.zeros(w_in.shape[:-1] + (width,), BF16)
    pieces = [src["gate_a"], src["gate_b"], src["gate_c"], src["fq"], src["fk"], src["fv"],
              src["gq"], src["gk"], src["gv"], src["gz"], src["hq"], src["hf"], src["hi"], src["hg"],
              src["ff"], zeros(LANES - fh), src["ga"], src["gb"], zeros(LANES - 2 * gh)]
    packed = jnp.concatenate(pieces, axis=-1)
    assert packed.shape[-1] == lay["total"][1]
    return packed


def _pad_lanes(x, fill=0.0):
    return jnp.pad(x.astype(F32), ((0, 0), (0, LANES - x.shape[-1])), constant_values=fill)[:, None, :]


def _tile(n, candidates):
    for c in candidates:
        if n % c == 0:
            return c
    raise ValueError(f"no tile for {n} among {candidates}")


def kernel(x_prompt, x_sample, cache_fox_k, cache_fox_v, cache_fox_logf, state_gdn, state_gdn_conv, state_hgrn,
           page_table, norm_mix_pre, norm_mix_post, norm_mlp_pre, norm_mlp_post, w_in, fox_f_bias, gdn_conv_w,
           gdn_a_log, gdn_dt_bias, gdn_norm, hgrn_lower_bounds, hgrn_norm, w_fox_o, w_gdn_o, w_hgrn_o, w_out,
           w_up, w_down):
    bp, tp, d = x_prompt.shape
    bs, ts, _ = x_sample.shape
    depth = w_in.shape[0]
    fh, gh = fox_f_bias.shape[1], gdn_a_log.shape[1]
    fw, gw, hw = fh * HEAD_DIM, gh * HEAD_DIM, hgrn_lower_bounds.shape[1]
    hh = hw // HEAD_DIM
    n_pool, page = cache_fox_k.shape[1], cache_fox_k.shape[2]
    n_pages = page_table.shape[1]
    tsp = CHUNK
    assert tp % CHUNK == 0 and tp % LANES == 0 and ts <= tsp and ts >= CONV_W - 1
    assert page * fh == SUBLANES * LANES and 2 * gh <= LANES and fh <= LANES
    lay = _layout(d, fw, gw, hw)
    wp = lay["total"][1]

    w_in_p = _pack_w_in(w_in, lay, d, fh, gh, fw, gw, hw)
    w_fo, w_go, w_ho, w_o, w_u, w_d = (w.astype(BF16) for w in (w_fox_o, w_gdn_o, w_hgrn_o, w_out, w_up, w_down))
    g_mix_pre, g_mix_post, g_mlp_pre, g_mlp_post = (
        g.astype(F32)[:, None, :] for g in (norm_mix_pre, norm_mix_post, norm_mlp_pre, norm_mlp_post))
    f_bias = _pad_lanes(fox_f_bias)
    a_log, dt_bias = _pad_lanes(gdn_a_log), _pad_lanes(gdn_dt_bias)
    g_norm, h_norm = gdn_norm.astype(F32)[:, None, :], hgrn_norm.astype(F32)[:, None, :]
    conv_w = gdn_conv_w.astype(F32)
    lb_all = _lower_bounds(hgrn_lower_bounds)[:, None, :]

    ck = cache_fox_k.reshape(depth * n_pool, page * fh, HEAD_DIM)
    cv = cache_fox_v.reshape(depth * n_pool, page * fh, HEAD_DIM)
    clf = cache_fox_logf.astype(F32).reshape(depth * n_pool, SUBLANES, LANES)
    bias_page = jnp.tile(fox_f_bias.astype(F32), (1, LANES // fh))[:, None, :] * jnp.ones((1, SUBLANES, 1), F32)
    bias_col = jnp.tile(fox_f_bias.astype(F32), (1, ts))[:, :, None]

    zeros_state_p = jnp.zeros((bp, gh, HEAD_DIM, HEAD_DIM), F32)
    zeros_hstate_p = jnp.zeros((bp, hh, HEAD_DIM, HEAD_DIM), F32)
    zeros_conv_p = jnp.zeros((bp, CONV_W - 1, 3 * gw), F32)

    xp = x_prompt.reshape(bp * tp, d)
    xs = jnp.pad(x_sample, ((0, 0), (0, tsp - ts), (0, 0))).reshape(bs * tsp, d)

    tm_p = _tile(bp * tp, (1024, 512, 256, 128, 64))
    tm_s = _tile(bs * tsp, (512, 256, 128, 64))
    tn_in = _tile(wp, (768, 512, 384, 256, 128))
    tq = _tile(tp, FOX_Q_TILES)

    def dense_tail(x, oa, ob, oc, proj, l, tm):
        merged = _merge(oa, ob, oc, proj, lay, w_fo, w_go, w_ho, l, tm=min(tm, 512), tn=_tile(d, (1024, 512, 256)))
        x = _matmul_norm_res(merged, w_o, x, g_mix_post, l, tm=min(tm, 512), tk=_tile(d, (1024, 512, 256)),
                             name="out_proj_norm_res")
        u = _norm_matmul(x, g_mlp_pre, w_u, l, tm=tm, tn=_tile(w_u.shape[-1], (1024, 512, 256)), act="relu2",
                         out_dtype=BF16, name="mlp_up")
        return _matmul_norm_res(u, w_d, x, g_mlp_post, l, tm=min(tm, 512), tk=_tile(w_d.shape[1], (1024, 512, 256)),
                                name="mlp_down_norm_res")

    outs = {k: [] for k in ("fkp", "fvp", "flp", "fks", "fvs", "fls", "gsp", "gss", "gcp", "gcs", "hsp", "hss")}
    for l in range(depth):
        proj = _norm_matmul(xp, g_mix_pre, w_in_p, l, tm=tm_p, tn=tn_in, name="in_proj")
        lf, c_tok, c_row = _fox_gate(proj, lay, f_bias, l, bp, tp)
        oa = _fox_attn(proj, lay, c_tok, c_row, bp, tp, tq=tq)
        ob, gs, gc = _gdn(proj, lay, conv_w, a_log, dt_bias, g_norm, zeros_state_p, zeros_conv_p, l, bp, tp, tp)
        oc, hs = _hgrn(proj, lay, lb_all, h_norm, zeros_hstate_p, l, bp, tp, tp)
        xp = dense_tail(xp, oa, ob, oc, proj, l, tm_p)
        outs["fkp"].append(proj[:, lay["fk"][0]:lay["fk"][0] + fw].reshape(bp, tp, fh, HEAD_DIM))
        outs["fvp"].append(proj[:, lay["fv"][0]:lay["fv"][0] + fw].reshape(bp, tp, fh, HEAD_DIM))
        outs["flp"].append(lf[:, :fh].reshape(bp, tp, fh))
        outs["gsp"].append(gs)
        outs["gcp"].append(gc)
        outs["hsp"].append(hs)

        proj = _norm_matmul(xs, g_mix_pre, w_in_p, l, tm=tm_s, tn=tn_in, name="in_proj")
        p3 = proj.reshape(bs, tsp, wp)[:, :ts]
        q_new, k_new, v_new = (p3[:, :, lay[k][0]:lay[k][0] + fw] for k in ("fq", "fk", "fv"))
        ff_new = p3[:, :, lay["ff"][0]:lay["ff"][0] + fh]
        rows = ts * fh
        pad_rows = lambda z: jnp.pad(z.reshape(bs, rows, HEAD_DIM), ((0, 0), (0, LANES - rows), (0, 0)))
        ff_page = jnp.pad(ff_new.reshape(bs, 1, rows), ((0, 0), (0, SUBLANES - 1), (0, LANES - rows)))
        oa_s, lf_page = _fox_decode(
            q_new.reshape(bs, rows, HEAD_DIM), pad_rows(k_new), pad_rows(v_new), ff_page, bias_page[l],
            ff_new.reshape(bs, rows, 1), bias_col[l], ck, cv, clf,
            (page_table.astype(jnp.int32) + l * n_pool).reshape(-1), n_pages, heads=fh, n_new=ts)
        oa = jnp.pad(oa_s.reshape(bs, ts, fw), ((0, 0), (0, tsp - ts), (0, 0))).reshape(bs * tsp, fw)
        ob, gs, gc = _gdn(proj, lay, conv_w, a_log, dt_bias, g_norm, state_gdn[l].astype(F32),
                          state_gdn_conv[l].astype(F32), l, bs, tsp, ts)
        oc, hs = _hgrn(proj, lay, lb_all, h_norm, state_hgrn[l].astype(F32), l, bs, tsp, ts)
        xs = dense_tail(xs, oa, ob, oc, proj, l, tm_s)
        outs["fks"].append(k_new.reshape(bs, ts, fh, HEAD_DIM))
        outs["fvs"].append(v_new.reshape(bs, ts, fh, HEAD_DIM))
        outs["fls"].append(lf_page[:, 0, :rows].reshape(bs, ts, fh))
        outs["gss"].append(gs)
        outs["gcs"].append(gc)
        outs["hss"].append(hs)

    st = lambda k: jnp.stack(outs[k])
    return (xp.reshape(bp, tp, d), xs.reshape(bs, tsp, d)[:, :ts],
            st("fkp"), st("fvp"), st("flp"), st("fks"), st("fvs"), st("fls"),
            st("gsp"), st("gss"), st("gcp"), st("gcs"), st("hsp"), st("hss"))
```

```python
import functools

import jax
import jax.numpy as jnp
from jax import lax
from jax.experimental import pallas as pl
from jax.experimental.pallas import tpu as pltpu

F32, BF16 = jnp.float32, jnp.bfloat16
HEAD_DIM = 128
CONV_W = 4
CHUNK = 64
EPS = 1e-6
NEG_BIG = -1e30
LB_FLOOR = 1e-30
LANES = 128
SUBLANES = 8
V7X_VMEM_BYTES = 64 * 2**20
VMEM_RESERVE_BYTES = 6 * 2**20
KERNEL_TEMP_BYTES = 16 * 2**20
DECODE_PAGES_PER_STEP = 8
FOX_Q_TILES = (512, 256, 128)


def _nbytes(shape, dtype):
    n = 1
    for s in shape:
        n *= s
    return n * jnp.dtype(dtype).itemsize


def _params(semantics, blocks, scratch=()):
    need = 2 * sum(_nbytes(s, d) for s, d in blocks) + sum(_nbytes(s, d) for s, d in scratch) + KERNEL_TEMP_BYTES
    return pltpu.CompilerParams(dimension_semantics=semantics,
                                vmem_limit_bytes=min(need, V7X_VMEM_BYTES - VMEM_RESERVE_BYTES))


def _dot(a, b):
    return jnp.dot(a, b, preferred_element_type=F32)


def _dot_nt(a, b):
    return lax.dot_general(a, b, (((1,), (1,)), ((), ())), preferred_element_type=F32)


def _split3(x):
    x1 = x.astype(BF16)
    r = x - x1.astype(F32)
    x2 = r.astype(BF16)
    x3 = (r - x2.astype(F32)).astype(BF16)
    return x1, x2, x3


def _dot01(m01, x):
    x1, x2, x3 = _split3(x)
    return _dot(m01, x1) + (_dot(m01, x2) + _dot(m01, x3))


def _log_sigmoid(x):
    return -(jnp.maximum(-x, 0.0) + jnp.log1p(jnp.exp(-jnp.abs(x))))


def _softplus(x):
    return jnp.maximum(x, 0.0) + jnp.log1p(jnp.exp(-jnp.abs(x)))


def _silu(x):
    return x * jax.nn.sigmoid(x)


def _tril01(n, strict=False):
    r = lax.broadcasted_iota(jnp.int32, (n, n), 0)
    c = lax.broadcasted_iota(jnp.int32, (n, n), 1)
    return (r > c) if strict else (r >= c)


def _lower_bounds_kernel(x_ref, o_ref):
    x = x_ref[...]
    e = jnp.exp(x - jnp.max(x, axis=0, keepdims=True))
    p = e / jnp.sum(e, axis=0, keepdims=True)
    run = jnp.zeros_like(p[0:1])
    for l in range(x.shape[0]):
        run = run + p[l:l + 1]
        o_ref[l:l + 1, :] = run - p[0:1]


def _lower_bounds(x):
    return pl.pallas_call(_lower_bounds_kernel, out_shape=jax.ShapeDtypeStruct(x.shape, F32),
                          name="hgrn_lower_bounds")(x.astype(F32))


def _norm_matmul_kernel(x_ref, g_ref, w_ref, o_ref, xn_ref, *, act):
    @pl.when(pl.program_id(1) == 0)
    def _():
        x = x_ref[...]
        var = jnp.mean(x * x, axis=-1, keepdims=True)
        xn_ref[...] = ((x * lax.rsqrt(var + EPS)) * g_ref[...]).astype(BF16)

    y = _dot(xn_ref[...], w_ref[...])
    if act == "relu2":
        y = jnp.maximum(y, 0.0)
        y = y * y
    o_ref[...] = y.astype(o_ref.dtype)


def _norm_matmul(x, g, w, layer, *, tm, tn, act=None, out_dtype=F32, name):
    n, d = x.shape
    wn = w.shape[-1]
    blocks = [((tm, d), F32), ((1, d), F32), ((d, tn), BF16), ((tm, tn), out_dtype)]
    scratch = [((tm, d), BF16)]
    return pl.pallas_call(
        functools.partial(_norm_matmul_kernel, act=act),
        grid=(n // tm, wn // tn),
        in_specs=[pl.BlockSpec((tm, d), lambda i, j: (i, 0)),
                  pl.BlockSpec((None, 1, d), lambda i, j: (layer, 0, 0)),
                  pl.BlockSpec((None, d, tn), lambda i, j: (layer, 0, j))],
        out_specs=pl.BlockSpec((tm, tn), lambda i, j: (i, j)),
        out_shape=jax.ShapeDtypeStruct((n, wn), out_dtype),
        scratch_shapes=[pltpu.VMEM(s, t) for s, t in scratch],
        compiler_params=_params(("parallel", "arbitrary"), blocks, scratch),
        name=name)(x, g, w)


def _matmul_norm_res_kernel(a_ref, w_ref, x_ref, g_ref, o_ref, acc_ref):
    k = pl.program_id(1)

    @pl.when(k == 0)
    def _():
        acc_ref[...] = jnp.zeros_like(acc_ref)

    acc_ref[...] += _dot(a_ref[...], w_ref[...])

    @pl.when(k == pl.num_programs(1) - 1)
    def _():
        y = acc_ref[...]
        var = jnp.mean(y * y, axis=-1, keepdims=True)
        o_ref[...] = x_ref[...] + (y * lax.rsqrt(var + EPS)) * g_ref[...]


def _matmul_norm_res(a, w, x, g, layer, *, tm, tk, name):
    n, kdim = a.shape
    d = x.shape[1]
    blocks = [((tm, tk), BF16), ((tk, d), BF16), ((tm, d), F32), ((1, d), F32), ((tm, d), F32)]
    scratch = [((tm, d), F32)]
    return pl.pallas_call(
        _matmul_norm_res_kernel,
        grid=(n // tm, kdim // tk),
        in_specs=[pl.BlockSpec((tm, tk), lambda i, k: (i, k)),
                  pl.BlockSpec((None, tk, d), lambda i, k: (layer, k, 0)),
                  pl.BlockSpec((tm, d), lambda i, k: (i, 0)),
                  pl.BlockSpec((None, 1, d), lambda i, k: (layer, 0, 0))],
        out_specs=pl.BlockSpec((tm, d), lambda i, k: (i, 0)),
        out_shape=jax.ShapeDtypeStruct((n, d), F32),
        scratch_shapes=[pltpu.VMEM(s, t) for s, t in scratch],
        compiler_params=_params(("parallel", "arbitrary"), blocks, scratch),
        name=name)(a, w, x, g)


def _merge_kernel(oa_ref, ob_ref, oc_ref, ga_ref, gb_ref, gc_ref, wa_ref, wb_ref, wc_ref, o_ref):
    m = jax.nn.sigmoid(ga_ref[...]) * _dot(oa_ref[...], wa_ref[...])
    m = m + jax.nn.sigmoid(gb_ref[...]) * _dot(ob_ref[...], wb_ref[...])
    m = m + jax.nn.sigmoid(gc_ref[...]) * _dot(oc_ref[...], wc_ref[...])
    o_ref[...] = m.astype(o_ref.dtype)


def _merge(oa, ob, oc, proj, lay, wa, wb, wc, layer, *, tm, tn):
    n = oa.shape[0]
    d = wa.shape[-1]
    col = {k: lay[k][0] // tn for k in ("gate_a", "gate_b", "gate_c")}
    o_specs = [pl.BlockSpec((tm, o.shape[1]), lambda i, j: (i, 0)) for o in (oa, ob, oc)]
    g_specs = [pl.BlockSpec((tm, tn), functools.partial(lambda i, j, c: (i, c + j), c=col[k]))
               for k in ("gate_a", "gate_b", "gate_c")]
    w_specs = [pl.BlockSpec((None, w.shape[1], tn), lambda i, j: (layer, 0, j)) for w in (wa, wb, wc)]
    blocks = ([((tm, o.shape[1]), BF16) for o in (oa, ob, oc)] + [((tm, tn), F32)] * 3
              + [((w.shape[1], tn), BF16) for w in (wa, wb, wc)] + [((tm, tn), BF16)])
    return pl.pallas_call(
        _merge_kernel,
        grid=(n // tm, d // tn),
        in_specs=o_specs + g_specs + w_specs,
        out_specs=pl.BlockSpec((tm, tn), lambda i, j: (i, j)),
        out_shape=jax.ShapeDtypeStruct((n, d), BF16),
        compiler_params=_params(("parallel", "parallel"), blocks),
        name="gated_merge")(oa, ob, oc, proj, proj, proj, wa, wb, wc)


def _fox_gate_kernel(ff_ref, b_ref, lf_ref, c_ref, ct_ref):
    t = ff_ref.shape[0]
    lf_ref[...] = _log_sigmoid(ff_ref[...] + b_ref[...])
    tri = _tril01(LANES).astype(BF16)
    carry = jnp.zeros((1, LANES), F32)
    for blk in range(t // LANES):
        rows = slice(blk * LANES, (blk + 1) * LANES)
        cb = _dot01(tri, lf_ref[rows, :]) + carry
        c_ref[rows, :] = cb
        ct_ref[:, rows] = cb.T
        carry = cb[LANES - 1:LANES, :]


def _fox_gate(proj, lay, bias, layer, bx, tx):
    n = proj.shape[0]
    col = lay["ff"][0] // LANES
    blocks = [((tx, LANES), F32)] * 4 + [((LANES, tx), F32)]
    return pl.pallas_call(
        _fox_gate_kernel,
        grid=(bx,),
        in_specs=[pl.BlockSpec((tx, LANES), lambda b: (b, col)),
                  pl.BlockSpec((None, 1, LANES), lambda b: (layer, 0, 0))],
        out_specs=[pl.BlockSpec((tx, LANES), lambda b: (b, 0)),
                   pl.BlockSpec((tx, LANES), lambda b: (b, 0)),
                   pl.BlockSpec((None, LANES, tx), lambda b: (b, 0, 0))],
        out_shape=[jax.ShapeDtypeStruct((n, LANES), F32), jax.ShapeDtypeStruct((n, LANES), F32),
                   jax.ShapeDtypeStruct((bx, LANES, tx), F32)],
        compiler_params=_params(("parallel",), blocks),
        name="fox_gate")(proj, bias)


def _fox_attn_kernel(q_ref, k_ref, v_ref, ct_ref, cr_ref, o_ref, m_ref, l_ref, acc_ref, *, heads, scale):
    i, j = pl.program_id(1), pl.program_id(2)
    tq, tk = q_ref.shape[0], k_ref.shape[0]

    @pl.when(j == 0)
    def _():
        m_ref[...] = jnp.full_like(m_ref, NEG_BIG)
        l_ref[...] = jnp.zeros_like(l_ref)
        acc_ref[...] = jnp.zeros_like(acc_ref)

    @pl.when(j <= i)
    def _():
        q_pos = i * tq + lax.broadcasted_iota(jnp.int32, (tq, tk), 0)
        k_pos = j * tk + lax.broadcasted_iota(jnp.int32, (tq, tk), 1)
        causal = q_pos >= k_pos
        for h in range(heads):
            hs = slice(h * HEAD_DIM, (h + 1) * HEAD_DIM)
            s = _dot_nt(q_ref[:, hs].astype(BF16), k_ref[:, hs].astype(BF16)) * scale
            s = s + ct_ref[:, h:h + 1] - cr_ref[h:h + 1, :]
            s = jnp.where(causal, s, NEG_BIG)
            m_prev = m_ref[:, h:h + 1]
            m_new = jnp.maximum(m_prev, jnp.max(s, axis=-1, keepdims=True))
            alpha = jnp.exp(m_prev - m_new)
            p = jnp.exp(s - m_new)
            l_ref[:, h:h + 1] = alpha * l_ref[:, h:h + 1] + jnp.sum(p, axis=-1, keepdims=True)
            acc_ref[:, hs] = alpha * acc_ref[:, hs] + _dot(p.astype(BF16), v_ref[:, hs].astype(BF16))
            m_ref[:, h:h + 1] = m_new

    @pl.when(j == i)
    def _():
        for h in range(heads):
            hs = slice(h * HEAD_DIM, (h + 1) * HEAD_DIM)
            o_ref[:, hs] = (acc_ref[:, hs] / l_ref[:, h:h + 1]).astype(o_ref.dtype)


def _fox_attn(proj, lay, c_tok, c_row, bx, tx, *, tq):
    n = proj.shape[0]
    fw = lay["fq"][1]
    heads = fw // HEAD_DIM
    nq = tx // tq
    qc, kc, vc = (lay[k][0] // fw for k in ("fq", "fk", "fv"))
    blocks = [((tq, fw), F32)] * 3 + [((tq, LANES), F32), ((LANES, tq), F32), ((tq, fw), BF16)]
    scratch = [((tq, LANES), F32), ((tq, LANES), F32), ((tq, fw), F32)]
    return pl.pallas_call(
        functools.partial(_fox_attn_kernel, heads=heads, scale=HEAD_DIM ** -0.5),
        grid=(bx, nq, nq),
        in_specs=[pl.BlockSpec((tq, fw), lambda b, i, j: (b * nq + i, qc)),
                  pl.BlockSpec((tq, fw), lambda b, i, j: (b * nq + jnp.minimum(j, i), kc)),
                  pl.BlockSpec((tq, fw), lambda b, i, j: (b * nq + jnp.minimum(j, i), vc)),
                  pl.BlockSpec((tq, LANES), lambda b, i, j: (b * nq + i, 0)),
                  pl.BlockSpec((None, LANES, tq), lambda b, i, j: (b, 0, jnp.minimum(j, i)))],
        out_specs=pl.BlockSpec((tq, fw), lambda b, i, j: (b * nq + i, 0)),
        out_shape=jax.ShapeDtypeStruct((n, fw), BF16),
        scratch_shapes=[pltpu.VMEM(s, t) for s, t in scratch],
        compiler_params=_params(("parallel", "parallel", "arbitrary"), blocks, scratch),
        name="fox_attn_prompt")(proj, proj, proj, c_tok, c_row)


def _page_suffix(x, lane, sub, heads):
    y = x
    sh = heads
    while sh < LANES:
        y = y + jnp.where(lane < LANES - sh, pltpu.roll(y, LANES - sh, axis=1), 0.0)
        sh *= 2
    t = jnp.where(lane < heads, y, 0.0)
    sh = heads
    while sh < LANES:
        t = t + pltpu.roll(t, sh, axis=1)
        sh *= 2
    z = t
    sh = 1
    while sh < SUBLANES:
        z = z + jnp.where(sub < SUBLANES - sh, pltpu.roll(z, SUBLANES - sh, axis=0), 0.0)
        sh *= 2
    return (y - x) + (z - t), z[0:1, :]


def _fox_decode_kernel(pt_ref, q_ref, kn_ref, vn_ref, ffp_ref, bp_ref, ffc_ref, bc_ref, *rest,
                       pages, heads, n_new, scale):
    del pt_ref
    k_refs, v_refs, lf_refs = rest[:pages], rest[pages:2 * pages], rest[2 * pages:3 * pages]
    o_ref, lfo_ref, m_ref, l_ref, acc_ref, carry_ref, rn_ref = rest[3 * pages:]
    step = pl.program_id(1)
    rows = q_ref.shape[0]
    lane = lax.broadcasted_iota(jnp.int32, (SUBLANES, LANES), 1)
    sub = lax.broadcasted_iota(jnp.int32, (SUBLANES, LANES), 0)
    r_i = lax.broadcasted_iota(jnp.int32, (rows, LANES), 0)
    c_i = lax.broadcasted_iota(jnp.int32, (rows, LANES), 1)
    head_ok = (r_i % heads) == (c_i % heads)
    qb = q_ref[...].astype(BF16)

    def update(scores, values):
        m_prev = m_ref[...]
        m_new = m_prev
        for s in scores:
            m_new = jnp.maximum(m_new, jnp.max(s, axis=-1, keepdims=True))
        alpha = jnp.exp(m_prev - m_new)
        l_new = alpha * l_ref[...]
        acc = alpha * acc_ref[...]
        for s, v in zip(scores, values):
            p = jnp.exp(s - m_new)
            l_new = l_new + jnp.sum(p, axis=-1, keepdims=True)
            acc = acc + _dot(p.astype(BF16), v)
        m_ref[...] = m_new
        l_ref[...] = l_new
        acc_ref[...] = acc

    @pl.when(step == 0)
    def _():
        m_ref[...] = jnp.full_like(m_ref, NEG_BIG)
        l_ref[...] = jnp.zeros_like(l_ref)
        acc_ref[...] = jnp.zeros_like(acc_ref)
        new_ok = (sub == 0) & (lane < n_new * heads)
        lf = jnp.where(new_ok, _log_sigmoid(ffp_ref[...] + bp_ref[...]), 0.0)
        lfo_ref[...] = lf
        gate, total = _page_suffix(lf, lane, sub, heads)
        carry_ref[...] = total
        lfc = _log_sigmoid(ffc_ref[...] + bc_ref[...])
        run = jnp.zeros((heads, 1), F32)
        for t in range(n_new - 1, -1, -1):
            rn_ref[t * heads:(t + 1) * heads, :] = run
            run = run + lfc[t * heads:(t + 1) * heads, :]
        s = _dot_nt(qb, kn_ref[...].astype(BF16)) * scale + gate[0:1, :] - rn_ref[...]
        ok = head_ok & (c_i // heads <= r_i // heads) & (c_i < n_new * heads)
        update([jnp.where(ok, s, NEG_BIG)], [vn_ref[...].astype(BF16)])

    carry = carry_ref[...]
    rn = rn_ref[...]
    scores = []
    for r in range(pages):
        gate, total = _page_suffix(lf_refs[r][...], lane, sub, heads)
        gate = gate + carry
        carry = carry + total
        s_page = _dot_nt(qb, k_refs[r][...].astype(BF16)) * scale
        scores.append([jnp.where(head_ok, s_page[:, c * LANES:(c + 1) * LANES] + gate[c:c + 1, :] - rn, NEG_BIG)
                       for c in range(SUBLANES)])
    carry_ref[...] = carry
    m_prev = m_ref[...]
    m_new = m_prev
    for page_scores in scores:
        for s in page_scores:
            m_new = jnp.maximum(m_new, jnp.max(s, axis=-1, keepdims=True))
    alpha = jnp.exp(m_prev - m_new)
    l_new = alpha * l_ref[...]
    acc = alpha * acc_ref[...]
    for r in range(pages):
        probs = [jnp.exp(s - m_new) for s in scores[r]]
        for p in probs:
            l_new = l_new + jnp.sum(p, axis=-1, keepdims=True)
        acc = acc + _dot(jnp.concatenate(probs, axis=-1).astype(BF16), v_refs[r][...].astype(BF16))
    m_ref[...] = m_new
    l_ref[...] = l_new
    acc_ref[...] = acc

    @pl.when(step == pl.num_programs(1) - 1)
    def _():
        o_ref[...] = (acc_ref[...] / l_ref[...]).astype(o_ref.dtype)


def _fox_decode(q, k_new, v_new, ff_page, bias_page, ff_col, bias_col, cache_k, cache_v, cache_lf, pages_flat,
                n_pages, *, heads, n_new):
    bs, rows, _ = q.shape
    page_rows = cache_k.shape[1]
    assert page_rows == SUBLANES * LANES and heads * (LANES // heads) == LANES and n_pages % DECODE_PAGES_PER_STEP == 0
    pages = DECODE_PAGES_PER_STEP
    n_steps = n_pages // pages

    def page_map(r):
        return lambda b, s, pt: (pt[b * n_pages + (n_pages - 1 - (s * pages + r))], 0, 0)

    in_specs = [pl.BlockSpec((None, rows, HEAD_DIM), lambda b, s, pt: (b, 0, 0)),
                pl.BlockSpec((None, LANES, HEAD_DIM), lambda b, s, pt: (b, 0, 0)),
                pl.BlockSpec((None, LANES, HEAD_DIM), lambda b, s, pt: (b, 0, 0)),
                pl.BlockSpec((None, SUBLANES, LANES), lambda b, s, pt: (b, 0, 0)),
                pl.BlockSpec((SUBLANES, LANES), lambda b, s, pt: (0, 0)),
                pl.BlockSpec((None, rows, 1), lambda b, s, pt: (b, 0, 0)),
                pl.BlockSpec((rows, 1), lambda b, s, pt: (0, 0))]
    in_specs += [pl.BlockSpec((None, page_rows, HEAD_DIM), page_map(r)) for r in range(pages)] * 2
    in_specs += [pl.BlockSpec((None, SUBLANES, LANES), page_map(r)) for r in range(pages)]
    blocks = [((page_rows, HEAD_DIM), F32)] * (2 * pages) + [((LANES, HEAD_DIM), F32)] * 4
    scratch = [((rows, 1), F32), ((rows, 1), F32), ((rows, HEAD_DIM), F32), ((1, LANES), F32), ((rows, 1), F32)]
    grid_spec = pltpu.PrefetchScalarGridSpec(
        num_scalar_prefetch=1, grid=(bs, n_steps), in_specs=in_specs,
        out_specs=[pl.BlockSpec((None, rows, HEAD_DIM), lambda b, s, pt: (b, 0, 0)),
                   pl.BlockSpec((None, SUBLANES, LANES), lambda b, s, pt: (b, 0, 0))],
        scratch_shapes=[pltpu.VMEM(s, t) for s, t in scratch])
    return pl.pallas_call(
        functools.partial(_fox_decode_kernel, pages=pages, heads=heads, n_new=n_new, scale=HEAD_DIM ** -0.5),
        grid_spec=grid_spec,
        out_shape=[jax.ShapeDtypeStruct((bs, rows, HEAD_DIM), BF16),
                   jax.ShapeDtypeStruct((bs, SUBLANES, LANES), F32)],
        compiler_params=_params(("parallel", "arbitrary"), blocks, scratch),
        name="fox_decode")(pages_flat, q, k_new, v_new, ff_page, bias_page, ff_col, bias_col,
                           *([cache_k] * pages), *([cache_v] * pages), *([cache_lf] * pages))


def _dot_hi(a, b):
    a_hi = a.astype(BF16).astype(F32)
    a16 = jnp.concatenate([a_hi, a - a_hi], axis=1).astype(BF16)
    b_hi = b.astype(BF16)
    b_lo = (b - b_hi.astype(F32)).astype(BF16)
    return _dot(jnp.concatenate([a16, a16], axis=1), jnp.concatenate([b_hi, b_hi, b_lo, b_lo], axis=0))


def _unit_lower_solves(ms, rhss):
    c = ms[0].shape[0]
    assert 2 * c == LANES
    xs = [r - _dot_hi(m, r) for m, r in zip(ms, rhss)]
    ps, power = list(ms), 1
    while 2 * power < c:
        ps = [_dot_hi(p, p) for p in ps]
        power *= 2
        xs = [x + _dot_hi(p, x) for p, x in zip(ps, xs)]
    return xs


def _gdn_prep_kernel(qkv_ref, ab_ref, cw_ref, al_ref, dt_ref, cb0_ref,
                     u_ref, w_ref, qe_ref, kdt_ref, attn_ref, egl_ref, cb_out_ref, xe_ref, *, heads, t_valid):
    c = pl.program_id(1)
    ch = qkv_ref.shape[0]
    width = heads * HEAD_DIM
    tail = CONV_W - 1
    base = SUBLANES
    last_chunk = (t_valid - 1) // ch
    last_len = t_valid - last_chunk * ch

    @pl.when(c == 0)
    def _():
        xe_ref[base - tail:base, :] = cb0_ref[...]

    xe_ref[base:base + ch, :] = qkv_ref[...]
    y = xe_ref[base - tail:base - tail + ch, :] * cw_ref[0:1, :]
    for i in range(1, CONV_W):
        y = y + xe_ref[base - tail + i:base - tail + i + ch, :] * cw_ref[i:i + 1, :]
    y = _silu(y)

    @pl.when(c == last_chunk)
    def _():
        cb_out_ref[...] = xe_ref[base + last_len - tail:base + last_len, :]

    xe_ref[base - tail:base, :] = xe_ref[base + ch - tail:base + ch, :]

    valid = (c * ch + lax.broadcasted_iota(jnp.int32, (ch, 1), 0)) < t_valid
    ab = ab_ref[...]
    g_all = jnp.where(valid, -jnp.exp(al_ref[...]) * _softplus(ab + dt_ref[...]), 0.0)
    beta_all = jnp.where(valid, jax.nn.sigmoid(ab), 0.0)
    cum = _dot01(_tril01(ch).astype(BF16), g_all)
    cum_t = cum.T
    lower = _tril01(ch)
    strict = _tril01(ch, strict=True)
    egl_ref[...] = jnp.exp(cum[ch - 1:ch, :])

    hr = range(heads)
    sl = lambda part, h: slice(part * width + h * HEAD_DIM, part * width + (h + 1) * HEAD_DIM)
    gc = [cum[:, h:h + 1] for h in hr]
    beta = [beta_all[:, heads + h:heads + h + 1] for h in hr]
    decay = [jnp.exp(jnp.where(lower, gc[h] - cum_t[h:h + 1, :], NEG_BIG)) for h in hr]
    q = [y[:, sl(0, h)] for h in hr]
    k = [y[:, sl(1, h)] for h in hr]
    q = [x * lax.rsqrt(jnp.sum(x * x, axis=-1, keepdims=True) + EPS) * (HEAD_DIM ** -0.5) for x in q]
    k = [x * lax.rsqrt(jnp.sum(x * x, axis=-1, keepdims=True) + EPS) for x in k]
    kb = [k[h] * beta[h] for h in hr]
    k16 = [x.astype(BF16) for x in k]
    eg = [jnp.exp(gc[h]) for h in hr]
    sols = _unit_lower_solves(
        [jnp.where(strict, _dot_nt(kb[h].astype(BF16), k16[h]) * decay[h], 0.0) for h in hr],
        [jnp.concatenate([y[:, sl(2, h)] * beta[h], kb[h] * eg[h]], axis=-1) for h in hr])
    for h in hr:
        sol = sols[h]
        u_ref[:, sl(0, h)] = sol[:, :HEAD_DIM]
        w_ref[:, sl(0, h)] = sol[:, HEAD_DIM:].astype(BF16)
        qe_ref[:, sl(0, h)] = (q[h] * eg[h]).astype(BF16)
        attn_ref[h] = (_dot_nt(q[h].astype(BF16), k16[h]) * decay[h]).astype(BF16)
        kdt_ref[h] = (k[h] * jnp.exp(cum[ch - 1:ch, h:h + 1] - gc[h])).T.astype(BF16)


def _gdn_scan_kernel(u_ref, w_ref, qe_ref, kdt_ref, attn_ref, egl_ref, gz_ref, gn_ref, s0_ref,
                     o_ref, s_out_ref, st_ref, *, heads):
    c = pl.program_id(0)
    bx = u_ref.shape[0]

    @pl.when(c == 0)
    def _():
        st_ref[...] = s0_ref[...]

    chains = [(b, h) for b in range(bx) for h in range(heads)]
    hs = lambda h: slice(h * HEAD_DIM, (h + 1) * HEAD_DIM)
    s_old = [st_ref[b, h] for b, h in chains]
    s16 = [s.astype(BF16) for s in s_old]
    ws = [_dot(w_ref[b, :, hs(h)], s) for (b, h), s in zip(chains, s16)]
    qs = [_dot(qe_ref[b, :, hs(h)], s) for (b, h), s in zip(chains, s16)]
    vn16 = [(u_ref[b, :, hs(h)] - x).astype(BF16) for (b, h), x in zip(chains, ws)]
    for i, (b, h) in enumerate(chains):
        o = qs[i] + _dot(attn_ref[b, h], vn16[i])
        st_ref[b, h] = s_old[i] * egl_ref[b, :, h:h + 1] + _dot(kdt_ref[b, h], vn16[i])
        on = (o * lax.rsqrt(jnp.mean(o * o, axis=-1, keepdims=True) + EPS)) * gn_ref[...]
        o_ref[b, :, hs(h)] = (on * _silu(gz_ref[b, :, hs(h)])).astype(o_ref.dtype)

    @pl.when(c == pl.num_programs(0) - 1)
    def _():
        s_out_ref[...] = st_ref[...]


def _gdn(proj, lay, conv_w, a_log, dt_bias, gnorm, s0, cb0, layer, bx, tx, t_valid):
    n, wp = proj.shape
    gw = lay["gz"][1]
    heads = gw // HEAD_DIM
    ch = CHUNK
    nc = tx // ch
    qkv_c, ab_c, gz_c = lay["gqkv"][0] // (3 * gw), lay["gab"][0] // LANES, lay["gz"][0] // gw
    blocks = [((ch, 3 * gw), F32), ((ch, LANES), F32), ((CONV_W, 3 * gw), F32), ((CONV_W - 1, 3 * gw), F32),
              ((ch, gw), F32), ((ch, gw), BF16), ((ch, gw), BF16), ((heads, HEAD_DIM, LANES), BF16),
              ((heads, ch, LANES), BF16), ((CONV_W - 1, 3 * gw), F32)]
    scratch = [((SUBLANES + ch, 3 * gw), F32)]
    u, w16, qe16, kdt, attn, egl, cb_out = pl.pallas_call(
        functools.partial(_gdn_prep_kernel, heads=heads, t_valid=t_valid),
        grid=(bx, nc),
        in_specs=[pl.BlockSpec((ch, 3 * gw), lambda b, c: (b * nc + c, qkv_c)),
                  pl.BlockSpec((ch, LANES), lambda b, c: (b * nc + c, ab_c)),
                  pl.BlockSpec((None, CONV_W, 3 * gw), lambda b, c: (layer, 0, 0)),
                  pl.BlockSpec((None, 1, LANES), lambda b, c: (layer, 0, 0)),
                  pl.BlockSpec((None, 1, LANES), lambda b, c: (layer, 0, 0)),
                  pl.BlockSpec((None, CONV_W - 1, 3 * gw), lambda b, c: (b, 0, 0))],
        out_specs=[pl.BlockSpec((ch, gw), lambda b, c: (b * nc + c, 0)),
                   pl.BlockSpec((ch, gw), lambda b, c: (b * nc + c, 0)),
                   pl.BlockSpec((ch, gw), lambda b, c: (b * nc + c, 0)),
                   pl.BlockSpec((None, None, heads, HEAD_DIM, ch), lambda b, c: (b, c, 0, 0, 0)),
                   pl.BlockSpec((None, None, heads, ch, ch), lambda b, c: (b, c, 0, 0, 0)),
                   pl.BlockSpec((None, None, 1, LANES), lambda b, c: (b, c, 0, 0)),
                   pl.BlockSpec((None, CONV_W - 1, 3 * gw), lambda b, c: (b, 0, 0))],
        out_shape=[jax.ShapeDtypeStruct((n, gw), F32), jax.ShapeDtypeStruct((n, gw), BF16),
                   jax.ShapeDtypeStruct((n, gw), BF16),
                   jax.ShapeDtypeStruct((bx, nc, heads, HEAD_DIM, ch), BF16),
                   jax.ShapeDtypeStruct((bx, nc, heads, ch, ch), BF16),
                   jax.ShapeDtypeStruct((bx, nc, 1, LANES), F32),
                   jax.ShapeDtypeStruct((bx, CONV_W - 1, 3 * gw), F32)],
        scratch_shapes=[pltpu.VMEM(s, t) for s, t in scratch],
        compiler_params=_params(("parallel", "arbitrary"), blocks, scratch),
        name="gdn_prep")(proj, proj, conv_w, a_log, dt_bias, cb0)

    state = (bx, heads, HEAD_DIM, HEAD_DIM)
    blocks = [((bx, ch, gw), F32), ((bx, ch, gw), BF16), ((bx, ch, gw), BF16), ((bx, heads, HEAD_DIM, LANES), BF16),
              ((bx, heads, ch, LANES), BF16), ((bx, ch, gw), F32), (state, F32), ((bx, ch, gw), BF16), (state, F32)]
    scratch = [(state, F32)]
    rows3 = lambda a: a.reshape(bx, tx, a.shape[-1])
    o, s_out = pl.pallas_call(
        functools.partial(_gdn_scan_kernel, heads=heads),
        grid=(nc,),
        in_specs=[pl.BlockSpec((bx, ch, gw), lambda c: (0, c, 0)),
                  pl.BlockSpec((bx, ch, gw), lambda c: (0, c, 0)),
                  pl.BlockSpec((bx, ch, gw), lambda c: (0, c, 0)),
                  pl.BlockSpec((bx, None, heads, HEAD_DIM, ch), lambda c: (0, c, 0, 0, 0)),
                  pl.BlockSpec((bx, None, heads, ch, ch), lambda c: (0, c, 0, 0, 0)),
                  pl.BlockSpec((bx, None, 1, LANES), lambda c: (0, c, 0, 0)),
                  pl.BlockSpec((bx, ch, gw), lambda c: (0, c, gz_c)),
                  pl.BlockSpec((None, 1, HEAD_DIM), lambda c: (layer, 0, 0)),
                  pl.BlockSpec(state, lambda c: (0, 0, 0, 0))],
        out_specs=[pl.BlockSpec((bx, ch, gw), lambda c: (0, c, 0)),
                   pl.BlockSpec(state, lambda c: (0, 0, 0, 0))],
        out_shape=[jax.ShapeDtypeStruct((bx, tx, gw), BF16), jax.ShapeDtypeStruct(state, F32)],
        scratch_shapes=[pltpu.VMEM(s, t) for s, t in scratch],
        compiler_params=_params(("arbitrary",), blocks, scratch),
        name="gdn_scan")(rows3(u), rows3(w16), rows3(qe16), kdt, attn, egl, rows3(proj), gnorm, s0)
    return o.reshape(n, gw), s_out, cb_out


def _hgrn_kernel(hq_ref, hf_ref, hi_ref, hg_ref, lb_ref, hn_ref, s0_ref, o_ref, s_out_ref, st_ref,
                 *, heads, t_valid):
    c = pl.program_id(1)
    ch = hq_ref.shape[0]
    nblk = ch // SUBLANES

    @pl.when(c == 0)
    def _():
        for h in range(heads):
            st_ref[h] = s0_ref[h].T

    tok = lax.broadcasted_iota(jnp.int32, (ch, 1), 0)
    valid = (c * ch + tok) < t_valid
    lb = lb_ref[...]
    hf = hf_ref[...]
    a = jnp.log(jnp.maximum(lb, LB_FLOOR))
    b = jnp.log1p(-lb) + _log_sigmoid(hf)
    log_f = jnp.maximum(a, b) + jnp.log1p(jnp.exp(-jnp.abs(a - b)))
    log_f = jnp.where(valid, log_f, 0.0)
    k_all = jnp.where(valid, (1.0 - lb) * jax.nn.sigmoid(-hf), 0.0)
    cum_all = _dot01(_tril01(ch).astype(BF16), log_f)
    q_all = _silu(hq_ref[...])

    ti = lax.broadcasted_iota(jnp.int32, (ch, ch), 0)
    tj = lax.broadcasted_iota(jnp.int32, (ch, ch), 1)
    sub3 = lax.broadcasted_iota(jnp.int32, (nblk, SUBLANES, HEAD_DIM), 1)

    for h in range(heads):
        hs = slice(h * HEAD_DIM, (h + 1) * HEAD_DIM)
        g, q, k, v = cum_all[:, hs], q_all[:, hs], k_all[:, hs], hi_ref[:, hs]
        attn_t = jnp.zeros((ch, ch), F32)
        half = ch // 2
        while half >= SUBLANES:
            blk = 2 * half
            g3 = g.reshape(ch // blk, blk, HEAD_DIM)
            e = jnp.exp(-jnp.abs(g3 - g3[:, half - 1:half, :])).reshape(ch, HEAD_DIM)
            upper = (tok % blk) >= half
            qa = jnp.where(upper, q * e, 0.0).astype(BF16)
            kb = jnp.where(upper, 0.0, k * e).astype(BF16)
            attn_t = attn_t + jnp.where((ti // blk) == (tj // blk), _dot_nt(kb, qa), 0.0)
            half //= 2
        g3, q3, k3 = (z.reshape(nblk, SUBLANES, HEAD_DIM) for z in (g, q, k))
        cols = []
        for i in range(SUBLANES):
            e = jnp.exp(jnp.where(sub3 <= i, g3[:, i:i + 1, :] - g3, NEG_BIG))
            w = jnp.sum(e * k3 * q3[:, i:i + 1, :], axis=-1, keepdims=True).reshape(ch, 1)
            cols.append(jnp.where((tj % SUBLANES) == i, w, 0.0))
        while len(cols) > 1:
            cols = [a + b for a, b in zip(cols[0::2], cols[1::2])]
        attn_t = attn_t + jnp.where((ti // SUBLANES) == (tj // SUBLANES), cols[0], 0.0)
        st = st_ref[h]
        o = _dot_nt((q * jnp.exp(g)).astype(BF16), st.astype(BF16)) + _dot(attn_t.T.astype(BF16), v.astype(BF16))
        gl = g[ch - 1:ch, :]
        kd = k * jnp.exp(gl - g)
        st_ref[h] = st * jnp.exp(gl) + _dot(v.T.astype(BF16), kd.astype(BF16))
        on = (o * lax.rsqrt(jnp.mean(o * o, axis=-1, keepdims=True) + EPS)) * hn_ref[...]
        o_ref[:, hs] = (on * _silu(hg_ref[:, hs])).astype(o_ref.dtype)

    @pl.when(c == pl.num_programs(1) - 1)
    def _():
        for h in range(heads):
            s_out_ref[h] = st_ref[h].T


def _hgrn(proj, lay, lb, hnorm, s0, layer, bx, tx, t_valid):
    n = proj.shape[0]
    hw = lay["hq"][1]
    heads = hw // HEAD_DIM
    ch = CHUNK
    nc = tx // ch
    cols = [lay[k][0] // hw for k in ("hq", "hf", "hi", "hg")]
    blocks = [((ch, hw), F32)] * 4 + [((heads, HEAD_DIM, HEAD_DIM), F32)] * 2 + [((ch, hw), BF16)]
    scratch = [((heads, HEAD_DIM, HEAD_DIM), F32)]
    in_specs = [pl.BlockSpec((ch, hw), functools.partial(lambda b, c, col: (b * nc + c, col), col=col))
                for col in cols]
    in_specs += [pl.BlockSpec((None, 1, hw), lambda b, c: (layer, 0, 0)),
                 pl.BlockSpec((None, 1, HEAD_DIM), lambda b, c: (layer, 0, 0)),
                 pl.BlockSpec((None, heads, HEAD_DIM, HEAD_DIM), lambda b, c: (b, 0, 0, 0))]
    return pl.pallas_call(
        functools.partial(_hgrn_kernel, heads=heads, t_valid=t_valid),
        grid=(bx, nc),
        in_specs=in_specs,
        out_specs=[pl.BlockSpec((ch, hw), lambda b, c: (b * nc + c, 0)),
                   pl.BlockSpec((None, heads, HEAD_DIM, HEAD_DIM), lambda b, c: (b, 0, 0, 0))],
        out_shape=[jax.ShapeDtypeStruct((n, hw), BF16),
                   jax.ShapeDtypeStruct((bx, heads, HEAD_DIM, HEAD_DIM), F32)],
        scratch_shapes=[pltpu.VMEM(s, t) for s, t in scratch],
        compiler_params=_params(("parallel", "arbitrary"), blocks, scratch),
        name="hgrn2")(proj, proj, proj, proj, lb, hnorm, s0)


def _layout(d_model, fw, gw, hw):
    order = [("gate_a", d_model), ("gate_b", d_model), ("gate_c", d_model), ("fq", fw), ("fk", fw), ("fv", fw),
             ("gqkv", 3 * gw), ("gz", gw), ("hq", hw), ("hf", hw), ("hi", hw), ("hg", hw),
             ("ff", LANES), ("gab", LANES)]
    lay, start = {}, 0
    for name, width in order:
        lay[name] = (start, width)
        start += width
    lay["total"] = (0, start)
    return lay


def _pack_w_in(w_in, lay, d_model, fh, gh, fw, gw, hw):
    splits = [("fq", fw), ("fk", fw), ("fv", fw), ("ff", fh), ("gq", gw), ("gk", gw), ("gv", gw), ("ga", gh),
              ("gb", gh), ("gz", gw), ("hq", hw), ("hf", hw), ("hi", hw), ("hg", hw),
              ("gate_a", d_model), ("gate_b", d_model), ("gate_c", d_model)]
    src, start = {}, 0
    for name, width in splits:
        src[name] = w_in[..., start:start + width].astype(BF16)
        start += width
    assert start == w_in.shape[-1]
    zeros = lambda width: jnp.zeros(w_in.shape[:-1] + (width,), BF16)
    pieces = [src["gate_a"], src["gate_b"], src["gate_c"], src["fq"], src["fk"], src["fv"],
              src["gq"], src["gk"], src["gv"], src["gz"], src["hq"], src["hf"], src["hi"], src["hg"],
              src["ff"], zeros(LANES - fh), src["ga"], src["gb"], zeros(LANES - 2 * gh)]
    packed = jnp.concatenate(pieces, axis=-1)
    assert packed.shape[-1] == lay["total"][1]
    return packed


def _pad_lanes(x, fill=0.0):
    return jnp.pad(x.astype(F32), ((0, 0), (0, LANES - x.shape[-1])), constant_values=fill)[:, None, :]


def _tile(n, candidates):
    for c in candidates:
        if n % c == 0:
            return c
    raise ValueError(f"no tile for {n} among {candidates}")


def kernel(x_prompt, x_sample, cache_fox_k, cache_fox_v, cache_fox_logf, state_gdn, state_gdn_conv, state_hgrn,
           page_table, norm_mix_pre, norm_mix_post, norm_mlp_pre, norm_mlp_post, w_in, fox_f_bias, gdn_conv_w,
           gdn_a_log, gdn_dt_bias, gdn_norm, hgrn_lower_bounds, hgrn_norm, w_fox_o, w_gdn_o, w_hgrn_o, w_out,
           w_up, w_down):
    bp, tp, d = x_prompt.shape
    bs, ts, _ = x_sample.shape
    depth = w_in.shape[0]
    fh, gh = fox_f_bias.shape[1], gdn_a_log.shape[1]
    fw, gw, hw = fh * HEAD_DIM, gh * HEAD_DIM, hgrn_lower_bounds.shape[1]
    hh = hw // HEAD_DIM
    n_pool, page = cache_fox_k.shape[1], cache_fox_k.shape[2]
    n_pages = page_table.shape[1]
    tsp = CHUNK
    assert tp % CHUNK == 0 and tp % LANES == 0 and ts <= tsp and ts >= CONV_W - 1
    assert page * fh == SUBLANES * LANES and 2 * gh <= LANES and fh <= LANES
    lay = _layout(d, fw, gw, hw)
    wp = lay["total"][1]

    w_in_p = _pack_w_in(w_in, lay, d, fh, gh, fw, gw, hw)
    w_fo, w_go, w_ho, w_o, w_u, w_d = (w.astype(BF16) for w in (w_fox_o, w_gdn_o, w_hgrn_o, w_out, w_up, w_down))
    g_mix_pre, g_mix_post, g_mlp_pre, g_mlp_post = (
        g.astype(F32)[:, None, :] for g in (norm_mix_pre, norm_mix_post, norm_mlp_pre, norm_mlp_post))
    f_bias = _pad_lanes(fox_f_bias)
    a_log, dt_bias = _pad_lanes(gdn_a_log), _pad_lanes(gdn_dt_bias)
    g_norm, h_norm = gdn_norm.astype(F32)[:, None, :], hgrn_norm.astype(F32)[:, None, :]
    conv_w = gdn_conv_w.astype(F32)
    lb_all = _lower_bounds(hgrn_lower_bounds)[:, None, :]

    ck = cache_fox_k.reshape(depth * n_pool, page * fh, HEAD_DIM)
    cv = cache_fox_v.reshape(depth * n_pool, page * fh, HEAD_DIM)
    clf = cache_fox_logf.astype(F32).reshape(depth * n_pool, SUBLANES, LANES)
    bias_page = jnp.tile(fox_f_bias.astype(F32), (1, LANES // fh))[:, None, :] * jnp.ones((1, SUBLANES, 1), F32)
    bias_col = jnp.tile(fox_f_bias.astype(F32), (1, ts))[:, :, None]

    zeros_state_p = jnp.zeros((bp, gh, HEAD_DIM, HEAD_DIM), F32)
    zeros_hstate_p = jnp.zeros((bp, hh, HEAD_DIM, HEAD_DIM), F32)
    zeros_conv_p = jnp.zeros((bp, CONV_W - 1, 3 * gw), F32)

    xp = x_prompt.reshape(bp * tp, d)
    xs = jnp.pad(x_sample, ((0, 0), (0, tsp - ts), (0, 0))).reshape(bs * tsp, d)

    tm_p = _tile(bp * tp, (1024, 512, 256, 128, 64))
    tm_s = _tile(bs * tsp, (512, 256, 128, 64))
    tn_in = _tile(wp, (768, 512, 384, 256, 128))
    tq = _tile(tp, FOX_Q_TILES)

    def dense_tail(x, oa, ob, oc, proj, l, tm):
        merged = _merge(oa, ob, oc, proj, lay, w_fo, w_go, w_ho, l, tm=min(tm, 512), tn=_tile(d, (1024, 512, 256)))
        x = _matmul_norm_res(merged, w_o, x, g_mix_post, l, tm=min(tm, 512), tk=_tile(d, (1024, 512, 256)),
                             name="out_proj_norm_res")
        u = _norm_matmul(x, g_mlp_pre, w_u, l, tm=tm, tn=_tile(w_u.shape[-1], (1024, 512, 256)), act="relu2",
                         out_dtype=BF16, name="mlp_up")
        return _matmul_norm_res(u, w_d, x, g_mlp_post, l, tm=min(tm, 512), tk=_tile(w_d.shape[1], (1024, 512, 256)),
                                name="mlp_down_norm_res")

    outs = {k: [] for k in ("fkp", "fvp", "flp", "fks", "fvs", "fls", "gsp", "gss", "gcp", "gcs", "hsp", "hss")}
    for l in range(depth):
        proj = _norm_matmul(xp, g_mix_pre, w_in_p, l, tm=tm_p, tn=tn_in, name="in_proj")
        lf, c_tok, c_row = _fox_gate(proj, lay, f_bias, l, bp, tp)
        oa = _fox_attn(proj, lay, c_tok, c_row, bp, tp, tq=tq)
        ob, gs, gc = _gdn(proj, lay, conv_w, a_log, dt_bias, g_norm, zeros_state_p, zeros_conv_p, l, bp, tp, tp)
        oc, hs = _hgrn(proj, lay, lb_all, h_norm, zeros_hstate_p, l, bp, tp, tp)
        xp = dense_tail(xp, oa, ob, oc, proj, l, tm_p)
        outs["fkp"].append(proj[:, lay["fk"][0]:lay["fk"][0] + fw].reshape(bp, tp, fh, HEAD_DIM))
        outs["fvp"].append(proj[:, lay["fv"][0]:lay["fv"][0] + fw].reshape(bp, tp, fh, HEAD_DIM))
        outs["flp"].append(lf[:, :fh].reshape(bp, tp, fh))
        outs["gsp"].append(gs)
        outs["gcp"].append(gc)
        outs["hsp"].append(hs)

        proj = _norm_matmul(xs, g_mix_pre, w_in_p, l, tm=tm_s, tn=tn_in, name="in_proj")
        p3 = proj.reshape(bs, tsp, wp)[:, :ts]
        q_new, k_new, v_new = (p3[:, :, lay[k][0]:lay[k][0] + fw] for k in ("fq", "fk", "fv"))
        ff_new = p3[:, :, lay["ff"][0]:lay["ff"][0] + fh]
        rows = ts * fh
        pad_rows = lambda z: jnp.pad(z.reshape(bs, rows, HEAD_DIM), ((0, 0), (0, LANES - rows), (0, 0)))
        ff_page = jnp.pad(ff_new.reshape(bs, 1, rows), ((0, 0), (0, SUBLANES - 1), (0, LANES - rows)))
        oa_s, lf_page = _fox_decode(
            q_new.reshape(bs, rows, HEAD_DIM), pad_rows(k_new), pad_rows(v_new), ff_page, bias_page[l],
            ff_new.reshape(bs, rows, 1), bias_col[l], ck, cv, clf,
            (page_table.astype(jnp.int32) + l * n_pool).reshape(-1), n_pages, heads=fh, n_new=ts)
        oa = jnp.pad(oa_s.reshape(bs, ts, fw), ((0, 0), (0, tsp - ts), (0, 0))).reshape(bs * tsp, fw)
        ob, gs, gc = _gdn(proj, lay, conv_w, a_log, dt_bias, g_norm, state_gdn[l].astype(F32),
                          state_gdn_conv[l].astype(F32), l, bs, tsp, ts)
        oc, hs = _hgrn(proj, lay, lb_all, h_norm, state_hgrn[l].astype(F32), l, bs, tsp, ts)
        xs = dense_tail(xs, oa, ob, oc, proj, l, tm_s)
        outs["fks"].append(k_new.reshape(bs, ts, fh, HEAD_DIM))
        outs["fvs"].append(v_new.reshape(bs, ts, fh, HEAD_DIM))
        outs["fls"].append(lf_page[:, 0, :rows].reshape(bs, ts, fh))
        outs["gss"].append(gs)
        outs["gcs"].append(gc)
        outs["hss"].append(hs)

    st = lambda k: jnp.stack(outs[k])
    return (xp.reshape(bp, tp, d), xs.reshape(bs, tsp, d)[:, :ts],
            st("fkp"), st("fvp"), st("flp"), st("fks"), st("fvs"), st("fls"),
            st("gsp"), st("gss"), st("gcp"), st("gcs"), st("hsp"), st("hss"))
```

```python
import functools

import jax
import jax.numpy as jnp
from jax import lax
from jax.experimental import pallas as pl
from jax.experimental.pallas import tpu as pltpu

F32, BF16 = jnp.float32, jnp.bfloat16
HEAD_DIM = 128
CONV_W = 4
CHUNK = 64
EPS = 1e-6
NEG_BIG = -1e30
LB_FLOOR = 1e-30
LANES = 128
SUBLANES = 8
V7X_VMEM_BYTES = 64 * 2**20
VMEM_RESERVE_BYTES = 6 * 2**20
KERNEL_TEMP_BYTES = 16 * 2**20
DECODE_PAGES_PER_STEP = 8
FOX_Q_TILES = (1024, 512, 256, 128)
PACK_ROW_TILE = 512
PACK_COL_TILE = 1024
MERGE_ROW_TILE = 256
RESIDUAL_ROW_TILE = 512
RESIDUAL_K_TILES = (2048, 1024, 512, 256)


def _nbytes(shape, dtype):
    n = 1
    for s in shape:
        n *= s
    return n * jnp.dtype(dtype).itemsize


def _params(semantics, blocks, scratch=()):
    need = 2 * sum(_nbytes(s, d) for s, d in blocks) + sum(_nbytes(s, d) for s, d in scratch) + KERNEL_TEMP_BYTES
    return pltpu.CompilerParams(dimension_semantics=semantics,
                                vmem_limit_bytes=min(need, V7X_VMEM_BYTES - VMEM_RESERVE_BYTES))


def _dot(a, b):
    return jnp.dot(a, b, preferred_element_type=F32)


def _dot_nt(a, b):
    return lax.dot_general(a, b, (((1,), (1,)), ((), ())), preferred_element_type=F32)


def _split3(x):
    x1 = x.astype(BF16)
    r = x - x1.astype(F32)
    x2 = r.astype(BF16)
    x3 = (r - x2.astype(F32)).astype(BF16)
    return x1, x2, x3


def _dot01(m01, x):
    x1, x2, x3 = _split3(x)
    return _dot(m01, x1) + (_dot(m01, x2) + _dot(m01, x3))


def _log_sigmoid(x):
    return -(jnp.maximum(-x, 0.0) + jnp.log1p(jnp.exp(-jnp.abs(x))))


def _softplus(x):
    return jnp.maximum(x, 0.0) + jnp.log1p(jnp.exp(-jnp.abs(x)))


def _silu(x):
    return x * jax.nn.sigmoid(x)


def _tril01(n, strict=False):
    r = lax.broadcasted_iota(jnp.int32, (n, n), 0)
    c = lax.broadcasted_iota(jnp.int32, (n, n), 1)
    return (r > c) if strict else (r >= c)


def _lower_bounds_kernel(x_ref, o_ref):
    x = x_ref[...]
    e = jnp.exp(x - jnp.max(x, axis=0, keepdims=True))
    p = e / jnp.sum(e, axis=0, keepdims=True)
    run = jnp.zeros_like(p[0:1])
    for l in range(x.shape[0]):
        run = run + p[l:l + 1]
        o_ref[l:l + 1, :] = run - p[0:1]


def _lower_bounds(x):
    return pl.pallas_call(_lower_bounds_kernel, out_shape=jax.ShapeDtypeStruct(x.shape, F32),
                          name="hgrn_lower_bounds")(x.astype(F32))


def _norm_matmul_kernel(x_ref, g_ref, w_ref, o_ref, xn_ref, *, act):
    @pl.when(pl.program_id(1) == 0)
    def _():
        x = x_ref[...]
        var = jnp.mean(x * x, axis=-1, keepdims=True)
        xn_ref[...] = ((x * lax.rsqrt(var + EPS)) * g_ref[...]).astype(BF16)

    y = _dot(xn_ref[...], w_ref[...])
    if act == "relu2":
        y = jnp.maximum(y, 0.0)
        y = y * y
    o_ref[...] = y.astype(o_ref.dtype)


def _norm_matmul(x, g, w, layer, *, tm, tn, act=None, out_dtype=F32, name):
    n, d = x.shape
    wn = w.shape[-1]
    blocks = [((tm, d), F32), ((1, d), F32), ((d, tn), BF16), ((tm, tn), out_dtype)]
    scratch = [((tm, d), BF16)]
    return pl.pallas_call(
        functools.partial(_norm_matmul_kernel, act=act),
        grid=(n // tm, wn // tn),
        in_specs=[pl.BlockSpec((tm, d), lambda i, j: (i, 0)),
                  pl.BlockSpec((None, 1, d), lambda i, j: (layer, 0, 0)),
                  pl.BlockSpec((None, d, tn), lambda i, j: (layer, 0, j))],
        out_specs=pl.BlockSpec((tm, tn), lambda i, j: (i, j)),
        out_shape=jax.ShapeDtypeStruct((n, wn), out_dtype),
        scratch_shapes=[pltpu.VMEM(s, t) for s, t in scratch],
        compiler_params=_params(("parallel", "arbitrary"), blocks, scratch),
        name=name)(x, g, w)


def _matmul_norm_res_kernel(a_ref, w_ref, x_ref, g_ref, o_ref, acc_ref):
    k = pl.program_id(1)

    @pl.when(k == 0)
    def _():
        acc_ref[...] = jnp.zeros_like(acc_ref)

    acc_ref[...] += _dot(a_ref[...], w_ref[...])

    @pl.when(k == pl.num_programs(1) - 1)
    def _():
        y = acc_ref[...]
        var = jnp.mean(y * y, axis=-1, keepdims=True)
        o_ref[...] = x_ref[...] + (y * lax.rsqrt(var + EPS)) * g_ref[...]


def _matmul_norm_res(a, w, x, g, layer, *, tm, tk, name):
    n, kdim = a.shape
    d = x.shape[1]
    blocks = [((tm, tk), BF16), ((tk, d), BF16), ((tm, d), F32), ((1, d), F32), ((tm, d), F32)]
    scratch = [((tm, d), F32)]
    return pl.pallas_call(
        _matmul_norm_res_kernel,
        grid=(n // tm, kdim // tk),
        in_specs=[pl.BlockSpec((tm, tk), lambda i, k: (i, k)),
                  pl.BlockSpec((None, tk, d), lambda i, k: (layer, k, 0)),
                  pl.BlockSpec((tm, d), lambda i, k: (i, 0)),
                  pl.BlockSpec((None, 1, d), lambda i, k: (layer, 0, 0))],
        out_specs=pl.BlockSpec((tm, d), lambda i, k: (i, 0)),
        out_shape=jax.ShapeDtypeStruct((n, d), F32),
        scratch_shapes=[pltpu.VMEM(s, t) for s, t in scratch],
        compiler_params=_params(("parallel", "arbitrary"), blocks, scratch),
        name=name)(a, w, x, g)


def _merge_kernel(oa_ref, ob_ref, oc_ref, ga_ref, gb_ref, gc_ref, wa_ref, wb_ref, wc_ref, o_ref):
    m = jax.nn.sigmoid(ga_ref[...]) * _dot(oa_ref[...], wa_ref[...])
    m = m + jax.nn.sigmoid(gb_ref[...]) * _dot(ob_ref[...], wb_ref[...])
    m = m + jax.nn.sigmoid(gc_ref[...]) * _dot(oc_ref[...], wc_ref[...])
    o_ref[...] = m.astype(o_ref.dtype)


def _merge(oa, ob, oc, proj, lay, wa, wb, wc, layer, *, tm, tn):
    n = oa.shape[0]
    d = wa.shape[-1]
    col = {k: lay[k][0] // tn for k in ("gate_a", "gate_b", "gate_c")}
    o_specs = [pl.BlockSpec((tm, o.shape[1]), lambda i, j: (i, 0)) for o in (oa, ob, oc)]
    g_specs = [pl.BlockSpec((tm, tn), functools.partial(lambda i, j, c: (i, c + j), c=col[k]))
               for k in ("gate_a", "gate_b", "gate_c")]
    w_specs = [pl.BlockSpec((None, w.shape[1], tn), lambda i, j: (layer, 0, j)) for w in (wa, wb, wc)]
    blocks = ([((tm, o.shape[1]), BF16) for o in (oa, ob, oc)] + [((tm, tn), F32)] * 3
              + [((w.shape[1], tn), BF16) for w in (wa, wb, wc)] + [((tm, tn), BF16)])
    return pl.pallas_call(
        _merge_kernel,
        grid=(n // tm, d // tn),
        in_specs=o_specs + g_specs + w_specs,
        out_specs=pl.BlockSpec((tm, tn), lambda i, j: (i, j)),
        out_shape=jax.ShapeDtypeStruct((n, d), BF16),
        compiler_params=_params(("parallel", "parallel"), blocks),
        name="gated_merge")(oa, ob, oc, proj, proj, proj, wa, wb, wc)


def _fox_gate_kernel(ff_ref, b_ref, lf_ref, c_ref, ct_ref):
    t = ff_ref.shape[0]
    lf_ref[...] = _log_sigmoid(ff_ref[...] + b_ref[...])
    tri = _tril01(LANES).astype(BF16)
    carry = jnp.zeros((1, LANES), F32)
    for blk in range(t // LANES):
        rows = slice(blk * LANES, (blk + 1) * LANES)
        cb = _dot01(tri, lf_ref[rows, :]) + carry
        c_ref[rows, :] = cb
        ct_ref[:, rows] = cb.T
        carry = cb[LANES - 1:LANES, :]


def _fox_gate(proj, lay, bias, layer, bx, tx):
    n = proj.shape[0]
    col = lay["ff"][0] // LANES
    blocks = [((tx, LANES), F32)] * 4 + [((LANES, tx), F32)]
    return pl.pallas_call(
        _fox_gate_kernel,
        grid=(bx,),
        in_specs=[pl.BlockSpec((tx, LANES), lambda b: (b, col)),
                  pl.BlockSpec((None, 1, LANES), lambda b: (layer, 0, 0))],
        out_specs=[pl.BlockSpec((tx, LANES), lambda b: (b, 0)),
                   pl.BlockSpec((tx, LANES), lambda b: (b, 0)),
                   pl.BlockSpec((None, LANES, tx), lambda b: (b, 0, 0))],
        out_shape=[jax.ShapeDtypeStruct((n, LANES), F32), jax.ShapeDtypeStruct((n, LANES), F32),
                   jax.ShapeDtypeStruct((bx, LANES, tx), F32)],
        compiler_params=_params(("parallel",), blocks),
        name="fox_gate")(proj, bias)


def _fox_attn_kernel(q_ref, k_ref, v_ref, ct_ref, cr_ref, o_ref, m_ref, l_ref, acc_ref, *, heads, scale):
    i, j = pl.program_id(1), pl.program_id(2)
    tq, tk = q_ref.shape[0], k_ref.shape[0]

    @pl.when(j == 0)
    def _():
        m_ref[...] = jnp.full_like(m_ref, NEG_BIG)
        l_ref[...] = jnp.zeros_like(l_ref)
        acc_ref[...] = jnp.zeros_like(acc_ref)

    @pl.when(j <= i)
    def _():
        q_pos = i * tq + lax.broadcasted_iota(jnp.int32, (tq, tk), 0)
        k_pos = j * tk + lax.broadcasted_iota(jnp.int32, (tq, tk), 1)
        causal = q_pos >= k_pos
        for h in range(heads):
            hs = slice(h * HEAD_DIM, (h + 1) * HEAD_DIM)
            s = _dot_nt(q_ref[:, hs].astype(BF16), k_ref[:, hs].astype(BF16)) * scale
            s = s + ct_ref[:, h:h + 1] - cr_ref[h:h + 1, :]
            s = jnp.where(causal, s, NEG_BIG)
            m_prev = m_ref[:, h:h + 1]
            m_new = jnp.maximum(m_prev, jnp.max(s, axis=-1, keepdims=True))
            alpha = jnp.exp(m_prev - m_new)
            p = jnp.exp(s - m_new)
            l_ref[:, h:h + 1] = alpha * l_ref[:, h:h + 1] + jnp.sum(p, axis=-1, keepdims=True)
            acc_ref[:, hs] = alpha * acc_ref[:, hs] + _dot(p.astype(BF16), v_ref[:, hs].astype(BF16))
            m_ref[:, h:h + 1] = m_new

    @pl.when(j == i)
    def _():
        for h in range(heads):
            hs = slice(h * HEAD_DIM, (h + 1) * HEAD_DIM)
            o_ref[:, hs] = (acc_ref[:, hs] / l_ref[:, h:h + 1]).astype(o_ref.dtype)


def _fox_attn(proj, lay, c_tok, c_row, bx, tx, *, tq):
    n = proj.shape[0]
    fw = lay["fq"][1]
    heads = fw // HEAD_DIM
    nq = tx // tq
    qc, kc, vc = (lay[k][0] // fw for k in ("fq", "fk", "fv"))
    blocks = [((tq, fw), F32)] * 3 + [((tq, LANES), F32), ((LANES, tq), F32), ((tq, fw), BF16)]
    scratch = [((tq, LANES), F32), ((tq, LANES), F32), ((tq, fw), F32)]
    return pl.pallas_call(
        functools.partial(_fox_attn_kernel, heads=heads, scale=HEAD_DIM ** -0.5),
        grid=(bx, nq, nq),
        in_specs=[pl.BlockSpec((tq, fw), lambda b, i, j: (b * nq + i, qc)),
                  pl.BlockSpec((tq, fw), lambda b, i, j: (b * nq + jnp.minimum(j, i), kc)),
                  pl.BlockSpec((tq, fw), lambda b, i, j: (b * nq + jnp.minimum(j, i), vc)),
                  pl.BlockSpec((tq, LANES), lambda b, i, j: (b * nq + i, 0)),
                  pl.BlockSpec((None, LANES, tq), lambda b, i, j: (b, 0, jnp.minimum(j, i)))],
        out_specs=pl.BlockSpec((tq, fw), lambda b, i, j: (b * nq + i, 0)),
        out_shape=jax.ShapeDtypeStruct((n, fw), BF16),
        scratch_shapes=[pltpu.VMEM(s, t) for s, t in scratch],
        compiler_params=_params(("parallel", "parallel", "arbitrary"), blocks, scratch),
        name="fox_attn_prompt")(proj, proj, proj, c_tok, c_row)


def _page_suffix(x, lane, sub, heads):
    y = x
    sh = heads
    while sh < LANES:
        y = y + jnp.where(lane < LANES - sh, pltpu.roll(y, LANES - sh, axis=1), 0.0)
        sh *= 2
    t = jnp.where(lane < heads, y, 0.0)
    sh = heads
    while sh < LANES:
        t = t + pltpu.roll(t, sh, axis=1)
        sh *= 2
    z = t
    sh = 1
    while sh < SUBLANES:
        z = z + jnp.where(sub < SUBLANES - sh, pltpu.roll(z, SUBLANES - sh, axis=0), 0.0)
        sh *= 2
    return (y - x) + (z - t), z[0:1, :]


def _fox_decode_kernel(pt_ref, q_ref, kn_ref, vn_ref, ffp_ref, bp_ref, ffc_ref, bc_ref, *rest,
                       pages, heads, n_new, scale):
    del pt_ref
    k_refs, v_refs, lf_refs = rest[:pages], rest[pages:2 * pages], rest[2 * pages:3 * pages]
    o_ref, lfo_ref, m_ref, l_ref, acc_ref, carry_ref, rn_ref = rest[3 * pages:]
    step = pl.program_id(1)
    rows = q_ref.shape[0]
    lane = lax.broadcasted_iota(jnp.int32, (SUBLANES, LANES), 1)
    sub = lax.broadcasted_iota(jnp.int32, (SUBLANES, LANES), 0)
    r_i = lax.broadcasted_iota(jnp.int32, (rows, LANES), 0)
    c_i = lax.broadcasted_iota(jnp.int32, (rows, LANES), 1)
    head_ok = (r_i % heads) == (c_i % heads)
    qb = q_ref[...].astype(BF16)

    def update(scores, values):
        m_prev = m_ref[...]
        m_new = m_prev
        for s in scores:
            m_new = jnp.maximum(m_new, jnp.max(s, axis=-1, keepdims=True))
        alpha = jnp.exp(m_prev - m_new)
        l_new = alpha * l_ref[...]
        acc = alpha * acc_ref[...]
        for s, v in zip(scores, values):
            p = jnp.exp(s - m_new)
            l_new = l_new + jnp.sum(p, axis=-1, keepdims=True)
            acc = acc + _dot(p.astype(BF16), v)
        m_ref[...] = m_new
        l_ref[...] = l_new
        acc_ref[...] = acc

    @pl.when(step == 0)
    def _():
        m_ref[...] = jnp.full_like(m_ref, NEG_BIG)
        l_ref[...] = jnp.zeros_like(l_ref)
        acc_ref[...] = jnp.zeros_like(acc_ref)
        new_ok = (sub == 0) & (lane < n_new * heads)
        lf = jnp.where(new_ok, _log_sigmoid(ffp_ref[...] + bp_ref[...]), 0.0)
        lfo_ref[...] = lf
        gate, total = _page_suffix(lf, lane, sub, heads)
        carry_ref[...] = total
        lfc = _log_sigmoid(ffc_ref[...] + bc_ref[...])
        run = jnp.zeros((heads, 1), F32)
        for t in range(n_new - 1, -1, -1):
            rn_ref[t * heads:(t + 1) * heads, :] = run
            run = run + lfc[t * heads:(t + 1) * heads, :]
        s = _dot_nt(qb, kn_ref[...].astype(BF16)) * scale + gate[0:1, :] - rn_ref[...]
        ok = head_ok & (c_i // heads <= r_i // heads) & (c_i < n_new * heads)
        update([jnp.where(ok, s, NEG_BIG)], [vn_ref[...].astype(BF16)])

    carry = carry_ref[...]
    rn = rn_ref[...]
    scores = []
    for r in range(pages):
        gate, total = _page_suffix(lf_refs[r][...], lane, sub, heads)
        gate = gate + carry
        carry = carry + total
        s_page = _dot_nt(qb, k_refs[r][...].astype(BF16)) * scale
        scores.append([jnp.where(head_ok, s_page[:, c * LANES:(c + 1) * LANES] + gate[c:c + 1, :] - rn, NEG_BIG)
                       for c in range(SUBLANES)])
    carry_ref[...] = carry
    m_prev = m_ref[...]
    m_new = m_prev
    for page_scores in scores:
        for s in page_scores:
            m_new = jnp.maximum(m_new, jnp.max(s, axis=-1, keepdims=True))
    alpha = jnp.exp(m_prev - m_new)
    l_new = alpha * l_ref[...]
    acc = alpha * acc_ref[...]
    for r in range(pages):
        probs = [jnp.exp(s - m_new) for s in scores[r]]
        for p in probs:
            l_new = l_new + jnp.sum(p, axis=-1, keepdims=True)
        acc = acc + _dot(jnp.concatenate(probs, axis=-1).astype(BF16), v_refs[r][...].astype(BF16))
    m_ref[...] = m_new
    l_ref[...] = l_new
    acc_ref[...] = acc

    @pl.when(step == pl.num_programs(1) - 1)
    def _():
        o_ref[...] = (acc_ref[...] / l_ref[...]).astype(o_ref.dtype)


def _fox_decode(q, k_new, v_new, ff_page, bias_page, ff_col, bias_col, cache_k, cache_v, cache_lf, pages_flat,
                n_pages, *, heads, n_new):
    bs, rows, _ = q.shape
    page_rows = cache_k.shape[1]
    assert page_rows == SUBLANES * LANES and heads * (LANES // heads) == LANES and n_pages % DECODE_PAGES_PER_STEP == 0
    pages = DECODE_PAGES_PER_STEP
    n_steps = n_pages // pages

    def page_map(r):
        return lambda b, s, pt: (pt[b * n_pages + (n_pages - 1 - (s * pages + r))], 0, 0)

    in_specs = [pl.BlockSpec((None, rows, HEAD_DIM), lambda b, s, pt: (b, 0, 0)),
                pl.BlockSpec((None, LANES, HEAD_DIM), lambda b, s, pt: (b, 0, 0)),
                pl.BlockSpec((None, LANES, HEAD_DIM), lambda b, s, pt: (b, 0, 0)),
                pl.BlockSpec((None, SUBLANES, LANES), lambda b, s, pt: (b, 0, 0)),
                pl.BlockSpec((SUBLANES, LANES), lambda b, s, pt: (0, 0)),
                pl.BlockSpec((None, rows, 1), lambda b, s, pt: (b, 0, 0)),
                pl.BlockSpec((rows, 1), lambda b, s, pt: (0, 0))]
    in_specs += [pl.BlockSpec((None, page_rows, HEAD_DIM), page_map(r)) for r in range(pages)] * 2
    in_specs += [pl.BlockSpec((None, SUBLANES, LANES), page_map(r)) for r in range(pages)]
    blocks = [((page_rows, HEAD_DIM), F32)] * (2 * pages) + [((LANES, HEAD_DIM), F32)] * 4
    scratch = [((rows, 1), F32), ((rows, 1), F32), ((rows, HEAD_DIM), F32), ((1, LANES), F32), ((rows, 1), F32)]
    grid_spec = pltpu.PrefetchScalarGridSpec(
        num_scalar_prefetch=1, grid=(bs, n_steps), in_specs=in_specs,
        out_specs=[pl.BlockSpec((None, rows, HEAD_DIM), lambda b, s, pt: (b, 0, 0)),
                   pl.BlockSpec((None, SUBLANES, LANES), lambda b, s, pt: (b, 0, 0))],
        scratch_shapes=[pltpu.VMEM(s, t) for s, t in scratch])
    return pl.pallas_call(
        functools.partial(_fox_decode_kernel, pages=pages, heads=heads, n_new=n_new, scale=HEAD_DIM ** -0.5),
        grid_spec=grid_spec,
        out_shape=[jax.ShapeDtypeStruct((bs, rows, HEAD_DIM), BF16),
                   jax.ShapeDtypeStruct((bs, SUBLANES, LANES), F32)],
        compiler_params=_params(("parallel", "arbitrary"), blocks, scratch),
        name="fox_decode")(pages_flat, q, k_new, v_new, ff_page, bias_page, ff_col, bias_col,
                           *([cache_k] * pages), *([cache_v] * pages), *([cache_lf] * pages))


def _dot_hi(a, b):
    a_hi = a.astype(BF16).astype(F32)
    a16 = jnp.concatenate([a_hi, a - a_hi], axis=1).astype(BF16)
    b_hi = b.astype(BF16)
    b_lo = (b - b_hi.astype(F32)).astype(BF16)
    return _dot(jnp.concatenate([a16, a16], axis=1), jnp.concatenate([b_hi, b_hi, b_lo, b_lo], axis=0))


def _unit_lower_solves(ms, rhss):
    c = ms[0].shape[0]
    assert 2 * c == LANES
    xs = [r - _dot_hi(m, r) for m, r in zip(ms, rhss)]
    ps, power = list(ms), 1
    while 2 * power < c:
        ps = [_dot_hi(p, p) for p in ps]
        power *= 2
        xs = [x + _dot_hi(p, x) for p, x in zip(ps, xs)]
    return xs


def _gdn_prep_kernel(qkv_ref, ab_ref, cw_ref, al_ref, dt_ref, cb0_ref,
                     u_ref, w_ref, qe_ref, kdt_ref, attn_ref, egl_ref, cb_out_ref, xe_ref, *, heads, t_valid):
    c = pl.program_id(1)
    ch = qkv_ref.shape[0]
    width = heads * HEAD_DIM
    tail = CONV_W - 1
    base = SUBLANES
    last_chunk = (t_valid - 1) // ch
    last_len = t_valid - last_chunk * ch

    @pl.when(c == 0)
    def _():
        xe_ref[base - tail:base, :] = cb0_ref[...]

    xe_ref[base:base + ch, :] = qkv_ref[...]
    y = xe_ref[base - tail:base - tail + ch, :] * cw_ref[0:1, :]
    for i in range(1, CONV_W):
        y = y + xe_ref[base - tail + i:base - tail + i + ch, :] * cw_ref[i:i + 1, :]
    y = _silu(y)

    @pl.when(c == last_chunk)
    def _():
        cb_out_ref[...] = xe_ref[base + last_len - tail:base + last_len, :]

    xe_ref[base - tail:base, :] = xe_ref[base + ch - tail:base + ch, :]

    valid = (c * ch + lax.broadcasted_iota(jnp.int32, (ch, 1), 0)) < t_valid
    ab = ab_ref[...]
    g_all = jnp.where(valid, -jnp.exp(al_ref[...]) * _softplus(ab + dt_ref[...]), 0.0)
    beta_all = jnp.where(valid, jax.nn.sigmoid(ab), 0.0)
    cum = _dot01(_tril01(ch).astype(BF16), g_all)
    cum_t = cum.T
    lower = _tril01(ch)
    strict = _tril01(ch, strict=True)
    egl_ref[...] = jnp.exp(cum[ch - 1:ch, :])

    hr = range(heads)
    sl = lambda part, h: slice(part * width + h * HEAD_DIM, part * width + (h + 1) * HEAD_DIM)
    gc = [cum[:, h:h + 1] for h in hr]
    beta = [beta_all[:, heads + h:heads + h + 1] for h in hr]
    decay = [jnp.exp(jnp.where(lower, gc[h] - cum_t[h:h + 1, :], NEG_BIG)) for h in hr]
    q = [y[:, sl(0, h)] for h in hr]
    k = [y[:, sl(1, h)] for h in hr]
    q = [x * lax.rsqrt(jnp.sum(x * x, axis=-1, keepdims=True) + EPS) * (HEAD_DIM ** -0.5) for x in q]
    k = [x * lax.rsqrt(jnp.sum(x * x, axis=-1, keepdims=True) + EPS) for x in k]
    kb = [k[h] * beta[h] for h in hr]
    k16 = [x.astype(BF16) for x in k]
    eg = [jnp.exp(gc[h]) for h in hr]
    sols = _unit_lower_solves(
        [jnp.where(strict, _dot_nt(kb[h].astype(BF16), k16[h]) * decay[h], 0.0) for h in hr],
        [jnp.concatenate([y[:, sl(2, h)] * beta[h], kb[h] * eg[h]], axis=-1) for h in hr])
    for h in hr:
        sol = sols[h]
        u_ref[:, sl(0, h)] = sol[:, :HEAD_DIM]
        w_ref[:, sl(0, h)] = sol[:, HEAD_DIM:].astype(BF16)
        qe_ref[:, sl(0, h)] = (q[h] * eg[h]).astype(BF16)
        attn_ref[h] = (_dot_nt(q[h].astype(BF16), k16[h]) * decay[h]).astype(BF16)
        kdt_ref[h] = (k[h] * jnp.exp(cum[ch - 1:ch, h:h + 1] - gc[h])).T.astype(BF16)


def _gdn_scan_kernel(u_ref, w_ref, qe_ref, kdt_ref, attn_ref, egl_ref, gz_ref, gn_ref, s0_ref,
                     o_ref, s_out_ref, st_ref, *, heads):
    c = pl.program_id(0)
    bx = u_ref.shape[0]

    @pl.when(c == 0)
    def _():
        st_ref[...] = s0_ref[...]

    chains = [(b, h) for b in range(bx) for h in range(heads)]
    hs = lambda h: slice(h * HEAD_DIM, (h + 1) * HEAD_DIM)
    s_old = [st_ref[b, h] for b, h in chains]
    s16 = [s.astype(BF16) for s in s_old]
    ws = [_dot(w_ref[b, :, hs(h)], s) for (b, h), s in zip(chains, s16)]
    qs = [_dot(qe_ref[b, :, hs(h)], s) for (b, h), s in zip(chains, s16)]
    vn16 = [(u_ref[b, :, hs(h)] - x).astype(BF16) for (b, h), x in zip(chains, ws)]
    for i, (b, h) in enumerate(chains):
        o = qs[i] + _dot(attn_ref[b, h], vn16[i])
        st_ref[b, h] = s_old[i] * egl_ref[b, :, h:h + 1] + _dot(kdt_ref[b, h], vn16[i])
        on = (o * lax.rsqrt(jnp.mean(o * o, axis=-1, keepdims=True) + EPS)) * gn_ref[...]
        o_ref[b, :, hs(h)] = (on * _silu(gz_ref[b, :, hs(h)])).astype(o_ref.dtype)

    @pl.when(c == pl.num_programs(0) - 1)
    def _():
        s_out_ref[...] = st_ref[...]


def _gdn(proj, lay, conv_w, a_log, dt_bias, gnorm, s0, cb0, layer, bx, tx, t_valid):
    n, wp = proj.shape
    gw = lay["gz"][1]
    heads = gw // HEAD_DIM
    ch = CHUNK
    nc = tx // ch
    qkv_c, ab_c, gz_c = lay["gqkv"][0] // (3 * gw), lay["gab"][0] // LANES, lay["gz"][0] // gw
    blocks = [((ch, 3 * gw), F32), ((ch, LANES), F32), ((CONV_W, 3 * gw), F32), ((CONV_W - 1, 3 * gw), F32),
              ((ch, gw), F32), ((ch, gw), BF16), ((ch, gw), BF16), ((heads, HEAD_DIM, LANES), BF16),
              ((heads, ch, LANES), BF16), ((CONV_W - 1, 3 * gw), F32)]
    scratch = [((SUBLANES + ch, 3 * gw), F32)]
    u, w16, qe16, kdt, attn, egl, cb_out = pl.pallas_call(
        functools.partial(_gdn_prep_kernel, heads=heads, t_valid=t_valid),
        grid=(bx, nc),
        in_specs=[pl.BlockSpec((ch, 3 * gw), lambda b, c: (b * nc + c, qkv_c)),
                  pl.BlockSpec((ch, LANES), lambda b, c: (b * nc + c, ab_c)),
                  pl.BlockSpec((None, CONV_W, 3 * gw), lambda b, c: (layer, 0, 0)),
                  pl.BlockSpec((None, 1, LANES), lambda b, c: (layer, 0, 0)),
                  pl.BlockSpec((None, 1, LANES), lambda b, c: (layer, 0, 0)),
                  pl.BlockSpec((None, CONV_W - 1, 3 * gw), lambda b, c: (b, 0, 0))],
        out_specs=[pl.BlockSpec((ch, gw), lambda b, c: (b * nc + c, 0)),
                   pl.BlockSpec((ch, gw), lambda b, c: (b * nc + c, 0)),
                   pl.BlockSpec((ch, gw), lambda b, c: (b * nc + c, 0)),
                   pl.BlockSpec((None, None, heads, HEAD_DIM, ch), lambda b, c: (b, c, 0, 0, 0)),
                   pl.BlockSpec((None, None, heads, ch, ch), lambda b, c: (b, c, 0, 0, 0)),
                   pl.BlockSpec((None, None, 1, LANES), lambda b, c: (b, c, 0, 0)),
                   pl.BlockSpec((None, CONV_W - 1, 3 * gw), lambda b, c: (b, 0, 0))],
        out_shape=[jax.ShapeDtypeStruct((n, gw), F32), jax.ShapeDtypeStruct((n, gw), BF16),
                   jax.ShapeDtypeStruct((n, gw), BF16),
                   jax.ShapeDtypeStruct((bx, nc, heads, HEAD_DIM, ch), BF16),
                   jax.ShapeDtypeStruct((bx, nc, heads, ch, ch), BF16),
                   jax.ShapeDtypeStruct((bx, nc, 1, LANES), F32),
                   jax.ShapeDtypeStruct((bx, CONV_W - 1, 3 * gw), F32)],
        scratch_shapes=[pltpu.VMEM(s, t) for s, t in scratch],
        compiler_params=_params(("parallel", "arbitrary"), blocks, scratch),
        name="gdn_prep")(proj, proj, conv_w, a_log, dt_bias, cb0)

    state = (bx, heads, HEAD_DIM, HEAD_DIM)
    blocks = [((bx, ch, gw), F32), ((bx, ch, gw), BF16), ((bx, ch, gw), BF16), ((bx, heads, HEAD_DIM, LANES), BF16),
              ((bx, heads, ch, LANES), BF16), ((bx, ch, gw), F32), (state, F32), ((bx, ch, gw), BF16), (state, F32)]
    scratch = [(state, F32)]
    rows3 = lambda a: a.reshape(bx, tx, a.shape[-1])
    o, s_out = pl.pallas_call(
        functools.partial(_gdn_scan_kernel, heads=heads),
        grid=(nc,),
        in_specs=[pl.BlockSpec((bx, ch, gw), lambda c: (0, c, 0)),
                  pl.BlockSpec((bx, ch, gw), lambda c: (0, c, 0)),
                  pl.BlockSpec((bx, ch, gw), lambda c: (0, c, 0)),
                  pl.BlockSpec((bx, None, heads, HEAD_DIM, ch), lambda c: (0, c, 0, 0, 0)),
                  pl.BlockSpec((bx, None, heads, ch, ch), lambda c: (0, c, 0, 0, 0)),
                  pl.BlockSpec((bx, None, 1, LANES), lambda c: (0, c, 0, 0)),
                  pl.BlockSpec((bx, ch, gw), lambda c: (0, c, gz_c)),
                  pl.BlockSpec((None, 1, HEAD_DIM), lambda c: (layer, 0, 0)),
                  pl.BlockSpec(state, lambda c: (0, 0, 0, 0))],
        out_specs=[pl.BlockSpec((bx, ch, gw), lambda c: (0, c, 0)),
                   pl.BlockSpec(state, lambda c: (0, 0, 0, 0))],
        out_shape=[jax.ShapeDtypeStruct((bx, tx, gw), BF16), jax.ShapeDtypeStruct(state, F32)],
        scratch_shapes=[pltpu.VMEM(s, t) for s, t in scratch],
        compiler_params=_params(("arbitrary",), blocks, scratch),
        name="gdn_scan")(rows3(u), rows3(w16), rows3(qe16), kdt, attn, egl, rows3(proj), gnorm, s0)
    return o.reshape(n, gw), s_out, cb_out


def _hgrn_kernel(hq_ref, hf_ref, hi_ref, hg_ref, lb_ref, hn_ref, s0_ref, o_ref, s_out_ref, st_ref,
                 *, heads, t_valid):
    c = pl.program_id(1)
    ch = hq_ref.shape[0]
    nblk = ch // SUBLANES

    @pl.when(c == 0)
    def _():
        for h in range(heads):
            st_ref[h] = s0_ref[h].T

    tok = lax.broadcasted_iota(jnp.int32, (ch, 1), 0)
    valid = (c * ch + tok) < t_valid
    lb = lb_ref[...]
    hf = hf_ref[...]
    a = jnp.log(jnp.maximum(lb, LB_FLOOR))
    b = jnp.log1p(-lb) + _log_sigmoid(hf)
    log_f = jnp.maximum(a, b) + jnp.log1p(jnp.exp(-jnp.abs(a - b)))
    log_f = jnp.where(valid, log_f, 0.0)
    k_all = jnp.where(valid, (1.0 - lb) * jax.nn.sigmoid(-hf), 0.0)
    cum_all = _dot01(_tril01(ch).astype(BF16), log_f)
    q_all = _silu(hq_ref[...])

    ti = lax.broadcasted_iota(jnp.int32, (ch, ch), 0)
    tj = lax.broadcasted_iota(jnp.int32, (ch, ch), 1)
    sub3 = lax.broadcasted_iota(jnp.int32, (nblk, SUBLANES, HEAD_DIM), 1)

    for h in range(heads):
        hs = slice(h * HEAD_DIM, (h + 1) * HEAD_DIM)
        g, q, k, v = cum_all[:, hs], q_all[:, hs], k_all[:, hs], hi_ref[:, hs]
        attn_t = jnp.zeros((ch, ch), F32)
        half = ch // 2
        while half >= SUBLANES:
            blk = 2 * half
            g3 = g.reshape(ch // blk, blk, HEAD_DIM)
            e = jnp.exp(-jnp.abs(g3 - g3[:, half - 1:half, :])).reshape(ch, HEAD_DIM)
            upper = (tok % blk) >= half
            qa = jnp.where(upper, q * e, 0.0).astype(BF16)
            kb = jnp.where(upper, 0.0, k * e).astype(BF16)
            attn_t = attn_t + jnp.where((ti // blk) == (tj // blk), _dot_nt(kb, qa), 0.0)
            half //= 2
        g3, q3, k3 = (z.reshape(nblk, SUBLANES, HEAD_DIM) for z in (g, q, k))
        cols = []
        for i in range(SUBLANES):
            e = jnp.exp(jnp.where(sub3 <= i, g3[:, i:i + 1, :] - g3, NEG_BIG))
            w = jnp.sum(e * k3 * q3[:, i:i + 1, :], axis=-1, keepdims=True).reshape(ch, 1)
            cols.append(jnp.where((tj % SUBLANES) == i, w, 0.0))
        while len(cols) > 1:
            cols = [a + b for a, b in zip(cols[0::2], cols[1::2])]
        attn_t = attn_t + jnp.where((ti // SUBLANES) == (tj // SUBLANES), cols[0], 0.0)
        st = st_ref[h]
        o = _dot_nt((q * jnp.exp(g)).astype(BF16), st.astype(BF16)) + _dot(attn_t.T.astype(BF16), v.astype(BF16))
        gl = g[ch - 1:ch, :]
        kd = k * jnp.exp(gl - g)
        st_ref[h] = st * jnp.exp(gl) + _dot(v.T.astype(BF16), kd.astype(BF16))
        on = (o * lax.rsqrt(jnp.mean(o * o, axis=-1, keepdims=True) + EPS)) * hn_ref[...]
        o_ref[:, hs] = (on * _silu(hg_ref[:, hs])).astype(o_ref.dtype)

    @pl.when(c == pl.num_programs(1) - 1)
    def _():
        for h in range(heads):
            s_out_ref[h] = st_ref[h].T


def _hgrn(proj, lay, lb, hnorm, s0, layer, bx, tx, t_valid):
    n = proj.shape[0]
    hw = lay["hq"][1]
    heads = hw // HEAD_DIM
    ch = CHUNK
    nc = tx // ch
    cols = [lay[k][0] // hw for k in ("hq", "hf", "hi", "hg")]
    blocks = [((ch, hw), F32)] * 4 + [((heads, HEAD_DIM, HEAD_DIM), F32)] * 2 + [((ch, hw), BF16)]
    scratch = [((heads, HEAD_DIM, HEAD_DIM), F32)]
    in_specs = [pl.BlockSpec((ch, hw), functools.partial(lambda b, c, col: (b * nc + c, col), col=col))
                for col in cols]
    in_specs += [pl.BlockSpec((None, 1, hw), lambda b, c: (layer, 0, 0)),
                 pl.BlockSpec((None, 1, HEAD_DIM), lambda b, c: (layer, 0, 0)),
                 pl.BlockSpec((None, heads, HEAD_DIM, HEAD_DIM), lambda b, c: (b, 0, 0, 0))]
    return pl.pallas_call(
        functools.partial(_hgrn_kernel, heads=heads, t_valid=t_valid),
        grid=(bx, nc),
        in_specs=in_specs,
        out_specs=[pl.BlockSpec((ch, hw), lambda b, c: (b * nc + c, 0)),
                   pl.BlockSpec((None, heads, HEAD_DIM, HEAD_DIM), lambda b, c: (b, 0, 0, 0))],
        out_shape=[jax.ShapeDtypeStruct((n, hw), BF16),
                   jax.ShapeDtypeStruct((bx, heads, HEAD_DIM, HEAD_DIM), F32)],
        scratch_shapes=[pltpu.VMEM(s, t) for s, t in scratch],
        compiler_params=_params(("parallel", "arbitrary"), blocks, scratch),
        name="hgrn2")(proj, proj, proj, proj, lb, hnorm, s0)


def _layout(d_model, fw, gw, hw):
    order = [("gate_a", d_model), ("gate_b", d_model), ("gate_c", d_model), ("fq", fw), ("fk", fw), ("fv", fw),
             ("gqkv", 3 * gw), ("gz", gw), ("hq", hw), ("hf", hw), ("hi", hw), ("hg", hw),
             ("ff", LANES), ("gab", LANES)]
    lay, start = {}, 0
    for name, width in order:
        lay[name] = (start, width)
        start += width
    lay["total"] = (0, start)
    return lay


def _pack_run_kernel(*refs, shift):
    main_ref, o_ref = refs[0], refs[-1]
    y = main_ref[...]
    if shift:
        tc = y.shape[1]
        y = pltpu.roll(jnp.concatenate([y, refs[1][...]], axis=1), tc + LANES - shift, axis=1)[:, :tc]
    o_ref[...] = y.astype(o_ref.dtype)


def _pack_run(buf, w, src_start, dst_start, width):
    depth, d, _ = w.shape
    tr, tc = PACK_ROW_TILE, PACK_COL_TILE
    shift = src_start % LANES
    src_blk, dst_blk = (src_start - shift) // tc, dst_start // tc
    assert (src_start - shift) % tc == 0 and dst_start % tc == 0 and width % tc == 0 and d % tr == 0
    in_specs = [pl.BlockSpec((None, tr, tc), lambda l, r, j: (l, r, src_blk + j))]
    operands = [w]
    if shift:
        in_specs.append(pl.BlockSpec((None, tr, LANES), lambda l, r, j: (l, r, (src_blk + j + 1) * (tc // LANES))))
        operands.append(w)
    in_specs.append(pl.BlockSpec(memory_space=pl.ANY))
    blocks = [((tr, tc), F32), ((tr, LANES), F32), ((tr, tc), BF16)]
    return pl.pallas_call(
        functools.partial(_pack_run_kernel, shift=shift),
        grid=(depth, d // tr, width // tc),
        in_specs=in_specs,
        out_specs=pl.BlockSpec((None, tr, tc), lambda l, r, j: (l, r, dst_blk + j)),
        out_shape=jax.ShapeDtypeStruct(buf.shape, buf.dtype),
        input_output_aliases={len(operands): 0},
        compiler_params=_params(("parallel", "parallel", "parallel"), blocks),
        name="pack_w_in")(*operands, buf)


def _pack_w_in(w_in, lay, d_model, fh, gh, fw, gw, hw):
    splits = [("fq", fw), ("fk", fw), ("fv", fw), ("ff", fh), ("gq", gw), ("gk", gw), ("gv", gw), ("ga", gh),
              ("gb", gh), ("gz", gw), ("hq", hw), ("hf", hw), ("hi", hw), ("hg", hw),
              ("gate_a", d_model), ("gate_b", d_model), ("gate_c", d_model)]
    src, start = {}, 0
    for name, width in splits:
        src[name] = start
        start += width
    assert start == w_in.shape[-1]
    zeros = lambda width: jnp.zeros(w_in.shape[:-1] + (width,), BF16)
    narrow = jnp.concatenate([w_in[..., src["ff"]:src["ff"] + fh].astype(BF16), zeros(LANES - fh),
                              w_in[..., src["ga"]:src["ga"] + 2 * gh].astype(BF16), zeros(LANES - 2 * gh)], axis=-1)
    assert lay["gab"][0] == lay["ff"][0] + LANES and lay["total"][1] == lay["gab"][0] + LANES
    buf = lax.dynamic_update_slice(jnp.zeros(w_in.shape[:-1] + (lay["total"][1],), BF16), narrow,
                                   (0, 0, lay["ff"][0]))
    for first, last, dst in (("gate_a", "gate_c", "gate_a"), ("fq", "fv", "fq"), ("gq", "gv", "gqkv"),
                             ("gz", "hg", "gz")):
        width = src[last] + dict(splits)[last] - src[first]
        buf = _pack_run(buf, w_in, src[first], lay[dst][0], width)
    return buf


def _pad_lanes(x, fill=0.0):
    return jnp.pad(x.astype(F32), ((0, 0), (0, LANES - x.shape[-1])), constant_values=fill)[:, None, :]


def _tile(n, candidates):
    for c in candidates:
        if n % c == 0:
            return c
    raise ValueError(f"no tile for {n} among {candidates}")


def kernel(x_prompt, x_sample, cache_fox_k, cache_fox_v, cache_fox_logf, state_gdn, state_gdn_conv, state_hgrn,
           page_table, norm_mix_pre, norm_mix_post, norm_mlp_pre, norm_mlp_post, w_in, fox_f_bias, gdn_conv_w,
           gdn_a_log, gdn_dt_bias, gdn_norm, hgrn_lower_bounds, hgrn_norm, w_fox_o, w_gdn_o, w_hgrn_o, w_out,
           w_up, w_down):
    bp, tp, d = x_prompt.shape
    bs, ts, _ = x_sample.shape
    depth = w_in.shape[0]
    fh, gh = fox_f_bias.shape[1], gdn_a_log.shape[1]
    fw, gw, hw = fh * HEAD_DIM, gh * HEAD_DIM, hgrn_lower_bounds.shape[1]
    hh = hw // HEAD_DIM
    n_pool, page = cache_fox_k.shape[1], cache_fox_k.shape[2]
    n_pages = page_table.shape[1]
    tsp = CHUNK
    assert tp % CHUNK == 0 and tp % LANES == 0 and ts <= tsp and ts >= CONV_W - 1
    assert page * fh == SUBLANES * LANES and 2 * gh <= LANES and fh <= LANES
    lay = _layout(d, fw, gw, hw)
    wp = lay["total"][1]

    w_in_p = _pack_w_in(w_in, lay, d, fh, gh, fw, gw, hw)
    w_fo, w_go, w_ho, w_o, w_u, w_d = (w.astype(BF16) for w in (w_fox_o, w_gdn_o, w_hgrn_o, w_out, w_up, w_down))
    g_mix_pre, g_mix_post, g_mlp_pre, g_mlp_post = (
        g.astype(F32)[:, None, :] for g in (norm_mix_pre, norm_mix_post, norm_mlp_pre, norm_mlp_post))
    f_bias = _pad_lanes(fox_f_bias)
    a_log, dt_bias = _pad_lanes(gdn_a_log), _pad_lanes(gdn_dt_bias)
    g_norm, h_norm = gdn_norm.astype(F32)[:, None, :], hgrn_norm.astype(F32)[:, None, :]
    conv_w = gdn_conv_w.astype(F32)
    lb_all = _lower_bounds(hgrn_lower_bounds)[:, None, :]

    ck = cache_fox_k.reshape(depth * n_pool, page * fh, HEAD_DIM)
    cv = cache_fox_v.reshape(depth * n_pool, page * fh, HEAD_DIM)
    clf = cache_fox_logf.astype(F32).reshape(depth * n_pool, SUBLANES, LANES)
    bias_page = jnp.tile(fox_f_bias.astype(F32), (1, LANES // fh))[:, None, :] * jnp.ones((1, SUBLANES, 1), F32)
    bias_col = jnp.tile(fox_f_bias.astype(F32), (1, ts))[:, :, None]

    zeros_state_p = jnp.zeros((bp, gh, HEAD_DIM, HEAD_DIM), F32)
    zeros_hstate_p = jnp.zeros((bp, hh, HEAD_DIM, HEAD_DIM), F32)
    zeros_conv_p = jnp.zeros((bp, CONV_W - 1, 3 * gw), F32)

    xp = x_prompt.reshape(bp * tp, d)
    xs = jnp.pad(x_sample, ((0, 0), (0, tsp - ts), (0, 0))).reshape(bs * tsp, d)

    tm_p = _tile(bp * tp, (1024, 512, 256, 128, 64))
    tm_s = _tile(bs * tsp, (512, 256, 128, 64))
    tn_in = _tile(wp, (768, 512, 384, 256, 128))
    tq = _tile(tp, FOX_Q_TILES)

    def dense_tail(x, oa, ob, oc, proj, l, tm):
        merged = _merge(oa, ob, oc, proj, lay, w_fo, w_go, w_ho, l, tm=min(tm, MERGE_ROW_TILE), tn=d)
        x = _matmul_norm_res(merged, w_o, x, g_mix_post, l, tm=min(tm, RESIDUAL_ROW_TILE),
                             tk=_tile(d, RESIDUAL_K_TILES), name="out_proj_norm_res")
        u = _norm_matmul(x, g_mlp_pre, w_u, l, tm=tm, tn=_tile(w_u.shape[-1], (1024, 512, 256)), act="relu2",
                         out_dtype=BF16, name="mlp_up")
        return _matmul_norm_res(u, w_d, x, g_mlp_post, l, tm=min(tm, RESIDUAL_ROW_TILE),
                                tk=_tile(w_d.shape[1], RESIDUAL_K_TILES), name="mlp_down_norm_res")

    outs = {k: [] for k in ("fkp", "fvp", "flp", "fks", "fvs", "fls", "gsp", "gss", "gcp", "gcs", "hsp", "hss")}
    for l in range(depth):
        proj = _norm_matmul(xp, g_mix_pre, w_in_p, l, tm=tm_p, tn=tn_in, name="in_proj")
        lf, c_tok, c_row = _fox_gate(proj, lay, f_bias, l, bp, tp)
        oa = _fox_attn(proj, lay, c_tok, c_row, bp, tp, tq=tq)
        ob, gs, gc = _gdn(proj, lay, conv_w, a_log, dt_bias, g_norm, zeros_state_p, zeros_conv_p, l, bp, tp, tp)
        oc, hs = _hgrn(proj, lay, lb_all, h_norm, zeros_hstate_p, l, bp, tp, tp)
        xp = dense_tail(xp, oa, ob, oc, proj, l, tm_p)
        outs["fkp"].append(proj[:, lay["fk"][0]:lay["fk"][0] + fw].reshape(bp, tp, fh, HEAD_DIM))
        outs["fvp"].append(proj[:, lay["fv"][0]:lay["fv"][0] + fw].reshape(bp, tp, fh, HEAD_DIM))
        outs["flp"].append(lf[:, :fh].reshape(bp, tp, fh))
        outs["gsp"].append(gs)
        outs["gcp"].append(gc)
        outs["hsp"].append(hs)

        proj = _norm_matmul(xs, g_mix_pre, w_in_p, l, tm=tm_s, tn=tn_in, name="in_proj")
        p3 = proj.reshape(bs, tsp, wp)[:, :ts]
        q_new, k_new, v_new = (p3[:, :, lay[k][0]:lay[k][0] + fw] for k in ("fq", "fk", "fv"))
        ff_new = p3[:, :, lay["ff"][0]:lay["ff"][0] + fh]
        rows = ts * fh
        pad_rows = lambda z: jnp.pad(z.reshape(bs, rows, HEAD_DIM), ((0, 0), (0, LANES - rows), (0, 0)))
        ff_page = jnp.pad(ff_new.reshape(bs, 1, rows), ((0, 0), (0, SUBLANES - 1), (0, LANES - rows)))
        oa_s, lf_page = _fox_decode(
            q_new.reshape(bs, rows, HEAD_DIM), pad_rows(k_new), pad_rows(v_new), ff_page, bias_page[l],
            ff_new.reshape(bs, rows, 1), bias_col[l], ck, cv, clf,
            (page_table.astype(jnp.int32) + l * n_pool).reshape(-1), n_pages, heads=fh, n_new=ts)
        oa = jnp.pad(oa_s.reshape(bs, ts, fw), ((0, 0), (0, tsp - ts), (0, 0))).reshape(bs * tsp, fw)
        ob, gs, gc = _gdn(proj, lay, conv_w, a_log, dt_bias, g_norm, state_gdn[l].astype(F32),
                          state_gdn_conv[l].astype(F32), l, bs, tsp, ts)
        oc, hs = _hgrn(proj, lay, lb_all, h_norm, state_hgrn[l].astype(F32), l, bs, tsp, ts)
        xs = dense_tail(xs, oa, ob, oc, proj, l, tm_s)
        outs["fks"].append(k_new.reshape(bs, ts, fh, HEAD_DIM))
        outs["fvs"].append(v_new.reshape(bs, ts, fh, HEAD_DIM))
        outs["fls"].append(lf_page[:, 0, :rows].reshape(bs, ts, fh))
        outs["gss"].append(gs)
        outs["gcs"].append(gc)
        outs["hss"].append(hs)

    st = lambda k: jnp.stack(outs[k])
    return (xp.reshape(bp, tp, d), xs.reshape(bs, tsp, d)[:, :ts],
            st("fkp"), st("fvp"), st("flp"), st("fks"), st("fvs"), st("fls"),
            st("gsp"), st("gss"), st("gcp"), st("gcs"), st("hsp"), st("hss"))
```

```python
import functools

import jax
import jax.numpy as jnp
from jax import lax
from jax.experimental import pallas as pl
from jax.experimental.pallas import tpu as pltpu

F32, BF16 = jnp.float32, jnp.bfloat16
HEAD_DIM = 128
CONV_W = 4
CHUNK = 64
EPS = 1e-6
NEG_BIG = -1e30
LB_FLOOR = 1e-30
LANES = 128
SUBLANES = 8
V7X_VMEM_BYTES = 64 * 2**20
VMEM_RESERVE_BYTES = 6 * 2**20
KERNEL_TEMP_BYTES = 16 * 2**20
DECODE_PAGES_PER_STEP = 8
FOX_Q_TILES = (1024, 512, 256, 128)
IN_PROJ_COL_TILE = 1024
MERGE_ROW_TILE = 256
RESIDUAL_ROW_TILE = 512
RESIDUAL_K_TILES = (2048, 1024, 512, 256)


def _nbytes(shape, dtype):
    n = 1
    for s in shape:
        n *= s
    return n * jnp.dtype(dtype).itemsize


def _params(semantics, blocks, scratch=()):
    need = 2 * sum(_nbytes(s, d) for s, d in blocks) + sum(_nbytes(s, d) for s, d in scratch) + KERNEL_TEMP_BYTES
    return pltpu.CompilerParams(dimension_semantics=semantics,
                                vmem_limit_bytes=min(need, V7X_VMEM_BYTES - VMEM_RESERVE_BYTES))


def _dot(a, b):
    return jnp.dot(a, b, preferred_element_type=F32)


def _dot_nt(a, b):
    return lax.dot_general(a, b, (((1,), (1,)), ((), ())), preferred_element_type=F32)


def _split3(x):
    x1 = x.astype(BF16)
    r = x - x1.astype(F32)
    x2 = r.astype(BF16)
    x3 = (r - x2.astype(F32)).astype(BF16)
    return x1, x2, x3


def _dot01(m01, x):
    x1, x2, x3 = _split3(x)
    return _dot(m01, x1) + (_dot(m01, x2) + _dot(m01, x3))


def _log_sigmoid(x):
    return -(jnp.maximum(-x, 0.0) + jnp.log1p(jnp.exp(-jnp.abs(x))))


def _softplus(x):
    return jnp.maximum(x, 0.0) + jnp.log1p(jnp.exp(-jnp.abs(x)))


def _silu(x):
    return x * jax.nn.sigmoid(x)


def _tril01(n, strict=False):
    r = lax.broadcasted_iota(jnp.int32, (n, n), 0)
    c = lax.broadcasted_iota(jnp.int32, (n, n), 1)
    return (r > c) if strict else (r >= c)


def _lower_bounds_kernel(x_ref, o_ref):
    x = x_ref[...]
    e = jnp.exp(x - jnp.max(x, axis=0, keepdims=True))
    p = e / jnp.sum(e, axis=0, keepdims=True)
    run = jnp.zeros_like(p[0:1])
    for l in range(x.shape[0]):
        run = run + p[l:l + 1]
        o_ref[l:l + 1, :] = run - p[0:1]


def _lower_bounds(x):
    return pl.pallas_call(_lower_bounds_kernel, out_shape=jax.ShapeDtypeStruct(x.shape, F32),
                          name="hgrn_lower_bounds")(x.astype(F32))


def _rmsnorm_bf16(x, g):
    var = jnp.mean(x * x, axis=-1, keepdims=True)
    return ((x * lax.rsqrt(var + EPS)) * g).astype(BF16)


def _rmsnorm_cast_kernel(x_ref, g_ref, o_ref):
    o_ref[...] = _rmsnorm_bf16(x_ref[...], g_ref[...])


def _rmsnorm_cast(x, g, layer, *, tm):
    n, d = x.shape
    blocks = [((tm, d), F32), ((1, d), F32), ((tm, d), BF16)]
    return pl.pallas_call(
        _rmsnorm_cast_kernel,
        grid=(n // tm,),
        in_specs=[pl.BlockSpec((tm, d), lambda i: (i, 0)), pl.BlockSpec((None, 1, d), lambda i: (layer, 0, 0))],
        out_specs=pl.BlockSpec((tm, d), lambda i: (i, 0)),
        out_shape=jax.ShapeDtypeStruct((n, d), BF16),
        compiler_params=_params(("parallel",), blocks),
        name="rmsnorm_cast")(x, g)


def _in_proj_kernel(offs_ref, xn_ref, w_ref, o_ref, w16_ref):
    del offs_ref

    @pl.when(pl.program_id(1) == 0)
    def _():
        w16_ref[...] = w_ref[0].astype(BF16)

    o_ref[...] = _dot_nt(xn_ref[...], w16_ref[...])


def _in_proj(xn, w_t, row_starts, layer, *, tm, tn):
    n, d = xn.shape
    assert all(s % SUBLANES == 0 for s in row_starts)
    offs = jnp.asarray([s // SUBLANES for s in row_starts], jnp.int32)
    blocks = [((tm, d), BF16), ((tn, d), F32), ((tm, tn), F32)]
    scratch = [((tn, d), BF16)]
    grid_spec = pltpu.PrefetchScalarGridSpec(
        num_scalar_prefetch=1, grid=(len(row_starts), n // tm),
        in_specs=[pl.BlockSpec((tm, d), lambda j, i, offs: (i, 0)),
                  pl.BlockSpec((pl.Element(1), pl.Element(tn), pl.Element(d)),
                               lambda j, i, offs: (layer, offs[j] * SUBLANES, 0))],
        out_specs=pl.BlockSpec((tm, tn), lambda j, i, offs: (i, j)),
        scratch_shapes=[pltpu.VMEM(s, t) for s, t in scratch])
    return pl.pallas_call(
        _in_proj_kernel, grid_spec=grid_spec,
        out_shape=jax.ShapeDtypeStruct((n, len(row_starts) * tn), F32),
        compiler_params=_params(("arbitrary", "arbitrary"), blocks, scratch),
        name="in_proj")(offs, xn, w_t)


def _in_proj_narrow_kernel(xn_ref, a_ref, b_ref, o_ref, w16_ref):
    @pl.when(pl.program_id(0) == 0)
    def _():
        pad = lambda w: jnp.zeros((LANES - w.shape[0], w.shape[1]), F32)
        a, b = a_ref[0], b_ref[0]
        w16_ref[...] = jnp.concatenate([a, pad(a), b, pad(b)], axis=0).astype(BF16)

    o_ref[...] = _dot_nt(xn_ref[...], w16_ref[...])


def _in_proj_narrow(xn, w_t, a_start, a_rows, b_start, b_rows, layer, *, tm):
    n, d = xn.shape
    assert a_start % SUBLANES == 0 and b_start % SUBLANES == 0 and a_rows % SUBLANES == 0 and b_rows % SUBLANES == 0
    element = lambda rows, start: pl.BlockSpec((pl.Element(1), pl.Element(rows), pl.Element(d)),
                                               lambda i: (layer, start, 0))
    blocks = [((tm, d), BF16), ((a_rows, d), F32), ((b_rows, d), F32), ((tm, 2 * LANES), F32)]
    scratch = [((2 * LANES, d), BF16)]
    return pl.pallas_call(
        _in_proj_narrow_kernel,
        grid=(n // tm,),
        in_specs=[pl.BlockSpec((tm, d), lambda i: (i, 0)), element(a_rows, a_start), element(b_rows, b_start)],
        out_specs=pl.BlockSpec((tm, 2 * LANES), lambda i: (i, 0)),
        out_shape=jax.ShapeDtypeStruct((n, 2 * LANES), F32),
        scratch_shapes=[pltpu.VMEM(s, t) for s, t in scratch],
        compiler_params=_params(("arbitrary",), blocks, scratch),
        name="in_proj_narrow")(xn, w_t, w_t)


def _mlp_up_kernel(xn_ref, w_ref, o_ref, w16_ref):
    @pl.when(pl.program_id(1) == 0)
    def _():
        w16_ref[...] = w_ref[...].astype(BF16)

    y = jnp.maximum(_dot(xn_ref[...], w16_ref[...]), 0.0)
    o_ref[...] = (y * y).astype(o_ref.dtype)


def _mlp_up(xn, w, layer, *, tm, tn):
    n, d = xn.shape
    f = w.shape[-1]
    blocks = [((tm, d), BF16), ((d, tn), F32), ((tm, tn), BF16)]
    scratch = [((d, tn), BF16)]
    return pl.pallas_call(
        _mlp_up_kernel,
        grid=(f // tn, n // tm),
        in_specs=[pl.BlockSpec((tm, d), lambda j, i: (i, 0)),
                  pl.BlockSpec((None, d, tn), lambda j, i: (layer, 0, j))],
        out_specs=pl.BlockSpec((tm, tn), lambda j, i: (i, j)),
        out_shape=jax.ShapeDtypeStruct((n, f), BF16),
        scratch_shapes=[pltpu.VMEM(s, t) for s, t in scratch],
        compiler_params=_params(("arbitrary", "arbitrary"), blocks, scratch),
        name="mlp_up")(xn, w)


def _matmul_norm_res_kernel(a_ref, w_ref, x_ref, g_ref, gn_ref, o_ref, xn_ref, acc_ref):
    k = pl.program_id(1)

    @pl.when(k == 0)
    def _():
        acc_ref[...] = jnp.zeros_like(acc_ref)

    acc_ref[...] += _dot(a_ref[...], w_ref[...])

    @pl.when(k == pl.num_programs(1) - 1)
    def _():
        y = acc_ref[...]
        var = jnp.mean(y * y, axis=-1, keepdims=True)
        x_new = x_ref[...] + (y * lax.rsqrt(var + EPS)) * g_ref[...]
        o_ref[...] = x_new
        xn_ref[...] = _rmsnorm_bf16(x_new, gn_ref[...])


def _matmul_norm_res(a, w, x, g, layer, g_next, layer_next, *, tm, tk, name):
    n, kdim = a.shape
    d = x.shape[1]
    blocks = [((tm, tk), BF16), ((tk, d), BF16), ((tm, d), F32), ((1, d), F32), ((1, d), F32), ((tm, d), F32),
              ((tm, d), BF16)]
    scratch = [((tm, d), F32)]
    return pl.pallas_call(
        _matmul_norm_res_kernel,
        grid=(n // tm, kdim // tk),
        in_specs=[pl.BlockSpec((tm, tk), lambda i, k: (i, k)),
                  pl.BlockSpec((None, tk, d), lambda i, k: (layer, k, 0)),
                  pl.BlockSpec((tm, d), lambda i, k: (i, 0)),
                  pl.BlockSpec((None, 1, d), lambda i, k: (layer, 0, 0)),
                  pl.BlockSpec((None, 1, d), lambda i, k: (layer_next, 0, 0))],
        out_specs=[pl.BlockSpec((tm, d), lambda i, k: (i, 0)), pl.BlockSpec((tm, d), lambda i, k: (i, 0))],
        out_shape=[jax.ShapeDtypeStruct((n, d), F32), jax.ShapeDtypeStruct((n, d), BF16)],
        scratch_shapes=[pltpu.VMEM(s, t) for s, t in scratch],
        compiler_params=_params(("parallel", "arbitrary"), blocks, scratch),
        name=name)(a, w, x, g, g_next)


def _merge_kernel(oa_ref, ob_ref, oc_ref, ga_ref, gb_ref, gc_ref, wa_ref, wb_ref, wc_ref, o_ref):
    m = jax.nn.sigmoid(ga_ref[...]) * _dot(oa_ref[...], wa_ref[...])
    m = m + jax.nn.sigmoid(gb_ref[...]) * _dot(ob_ref[...], wb_ref[...])
    m = m + jax.nn.sigmoid(gc_ref[...]) * _dot(oc_ref[...], wc_ref[...])
    o_ref[...] = m.astype(o_ref.dtype)


def _merge(oa, ob, oc, proj, lay, wa, wb, wc, layer, *, tm, tn):
    n = oa.shape[0]
    d = wa.shape[-1]
    col = {k: lay[k][0] // tn for k in ("gate_a", "gate_b", "gate_c")}
    o_specs = [pl.BlockSpec((tm, o.shape[1]), lambda i, j: (i, 0)) for o in (oa, ob, oc)]
    g_specs = [pl.BlockSpec((tm, tn), functools.partial(lambda i, j, c: (i, c + j), c=col[k]))
               for k in ("gate_a", "gate_b", "gate_c")]
    w_specs = [pl.BlockSpec((None, w.shape[1], tn), lambda i, j: (layer, 0, j)) for w in (wa, wb, wc)]
    blocks = ([((tm, o.shape[1]), BF16) for o in (oa, ob, oc)] + [((tm, tn), F32)] * 3
              + [((w.shape[1], tn), BF16) for w in (wa, wb, wc)] + [((tm, tn), BF16)])
    return pl.pallas_call(
        _merge_kernel,
        grid=(n // tm, d // tn),
        in_specs=o_specs + g_specs + w_specs,
        out_specs=pl.BlockSpec((tm, tn), lambda i, j: (i, j)),
        out_shape=jax.ShapeDtypeStruct((n, d), BF16),
        compiler_params=_params(("parallel", "parallel"), blocks),
        name="gated_merge")(oa, ob, oc, proj, proj, proj, wa, wb, wc)


def _fox_gate_kernel(ff_ref, b_ref, lf_ref, c_ref, ct_ref):
    t = ff_ref.shape[0]
    lf_ref[...] = _log_sigmoid(ff_ref[...] + b_ref[...])
    tri = _tril01(LANES).astype(BF16)
    carry = jnp.zeros((1, LANES), F32)
    for blk in range(t // LANES):
        rows = slice(blk * LANES, (blk + 1) * LANES)
        cb = _dot01(tri, lf_ref[rows, :]) + carry
        c_ref[rows, :] = cb
        ct_ref[:, rows] = cb.T
        carry = cb[LANES - 1:LANES, :]


def _fox_gate(proj, lay, bias, layer, bx, tx):
    n = proj.shape[0]
    col = lay["ff"][0] // LANES
    blocks = [((tx, LANES), F32)] * 4 + [((LANES, tx), F32)]
    return pl.pallas_call(
        _fox_gate_kernel,
        grid=(bx,),
        in_specs=[pl.BlockSpec((tx, LANES), lambda b: (b, col)),
                  pl.BlockSpec((None, 1, LANES), lambda b: (layer, 0, 0))],
        out_specs=[pl.BlockSpec((tx, LANES), lambda b: (b, 0)),
                   pl.BlockSpec((tx, LANES), lambda b: (b, 0)),
                   pl.BlockSpec((None, LANES, tx), lambda b: (b, 0, 0))],
        out_shape=[jax.ShapeDtypeStruct((n, LANES), F32), jax.ShapeDtypeStruct((n, LANES), F32),
                   jax.ShapeDtypeStruct((bx, LANES, tx), F32)],
        compiler_params=_params(("parallel",), blocks),
        name="fox_gate")(proj, bias)


def _fox_attn_kernel(q_ref, k_ref, v_ref, ct_ref, cr_ref, o_ref, m_ref, l_ref, acc_ref, *, heads, scale):
    i, j = pl.program_id(1), pl.program_id(2)
    tq, tk = q_ref.shape[0], k_ref.shape[0]

    @pl.when(j == 0)
    def _():
        m_ref[...] = jnp.full_like(m_ref, NEG_BIG)
        l_ref[...] = jnp.zeros_like(l_ref)
        acc_ref[...] = jnp.zeros_like(acc_ref)

    @pl.when(j <= i)
    def _():
        q_pos = i * tq + lax.broadcasted_iota(jnp.int32, (tq, tk), 0)
        k_pos = j * tk + lax.broadcasted_iota(jnp.int32, (tq, tk), 1)
        causal = q_pos >= k_pos
        for h in range(heads):
            hs = slice(h * HEAD_DIM, (h + 1) * HEAD_DIM)
            s = _dot_nt(q_ref[:, hs].astype(BF16), k_ref[:, hs].astype(BF16)) * scale
            s = s + ct_ref[:, h:h + 1] - cr_ref[h:h + 1, :]
            s = jnp.where(causal, s, NEG_BIG)
            m_prev = m_ref[:, h:h + 1]
            m_new = jnp.maximum(m_prev, jnp.max(s, axis=-1, keepdims=True))
            alpha = jnp.exp(m_prev - m_new)
            p = jnp.exp(s - m_new)
            l_ref[:, h:h + 1] = alpha * l_ref[:, h:h + 1] + jnp.sum(p, axis=-1, keepdims=True)
            acc_ref[:, hs] = alpha * acc_ref[:, hs] + _dot(p.astype(BF16), v_ref[:, hs].astype(BF16))
            m_ref[:, h:h + 1] = m_new

    @pl.when(j == i)
    def _():
        for h in range(heads):
            hs = slice(h * HEAD_DIM, (h + 1) * HEAD_DIM)
            o_ref[:, hs] = (acc_ref[:, hs] / l_ref[:, h:h + 1]).astype(o_ref.dtype)


def _fox_attn(proj, lay, c_tok, c_row, bx, tx, *, tq):
    n = proj.shape[0]
    fw = lay["fq"][1]
    heads = fw // HEAD_DIM
    nq = tx // tq
    qc, kc, vc = (lay[k][0] // fw for k in ("fq", "fk", "fv"))
    blocks = [((tq, fw), F32)] * 3 + [((tq, LANES), F32), ((LANES, tq), F32), ((tq, fw), BF16)]
    scratch = [((tq, LANES), F32), ((tq, LANES), F32), ((tq, fw), F32)]
    return pl.pallas_call(
        functools.partial(_fox_attn_kernel, heads=heads, scale=HEAD_DIM ** -0.5),
        grid=(bx, nq, nq),
        in_specs=[pl.BlockSpec((tq, fw), lambda b, i, j: (b * nq + i, qc)),
                  pl.BlockSpec((tq, fw), lambda b, i, j: (b * nq + jnp.minimum(j, i), kc)),
                  pl.BlockSpec((tq, fw), lambda b, i, j: (b * nq + jnp.minimum(j, i), vc)),
                  pl.BlockSpec((tq, LANES), lambda b, i, j: (b * nq + i, 0)),
                  pl.BlockSpec((None, LANES, tq), lambda b, i, j: (b, 0, jnp.minimum(j, i)))],
        out_specs=pl.BlockSpec((tq, fw), lambda b, i, j: (b * nq + i, 0)),
        out_shape=jax.ShapeDtypeStruct((n, fw), BF16),
        scratch_shapes=[pltpu.VMEM(s, t) for s, t in scratch],
        compiler_params=_params(("parallel", "parallel", "arbitrary"), blocks, scratch),
        name="fox_attn_prompt")(proj, proj, proj, c_tok, c_row)


def _page_suffix(x, lane, sub, heads):
    y = x
    sh = heads
    while sh < LANES:
        y = y + jnp.where(lane < LANES - sh, pltpu.roll(y, LANES - sh, axis=1), 0.0)
        sh *= 2
    t = jnp.where(lane < heads, y, 0.0)
    sh = heads
    while sh < LANES:
        t = t + pltpu.roll(t, sh, axis=1)
        sh *= 2
    z = t
    sh = 1
    while sh < SUBLANES:
        z = z + jnp.where(sub < SUBLANES - sh, pltpu.roll(z, SUBLANES - sh, axis=0), 0.0)
        sh *= 2
    return (y - x) + (z - t), z[0:1, :]


def _fox_decode_kernel(pt_ref, q_ref, kn_ref, vn_ref, ffp_ref, bp_ref, ffc_ref, bc_ref, *rest,
                       pages, heads, n_new, scale):
    del pt_ref
    k_refs, v_refs, lf_refs = rest[:pages], rest[pages:2 * pages], rest[2 * pages:3 * pages]
    o_ref, lfo_ref, m_ref, l_ref, acc_ref, carry_ref, rn_ref = rest[3 * pages:]
    step = pl.program_id(1)
    rows = q_ref.shape[0]
    lane = lax.broadcasted_iota(jnp.int32, (SUBLANES, LANES), 1)
    sub = lax.broadcasted_iota(jnp.int32, (SUBLANES, LANES), 0)
    r_i = lax.broadcasted_iota(jnp.int32, (rows, LANES), 0)
    c_i = lax.broadcasted_iota(jnp.int32, (rows, LANES), 1)
    head_ok = (r_i % heads) == (c_i % heads)
    qb = q_ref[...].astype(BF16)

    def update(scores, values):
        m_prev = m_ref[...]
        m_new = m_prev
        for s in scores:
            m_new = jnp.maximum(m_new, jnp.max(s, axis=-1, keepdims=True))
        alpha = jnp.exp(m_prev - m_new)
        l_new = alpha * l_ref[...]
        acc = alpha * acc_ref[...]
        for s, v in zip(scores, values):
            p = jnp.exp(s - m_new)
            l_new = l_new + jnp.sum(p, axis=-1, keepdims=True)
            acc = acc + _dot(p.astype(BF16), v)
        m_ref[...] = m_new
        l_ref[...] = l_new
        acc_ref[...] = acc

    @pl.when(step == 0)
    def _():
        m_ref[...] = jnp.full_like(m_ref, NEG_BIG)
        l_ref[...] = jnp.zeros_like(l_ref)
        acc_ref[...] = jnp.zeros_like(acc_ref)
        new_ok = (sub == 0) & (lane < n_new * heads)
        lf = jnp.where(new_ok, _log_sigmoid(ffp_ref[...] + bp_ref[...]), 0.0)
        lfo_ref[...] = lf
        gate, total = _page_suffix(lf, lane, sub, heads)
        carry_ref[...] = total
        lfc = _log_sigmoid(ffc_ref[...] + bc_ref[...])
        run = jnp.zeros((heads, 1), F32)
        for t in range(n_new - 1, -1, -1):
            rn_ref[t * heads:(t + 1) * heads, :] = run
            run = run + lfc[t * heads:(t + 1) * heads, :]
        s = _dot_nt(qb, kn_ref[...].astype(BF16)) * scale + gate[0:1, :] - rn_ref[...]
        ok = head_ok & (c_i // heads <= r_i // heads) & (c_i < n_new * heads)
        update([jnp.where(ok, s, NEG_BIG)], [vn_ref[...].astype(BF16)])

    carry = carry_ref[...]
    rn = rn_ref[...]
    scores = []
    for r in range(pages):
        gate, total = _page_suffix(lf_refs[r][...], lane, sub, heads)
        gate = gate + carry
        carry = carry + total
        s_page = _dot_nt(qb, k_refs[r][...].astype(BF16)) * scale
        scores.append([jnp.where(head_ok, s_page[:, c * LANES:(c + 1) * LANES] + gate[c:c + 1, :] - rn, NEG_BIG)
                       for c in range(SUBLANES)])
    carry_ref[...] = carry
    m_prev = m_ref[...]
    m_new = m_prev
    for page_scores in scores:
        for s in page_scores:
            m_new = jnp.maximum(m_new, jnp.max(s, axis=-1, keepdims=True))
    alpha = jnp.exp(m_prev - m_new)
    l_new = alpha * l_ref[...]
    acc = alpha * acc_ref[...]
    for r in range(pages):
        probs = [jnp.exp(s - m_new) for s in scores[r]]
        for p in probs:
            l_new = l_new + jnp.sum(p, axis=-1, keepdims=True)
        acc = acc + _dot(jnp.concatenate(probs, axis=-1).astype(BF16), v_refs[r][...].astype(BF16))
    m_ref[...] = m_new
    l_ref[...] = l_new
    acc_ref[...] = acc

    @pl.when(step == pl.num_programs(1) - 1)
    def _():
        o_ref[...] = (acc_ref[...] / l_ref[...]).astype(o_ref.dtype)


def _fox_decode(q, k_new, v_new, ff_page, bias_page, ff_col, bias_col, cache_k, cache_v, cache_lf, pages_flat,
                n_pages, *, heads, n_new):
    bs, rows, _ = q.shape
    page_rows = cache_k.shape[1]
    assert page_rows == SUBLANES * LANES and heads * (LANES // heads) == LANES and n_pages % DECODE_PAGES_PER_STEP == 0
    pages = DECODE_PAGES_PER_STEP
    n_steps = n_pages // pages

    def page_map(r):
        return lambda b, s, pt: (pt[b * n_pages + (n_pages - 1 - (s * pages + r))], 0, 0)

    in_specs = [pl.BlockSpec((None, rows, HEAD_DIM), lambda b, s, pt: (b, 0, 0)),
                pl.BlockSpec((None, LANES, HEAD_DIM), lambda b, s, pt: (b, 0, 0)),
                pl.BlockSpec((None, LANES, HEAD_DIM), lambda b, s, pt: (b, 0, 0)),
                pl.BlockSpec((None, SUBLANES, LANES), lambda b, s, pt: (b, 0, 0)),
                pl.BlockSpec((SUBLANES, LANES), lambda b, s, pt: (0, 0)),
                pl.BlockSpec((None, rows, 1), lambda b, s, pt: (b, 0, 0)),
                pl.BlockSpec((rows, 1), lambda b, s, pt: (0, 0))]
    in_specs += [pl.BlockSpec((None, page_rows, HEAD_DIM), page_map(r)) for r in range(pages)] * 2
    in_specs += [pl.BlockSpec((None, SUBLANES, LANES), page_map(r)) for r in range(pages)]
    blocks = [((page_rows, HEAD_DIM), F32)] * (2 * pages) + [((LANES, HEAD_DIM), F32)] * 4
    scratch = [((rows, 1), F32), ((rows, 1), F32), ((rows, HEAD_DIM), F32), ((1, LANES), F32), ((rows, 1), F32)]
    grid_spec = pltpu.PrefetchScalarGridSpec(
        num_scalar_prefetch=1, grid=(bs, n_steps), in_specs=in_specs,
        out_specs=[pl.BlockSpec((None, rows, HEAD_DIM), lambda b, s, pt: (b, 0, 0)),
                   pl.BlockSpec((None, SUBLANES, LANES), lambda b, s, pt: (b, 0, 0))],
        scratch_shapes=[pltpu.VMEM(s, t) for s, t in scratch])
    return pl.pallas_call(
        functools.partial(_fox_decode_kernel, pages=pages, heads=heads, n_new=n_new, scale=HEAD_DIM ** -0.5),
        grid_spec=grid_spec,
        out_shape=[jax.ShapeDtypeStruct((bs, rows, HEAD_DIM), BF16),
                   jax.ShapeDtypeStruct((bs, SUBLANES, LANES), F32)],
        compiler_params=_params(("parallel", "arbitrary"), blocks, scratch),
        name="fox_decode")(pages_flat, q, k_new, v_new, ff_page, bias_page, ff_col, bias_col,
                           *([cache_k] * pages), *([cache_v] * pages), *([cache_lf] * pages))


def _dot_hi(a, b):
    a_hi = a.astype(BF16).astype(F32)
    a16 = jnp.concatenate([a_hi, a - a_hi], axis=1).astype(BF16)
    b_hi = b.astype(BF16)
    b_lo = (b - b_hi.astype(F32)).astype(BF16)
    return _dot(jnp.concatenate([a16, a16], axis=1), jnp.concatenate([b_hi, b_hi, b_lo, b_lo], axis=0))


def _unit_lower_solves(ms, rhss):
    c = ms[0].shape[0]
    assert 2 * c == LANES
    xs = [r - _dot_hi(m, r) for m, r in zip(ms, rhss)]
    ps, power = list(ms), 1
    while 2 * power < c:
        ps = [_dot_hi(p, p) for p in ps]
        power *= 2
        xs = [x + _dot_hi(p, x) for p, x in zip(ps, xs)]
    return xs


def _gdn_prep_kernel(qkv_ref, ab_ref, cw_ref, al_ref, dt_ref, cb0_ref,
                     u_ref, w_ref, qe_ref, kdt_ref, attn_ref, egl_ref, cb_out_ref, xe_ref, *, heads, t_valid):
    c = pl.program_id(1)
    ch = qkv_ref.shape[0]
    width = heads * HEAD_DIM
    tail = CONV_W - 1
    base = SUBLANES
    last_chunk = (t_valid - 1) // ch
    last_len = t_valid - last_chunk * ch

    @pl.when(c == 0)
    def _():
        xe_ref[base - tail:base, :] = cb0_ref[...]

    xe_ref[base:base + ch, :] = qkv_ref[...]
    y = xe_ref[base - tail:base - tail + ch, :] * cw_ref[0:1, :]
    for i in range(1, CONV_W):
        y = y + xe_ref[base - tail + i:base - tail + i + ch, :] * cw_ref[i:i + 1, :]
    y = _silu(y)

    @pl.when(c == last_chunk)
    def _():
        cb_out_ref[...] = xe_ref[base + last_len - tail:base + last_len, :]

    xe_ref[base - tail:base, :] = xe_ref[base + ch - tail:base + ch, :]

    valid = (c * ch + lax.broadcasted_iota(jnp.int32, (ch, 1), 0)) < t_valid
    ab = ab_ref[...]
    g_all = jnp.where(valid, -jnp.exp(al_ref[...]) * _softplus(ab + dt_ref[...]), 0.0)
    beta_all = jnp.where(valid, jax.nn.sigmoid(ab), 0.0)
    cum = _dot01(_tril01(ch).astype(BF16), g_all)
    cum_t = cum.T
    lower = _tril01(ch)
    strict = _tril01(ch, strict=True)
    egl_ref[...] = jnp.exp(cum[ch - 1:ch, :])

    hr = range(heads)
    sl = lambda part, h: slice(part * width + h * HEAD_DIM, part * width + (h + 1) * HEAD_DIM)
    gc = [cum[:, h:h + 1] for h in hr]
    beta = [beta_all[:, heads + h:heads + h + 1] for h in hr]
    decay = [jnp.exp(jnp.where(lower, gc[h] - cum_t[h:h + 1, :], NEG_BIG)) for h in hr]
    q = [y[:, sl(0, h)] for h in hr]
    k = [y[:, sl(1, h)] for h in hr]
    q = [x * lax.rsqrt(jnp.sum(x * x, axis=-1, keepdims=True) + EPS) * (HEAD_DIM ** -0.5) for x in q]
    k = [x * lax.rsqrt(jnp.sum(x * x, axis=-1, keepdims=True) + EPS) for x in k]
    kb = [k[h] * beta[h] for h in hr]
    k16 = [x.astype(BF16) for x in k]
    eg = [jnp.exp(gc[h]) for h in hr]
    sols = _unit_lower_solves(
        [jnp.where(strict, _dot_nt(kb[h].astype(BF16), k16[h]) * decay[h], 0.0) for h in hr],
        [jnp.concatenate([y[:, sl(2, h)] * beta[h], kb[h] * eg[h]], axis=-1) for h in hr])
    for h in hr:
        sol = sols[h]
        u_ref[:, sl(0, h)] = sol[:, :HEAD_DIM]
        w_ref[:, sl(0, h)] = sol[:, HEAD_DIM:].astype(BF16)
        qe_ref[:, sl(0, h)] = (q[h] * eg[h]).astype(BF16)
        attn_ref[h] = (_dot_nt(q[h].astype(BF16), k16[h]) * decay[h]).astype(BF16)
        kdt_ref[h] = (k[h] * jnp.exp(cum[ch - 1:ch, h:h + 1] - gc[h])).T.astype(BF16)


def _gdn_scan_kernel(u_ref, w_ref, qe_ref, kdt_ref, attn_ref, egl_ref, gz_ref, gn_ref, s0_ref,
                     o_ref, s_out_ref, st_ref, *, heads):
    c = pl.program_id(0)
    bx = u_ref.shape[0]

    @pl.when(c == 0)
    def _():
        st_ref[...] = s0_ref[...]

    chains = [(b, h) for b in range(bx) for h in range(heads)]
    hs = lambda h: slice(h * HEAD_DIM, (h + 1) * HEAD_DIM)
    s_old = [st_ref[b, h] for b, h in chains]
    s16 = [s.astype(BF16) for s in s_old]
    ws = [_dot(w_ref[b, :, hs(h)], s) for (b, h), s in zip(chains, s16)]
    qs = [_dot(qe_ref[b, :, hs(h)], s) for (b, h), s in zip(chains, s16)]
    vn16 = [(u_ref[b, :, hs(h)] - x).astype(BF16) for (b, h), x in zip(chains, ws)]
    for i, (b, h) in enumerate(chains):
        o = qs[i] + _dot(attn_ref[b, h], vn16[i])
        st_ref[b, h] = s_old[i] * egl_ref[b, :, h:h + 1] + _dot(kdt_ref[b, h], vn16[i])
        on = (o * lax.rsqrt(jnp.mean(o * o, axis=-1, keepdims=True) + EPS)) * gn_ref[...]
        o_ref[b, :, hs(h)] = (on * _silu(gz_ref[b, :, hs(h)])).astype(o_ref.dtype)

    @pl.when(c == pl.num_programs(0) - 1)
    def _():
        s_out_ref[...] = st_ref[...]


def _gdn(proj, narrow, lay, conv_w, a_log, dt_bias, gnorm, s0, cb0, layer, bx, tx, t_valid):
    n, wp = proj.shape
    gw = lay["gz"][1]
    heads = gw // HEAD_DIM
    ch = CHUNK
    nc = tx // ch
    qkv_c, ab_c, gz_c = lay["gqkv"][0] // (3 * gw), lay["gab"][0] // LANES, lay["gz"][0] // gw
    blocks = [((ch, 3 * gw), F32), ((ch, LANES), F32), ((CONV_W, 3 * gw), F32), ((CONV_W - 1, 3 * gw), F32),
              ((ch, gw), F32), ((ch, gw), BF16), ((ch, gw), BF16), ((heads, HEAD_DIM, LANES), BF16),
              ((heads, ch, LANES), BF16), ((CONV_W - 1, 3 * gw), F32)]
    scratch = [((SUBLANES + ch, 3 * gw), F32)]
    u, w16, qe16, kdt, attn, egl, cb_out = pl.pallas_call(
        functools.partial(_gdn_prep_kernel, heads=heads, t_valid=t_valid),
        grid=(bx, nc),
        in_specs=[pl.BlockSpec((ch, 3 * gw), lambda b, c: (b * nc + c, qkv_c)),
                  pl.BlockSpec((ch, LANES), lambda b, c: (b * nc + c, ab_c)),
                  pl.BlockSpec((None, CONV_W, 3 * gw), lambda b, c: (layer, 0, 0)),
                  pl.BlockSpec((None, 1, LANES), lambda b, c: (layer, 0, 0)),
                  pl.BlockSpec((None, 1, LANES), lambda b, c: (layer, 0, 0)),
                  pl.BlockSpec((None, CONV_W - 1, 3 * gw), lambda b, c: (b, 0, 0))],
        out_specs=[pl.BlockSpec((ch, gw), lambda b, c: (b * nc + c, 0)),
                   pl.BlockSpec((ch, gw), lambda b, c: (b * nc + c, 0)),
                   pl.BlockSpec((ch, gw), lambda b, c: (b * nc + c, 0)),
                   pl.BlockSpec((None, None, heads, HEAD_DIM, ch), lambda b, c: (b, c, 0, 0, 0)),
                   pl.BlockSpec((None, None, heads, ch, ch), lambda b, c: (b, c, 0, 0, 0)),
                   pl.BlockSpec((None, None, 1, LANES), lambda b, c: (b, c, 0, 0)),
                   pl.BlockSpec((None, CONV_W - 1, 3 * gw), lambda b, c: (b, 0, 0))],
        out_shape=[jax.ShapeDtypeStruct((n, gw), F32), jax.ShapeDtypeStruct((n, gw), BF16),
                   jax.ShapeDtypeStruct((n, gw), BF16),
                   jax.ShapeDtypeStruct((bx, nc, heads, HEAD_DIM, ch), BF16),
                   jax.ShapeDtypeStruct((bx, nc, heads, ch, ch), BF16),
                   jax.ShapeDtypeStruct((bx, nc, 1, LANES), F32),
                   jax.ShapeDtypeStruct((bx, CONV_W - 1, 3 * gw), F32)],
        scratch_shapes=[pltpu.VMEM(s, t) for s, t in scratch],
        compiler_params=_params(("parallel", "arbitrary"), blocks, scratch),
        name="gdn_prep")(proj, narrow, conv_w, a_log, dt_bias, cb0)

    state = (bx, heads, HEAD_DIM, HEAD_DIM)
    blocks = [((bx, ch, gw), F32), ((bx, ch, gw), BF16), ((bx, ch, gw), BF16), ((bx, heads, HEAD_DIM, LANES), BF16),
              ((bx, heads, ch, LANES), BF16), ((bx, ch, gw), F32), (state, F32), ((bx, ch, gw), BF16), (state, F32)]
    scratch = [(state, F32)]
    rows3 = lambda a: a.reshape(bx, tx, a.shape[-1])
    o, s_out = pl.pallas_call(
        functools.partial(_gdn_scan_kernel, heads=heads),
        grid=(nc,),
        in_specs=[pl.BlockSpec((bx, ch, gw), lambda c: (0, c, 0)),
                  pl.BlockSpec((bx, ch, gw), lambda c: (0, c, 0)),
                  pl.BlockSpec((bx, ch, gw), lambda c: (0, c, 0)),
                  pl.BlockSpec((bx, None, heads, HEAD_DIM, ch), lambda c: (0, c, 0, 0, 0)),
                  pl.BlockSpec((bx, None, heads, ch, ch), lambda c: (0, c, 0, 0, 0)),
                  pl.BlockSpec((bx, None, 1, LANES), lambda c: (0, c, 0, 0)),
                  pl.BlockSpec((bx, ch, gw), lambda c: (0, c, gz_c)),
                  pl.BlockSpec((None, 1, HEAD_DIM), lambda c: (layer, 0, 0)),
                  pl.BlockSpec(state, lambda c: (0, 0, 0, 0))],
        out_specs=[pl.BlockSpec((bx, ch, gw), lambda c: (0, c, 0)),
                   pl.BlockSpec(state, lambda c: (0, 0, 0, 0))],
        out_shape=[jax.ShapeDtypeStruct((bx, tx, gw), BF16), jax.ShapeDtypeStruct(state, F32)],
        scratch_shapes=[pltpu.VMEM(s, t) for s, t in scratch],
        compiler_params=_params(("arbitrary",), blocks, scratch),
        name="gdn_scan")(rows3(u), rows3(w16), rows3(qe16), kdt, attn, egl, rows3(proj), gnorm, s0)
    return o.reshape(n, gw), s_out, cb_out


def _hgrn_kernel(hq_ref, hf_ref, hi_ref, hg_ref, lb_ref, hn_ref, s0_ref, o_ref, s_out_ref, st_ref,
                 *, heads, t_valid):
    c = pl.program_id(1)
    ch = hq_ref.shape[0]
    nblk = ch // SUBLANES

    @pl.when(c == 0)
    def _():
        for h in range(heads):
            st_ref[h] = s0_ref[h].T

    tok = lax.broadcasted_iota(jnp.int32, (ch, 1), 0)
    valid = (c * ch + tok) < t_valid
    lb = lb_ref[...]
    hf = hf_ref[...]
    a = jnp.log(jnp.maximum(lb, LB_FLOOR))
    b = jnp.log1p(-lb) + _log_sigmoid(hf)
    log_f = jnp.maximum(a, b) + jnp.log1p(jnp.exp(-jnp.abs(a - b)))
    log_f = jnp.where(valid, log_f, 0.0)
    k_all = jnp.where(valid, (1.0 - lb) * jax.nn.sigmoid(-hf), 0.0)
    cum_all = _dot01(_tril01(ch).astype(BF16), log_f)
    q_all = _silu(hq_ref[...])

    ti = lax.broadcasted_iota(jnp.int32, (ch, ch), 0)
    tj = lax.broadcasted_iota(jnp.int32, (ch, ch), 1)
    sub3 = lax.broadcasted_iota(jnp.int32, (nblk, SUBLANES, HEAD_DIM), 1)

    for h in range(heads):
        hs = slice(h * HEAD_DIM, (h + 1) * HEAD_DIM)
        g, q, k, v = cum_all[:, hs], q_all[:, hs], k_all[:, hs], hi_ref[:, hs]
        attn_t = jnp.zeros((ch, ch), F32)
        half = ch // 2
        while half >= SUBLANES:
            blk = 2 * half
            g3 = g.reshape(ch // blk, blk, HEAD_DIM)
            e = jnp.exp(-jnp.abs(g3 - g3[:, half - 1:half, :])).reshape(ch, HEAD_DIM)
            upper = (tok % blk) >= half
            qa = jnp.where(upper, q * e, 0.0).astype(BF16)
            kb = jnp.where(upper, 0.0, k * e).astype(BF16)
            attn_t = attn_t + jnp.where((ti // blk) == (tj // blk), _dot_nt(kb, qa), 0.0)
            half //= 2
        g3, q3, k3 = (z.reshape(nblk, SUBLANES, HEAD_DIM) for z in (g, q, k))
        cols = []
        for i in range(SUBLANES):
            e = jnp.exp(jnp.where(sub3 <= i, g3[:, i:i + 1, :] - g3, NEG_BIG))
            w = jnp.sum(e * k3 * q3[:, i:i + 1, :], axis=-1, keepdims=True).reshape(ch, 1)
            cols.append(jnp.where((tj % SUBLANES) == i, w, 0.0))
        while len(cols) > 1:
            cols = [a + b for a, b in zip(cols[0::2], cols[1::2])]
        attn_t = attn_t + jnp.where((ti // SUBLANES) == (tj // SUBLANES), cols[0], 0.0)
        st = st_ref[h]
        o = _dot_nt((q * jnp.exp(g)).astype(BF16), st.astype(BF16)) + _dot(attn_t.T.astype(BF16), v.astype(BF16))
        gl = g[ch - 1:ch, :]
        kd = k * jnp.exp(gl - g)
        st_ref[h] = st * jnp.exp(gl) + _dot(v.T.astype(BF16), kd.astype(BF16))
        on = (o * lax.rsqrt(jnp.mean(o * o, axis=-1, keepdims=True) + EPS)) * hn_ref[...]
        o_ref[:, hs] = (on * _silu(hg_ref[:, hs])).astype(o_ref.dtype)

    @pl.when(c == pl.num_programs(1) - 1)
    def _():
        for h in range(heads):
            s_out_ref[h] = st_ref[h].T


def _hgrn(proj, lay, lb, hnorm, s0, layer, bx, tx, t_valid):
    n = proj.shape[0]
    hw = lay["hq"][1]
    heads = hw // HEAD_DIM
    ch = CHUNK
    nc = tx // ch
    cols = [lay[k][0] // hw for k in ("hq", "hf", "hi", "hg")]
    blocks = [((ch, hw), F32)] * 4 + [((heads, HEAD_DIM, HEAD_DIM), F32)] * 2 + [((ch, hw), BF16)]
    scratch = [((heads, HEAD_DIM, HEAD_DIM), F32)]
    in_specs = [pl.BlockSpec((ch, hw), functools.partial(lambda b, c, col: (b * nc + c, col), col=col))
                for col in cols]
    in_specs += [pl.BlockSpec((None, 1, hw), lambda b, c: (layer, 0, 0)),
                 pl.BlockSpec((None, 1, HEAD_DIM), lambda b, c: (layer, 0, 0)),
                 pl.BlockSpec((None, heads, HEAD_DIM, HEAD_DIM), lambda b, c: (b, 0, 0, 0))]
    return pl.pallas_call(
        functools.partial(_hgrn_kernel, heads=heads, t_valid=t_valid),
        grid=(bx, nc),
        in_specs=in_specs,
        out_specs=[pl.BlockSpec((ch, hw), lambda b, c: (b * nc + c, 0)),
                   pl.BlockSpec((None, heads, HEAD_DIM, HEAD_DIM), lambda b, c: (b, 0, 0, 0))],
        out_shape=[jax.ShapeDtypeStruct((n, hw), BF16),
                   jax.ShapeDtypeStruct((bx, heads, HEAD_DIM, HEAD_DIM), F32)],
        scratch_shapes=[pltpu.VMEM(s, t) for s, t in scratch],
        compiler_params=_params(("parallel", "arbitrary"), blocks, scratch),
        name="hgrn2")(proj, proj, proj, proj, lb, hnorm, s0)


def _layout(d_model, fw, gw, hw):
    order = [("gate_a", d_model), ("gate_b", d_model), ("gate_c", d_model), ("fq", fw), ("fk", fw), ("fv", fw),
             ("gqkv", 3 * gw), ("gz", gw), ("hq", hw), ("hf", hw), ("hi", hw), ("hg", hw)]
    lay, start = {}, 0
    for name, width in order:
        lay[name] = (start, width)
        start += width
    lay["total"] = (0, start)
    lay["ff"], lay["gab"] = (0, LANES), (LANES, LANES)
    return lay


def _source_columns(d_model, fh, gh, fw, gw, hw):
    splits = [("fq", fw), ("fk", fw), ("fv", fw), ("ff", fh), ("gq", gw), ("gk", gw), ("gv", gw), ("ga", gh),
              ("gb", gh), ("gz", gw), ("hq", hw), ("hf", hw), ("hi", hw), ("hg", hw),
              ("gate_a", d_model), ("gate_b", d_model), ("gate_c", d_model)]
    src, start = {}, 0
    for name, width in splits:
        src[name] = start
        start += width
    src["end"] = start
    return src


def _in_proj_row_starts(lay, src, tn):
    starts = []
    for first, dst, width in (("gate_a", "gate_a", 3 * lay["gate_a"][1]), ("fq", "fq", 3 * lay["fq"][1]),
                              ("gq", "gqkv", lay["gqkv"][1]), ("gz", "gz", lay["total"][1] - lay["gz"][0])):
        assert lay[dst][0] == len(starts) * tn and width % tn == 0
        starts += [src[first] + k * tn for k in range(width // tn)]
    assert len(starts) * tn == lay["total"][1]
    return starts


def _pad_lanes(x, fill=0.0):
    return jnp.pad(x.astype(F32), ((0, 0), (0, LANES - x.shape[-1])), constant_values=fill)[:, None, :]


def _tile(n, candidates):
    for c in candidates:
        if n % c == 0:
            return c
    raise ValueError(f"no tile for {n} among {candidates}")


def kernel(x_prompt, x_sample, cache_fox_k, cache_fox_v, cache_fox_logf, state_gdn, state_gdn_conv, state_hgrn,
           page_table, norm_mix_pre, norm_mix_post, norm_mlp_pre, norm_mlp_post, w_in, fox_f_bias, gdn_conv_w,
           gdn_a_log, gdn_dt_bias, gdn_norm, hgrn_lower_bounds, hgrn_norm, w_fox_o, w_gdn_o, w_hgrn_o, w_out,
           w_up, w_down):
    bp, tp, d = x_prompt.shape
    bs, ts, _ = x_sample.shape
    depth = w_in.shape[0]
    fh, gh = fox_f_bias.shape[1], gdn_a_log.shape[1]
    fw, gw, hw = fh * HEAD_DIM, gh * HEAD_DIM, hgrn_lower_bounds.shape[1]
    hh = hw // HEAD_DIM
    n_pool, page = cache_fox_k.shape[1], cache_fox_k.shape[2]
    n_pages = page_table.shape[1]
    tsp = CHUNK
    assert tp % CHUNK == 0 and tp % LANES == 0 and ts <= tsp and ts >= CONV_W - 1
    assert page * fh == SUBLANES * LANES and 2 * gh <= LANES and fh <= LANES
    lay = _layout(d, fw, gw, hw)
    wp = lay["total"][1]
    src = _source_columns(d, fh, gh, fw, gw, hw)
    assert src["end"] == w_in.shape[-1] and src["gb"] == src["ga"] + gh
    row_starts = _in_proj_row_starts(lay, src, IN_PROJ_COL_TILE)

    w_in_t = jnp.swapaxes(w_in.astype(F32), 1, 2)
    w_u = w_up.astype(F32)
    w_fo, w_go, w_ho, w_o, w_d = (w.astype(BF16) for w in (w_fox_o, w_gdn_o, w_hgrn_o, w_out, w_down))
    g_mix_pre, g_mix_post, g_mlp_pre, g_mlp_post = (
        g.astype(F32)[:, None, :] for g in (norm_mix_pre, norm_mix_post, norm_mlp_pre, norm_mlp_post))
    f_bias = _pad_lanes(fox_f_bias)
    a_log, dt_bias = _pad_lanes(gdn_a_log), _pad_lanes(gdn_dt_bias)
    g_norm, h_norm = gdn_norm.astype(F32)[:, None, :], hgrn_norm.astype(F32)[:, None, :]
    conv_w = gdn_conv_w.astype(F32)
    lb_all = _lower_bounds(hgrn_lower_bounds)[:, None, :]

    ck = cache_fox_k.reshape(depth * n_pool, page * fh, HEAD_DIM)
    cv = cache_fox_v.reshape(depth * n_pool, page * fh, HEAD_DIM)
    clf = cache_fox_logf.astype(F32).reshape(depth * n_pool, SUBLANES, LANES)
    bias_page = jnp.tile(fox_f_bias.astype(F32), (1, LANES // fh))[:, None, :] * jnp.ones((1, SUBLANES, 1), F32)
    bias_col = jnp.tile(fox_f_bias.astype(F32), (1, ts))[:, :, None]

    zeros_state_p = jnp.zeros((bp, gh, HEAD_DIM, HEAD_DIM), F32)
    zeros_hstate_p = jnp.zeros((bp, hh, HEAD_DIM, HEAD_DIM), F32)
    zeros_conv_p = jnp.zeros((bp, CONV_W - 1, 3 * gw), F32)

    xp = x_prompt.reshape(bp * tp, d)
    xs = jnp.pad(x_sample, ((0, 0), (0, tsp - ts), (0, 0))).reshape(bs * tsp, d)

    tm_p = _tile(bp * tp, (1024, 512, 256, 128, 64))
    tm_s = _tile(bs * tsp, (512, 256, 128, 64))
    tq = _tile(tp, FOX_Q_TILES)

    def in_proj(xn, l, tm):
        proj = _in_proj(xn, w_in_t, row_starts, l, tm=tm, tn=IN_PROJ_COL_TILE)
        narrow = _in_proj_narrow(xn, w_in_t, src["ff"], fh, src["ga"], 2 * gh, l, tm=tm)
        return proj, narrow

    def dense_tail(x, oa, ob, oc, proj, l, tm):
        merged = _merge(oa, ob, oc, proj, lay, w_fo, w_go, w_ho, l, tm=min(tm, MERGE_ROW_TILE), tn=d)
        x, xn = _matmul_norm_res(merged, w_o, x, g_mix_post, l, g_mlp_pre, l, tm=min(tm, RESIDUAL_ROW_TILE),
                                 tk=_tile(d, RESIDUAL_K_TILES), name="out_proj_norm_res")
        u = _mlp_up(xn, w_u, l, tm=tm, tn=_tile(w_u.shape[-1], (1024, 512, 256)))
        return _matmul_norm_res(u, w_d, x, g_mlp_post, l, g_mix_pre, (l + 1) % depth,
                                tm=min(tm, RESIDUAL_ROW_TILE), tk=_tile(w_d.shape[1], RESIDUAL_K_TILES),
                                name="mlp_down_norm_res")

    xpn = _rmsnorm_cast(xp, g_mix_pre, 0, tm=tm_p)
    xsn = _rmsnorm_cast(xs, g_mix_pre, 0, tm=tm_s)
    outs = {k: [] for k in ("fkp", "fvp", "flp", "fks", "fvs", "fls", "gsp", "gss", "gcp", "gcs", "hsp", "hss")}
    for l in range(depth):
        proj, narrow = in_proj(xpn, l, tm_p)
        lf, c_tok, c_row = _fox_gate(narrow, lay, f_bias, l, bp, tp)
        oa = _fox_attn(proj, lay, c_tok, c_row, bp, tp, tq=tq)
        ob, gs, gc = _gdn(proj, narrow, lay, conv_w, a_log, dt_bias, g_norm, zeros_state_p, zeros_conv_p, l, bp, tp,
                          tp)
        oc, hs = _hgrn(proj, lay, lb_all, h_norm, zeros_hstate_p, l, bp, tp, tp)
        xp, xpn = dense_tail(xp, oa, ob, oc, proj, l, tm_p)
        outs["fkp"].append(proj[:, lay["fk"][0]:lay["fk"][0] + fw].reshape(bp, tp, fh, HEAD_DIM))
        outs["fvp"].append(proj[:, lay["fv"][0]:lay["fv"][0] + fw].reshape(bp, tp, fh, HEAD_DIM))
        outs["flp"].append(lf[:, :fh].reshape(bp, tp, fh))
        outs["gsp"].append(gs)
        outs["gcp"].append(gc)
        outs["hsp"].append(hs)

        proj, narrow = in_proj(xsn, l, tm_s)
        p3 = proj.reshape(bs, tsp, wp)[:, :ts]
        q_new, k_new, v_new = (p3[:, :, lay[k][0]:lay[k][0] + fw] for k in ("fq", "fk", "fv"))
        ff_new = narrow.reshape(bs, tsp, 2 * LANES)[:, :ts, lay["ff"][0]:lay["ff"][0] + fh]
        rows = ts * fh
        pad_rows = lambda z: jnp.pad(z.reshape(bs, rows, HEAD_DIM), ((0, 0), (0, LANES - rows), (0, 0)))
        ff_page = jnp.pad(ff_new.reshape(bs, 1, rows), ((0, 0), (0, SUBLANES - 1), (0, LANES - rows)))
        oa_s, lf_page = _fox_decode(
            q_new.reshape(bs, rows, HEAD_DIM), pad_rows(k_new), pad_rows(v_new), ff_page, bias_page[l],
            ff_new.reshape(bs, rows, 1), bias_col[l], ck, cv, clf,
            (page_table.astype(jnp.int32) + l * n_pool).reshape(-1), n_pages, heads=fh, n_new=ts)
        oa = jnp.pad(oa_s.reshape(bs, ts, fw), ((0, 0), (0, tsp - ts), (0, 0))).reshape(bs * tsp, fw)
        ob, gs, gc = _gdn(proj, narrow, lay, conv_w, a_log, dt_bias, g_norm, state_gdn[l].astype(F32),
                          state_gdn_conv[l].astype(F32), l, bs, tsp, ts)
        oc, hs = _hgrn(proj, lay, lb_all, h_norm, state_hgrn[l].astype(F32), l, bs, tsp, ts)
        xs, xsn = dense_tail(xs, oa, ob, oc, proj, l, tm_s)
        outs["fks"].append(k_new.reshape(bs, ts, fh, HEAD_DIM))
        outs["fvs"].append(v_new.reshape(bs, ts, fh, HEAD_DIM))
        outs["fls"].append(lf_page[:, 0, :rows].reshape(bs, ts, fh))
        outs["gss"].append(gs)
        outs["gcs"].append(gc)
        outs["hss"].append(hs)

    st = lambda k: jnp.stack(outs[k])
    return (xp.reshape(bp, tp, d), xs.reshape(bs, tsp, d)[:, :ts],
            st("fkp"), st("fvp"), st("flp"), st("fks"), st("fvs"), st("fls"),
            st("gsp"), st("gss"), st("gcp"), st("gcs"), st("hsp"), st("hss"))
```

```python
import functools

import jax
import jax.numpy as jnp
from jax import lax
from jax.experimental import pallas as pl
from jax.experimental.pallas import tpu as pltpu

F32, BF16 = jnp.float32, jnp.bfloat16
HEAD_DIM = 128
CONV_W = 4
CHUNK = 64
EPS = 1e-6
NEG_BIG = -1e30
LB_FLOOR = 1e-30
LANES = 128
SUBLANES = 8
V7X_VMEM_BYTES = 64 * 2**20
VMEM_RESERVE_BYTES = 6 * 2**20
KERNEL_TEMP_BYTES = 16 * 2**20
DECODE_PAGES_PER_STEP = 8
FOX_Q_TILES = (1024, 512, 256, 128)
IN_PROJ_COL_TILE = 1024
MERGE_ROW_TILE = 256
RESIDUAL_ROW_TILE = 512
RESIDUAL_K_TILES = (2048, 1024, 512, 256)


def _nbytes(shape, dtype):
    n = 1
    for s in shape:
        n *= s
    return n * jnp.dtype(dtype).itemsize


def _params(semantics, blocks, scratch=()):
    need = 2 * sum(_nbytes(s, d) for s, d in blocks) + sum(_nbytes(s, d) for s, d in scratch) + KERNEL_TEMP_BYTES
    return pltpu.CompilerParams(dimension_semantics=semantics,
                                vmem_limit_bytes=min(need, V7X_VMEM_BYTES - VMEM_RESERVE_BYTES))


def _dot(a, b):
    return jnp.dot(a, b, preferred_element_type=F32)


def _dot_nt(a, b):
    return lax.dot_general(a, b, (((1,), (1,)), ((), ())), preferred_element_type=F32)


def _split3(x):
    x1 = x.astype(BF16)
    r = x - x1.astype(F32)
    x2 = r.astype(BF16)
    x3 = (r - x2.astype(F32)).astype(BF16)
    return x1, x2, x3


def _dot01(m01, x):
    x1, x2, x3 = _split3(x)
    return _dot(m01, x1) + (_dot(m01, x2) + _dot(m01, x3))


def _log_sigmoid(x):
    return -(jnp.maximum(-x, 0.0) + jnp.log1p(jnp.exp(-jnp.abs(x))))


def _softplus(x):
    return jnp.maximum(x, 0.0) + jnp.log1p(jnp.exp(-jnp.abs(x)))


def _silu(x):
    return x * jax.nn.sigmoid(x)


def _tril01(n, strict=False):
    r = lax.broadcasted_iota(jnp.int32, (n, n), 0)
    c = lax.broadcasted_iota(jnp.int32, (n, n), 1)
    return (r > c) if strict else (r >= c)


def _lower_bounds_kernel(x_ref, o_ref):
    x = x_ref[...]
    e = jnp.exp(x - jnp.max(x, axis=0, keepdims=True))
    p = e / jnp.sum(e, axis=0, keepdims=True)
    run = jnp.zeros_like(p[0:1])
    for l in range(x.shape[0]):
        run = run + p[l:l + 1]
        o_ref[l:l + 1, :] = run - p[0:1]


def _lower_bounds(x):
    return pl.pallas_call(_lower_bounds_kernel, out_shape=jax.ShapeDtypeStruct(x.shape, F32),
                          name="hgrn_lower_bounds")(x.astype(F32))


def _rmsnorm_bf16(x, g):
    var = jnp.mean(x * x, axis=-1, keepdims=True)
    return ((x * lax.rsqrt(var + EPS)) * g).astype(BF16)


def _rmsnorm_cast_kernel(x_ref, g_ref, o_ref):
    o_ref[...] = _rmsnorm_bf16(x_ref[...], g_ref[...])


def _rmsnorm_cast(x, g, layer, *, tm):
    n, d = x.shape
    blocks = [((tm, d), F32), ((1, d), F32), ((tm, d), BF16)]
    return pl.pallas_call(
        _rmsnorm_cast_kernel,
        grid=(n // tm,),
        in_specs=[pl.BlockSpec((tm, d), lambda i: (i, 0)), pl.BlockSpec((None, 1, d), lambda i: (layer, 0, 0))],
        out_specs=pl.BlockSpec((tm, d), lambda i: (i, 0)),
        out_shape=jax.ShapeDtypeStruct((n, d), BF16),
        compiler_params=_params(("parallel",), blocks),
        name="rmsnorm_cast")(x, g)


def _in_proj_kernel(offs_ref, xn_ref, w_ref, o_ref, w16_ref):
    del offs_ref

    @pl.when(pl.program_id(1) == 0)
    def _():
        w16_ref[...] = w_ref[0].astype(BF16)

    o_ref[...] = _dot_nt(xn_ref[...], w16_ref[...])


def _in_proj(xn, w_t, row_starts, layer, *, tm, tn):
    n, d = xn.shape
    assert all(s % SUBLANES == 0 for s in row_starts)
    offs = jnp.asarray([s // SUBLANES for s in row_starts], jnp.int32)
    blocks = [((tm, d), BF16), ((tn, d), F32), ((tm, tn), F32)]
    scratch = [((tn, d), BF16)]
    grid_spec = pltpu.PrefetchScalarGridSpec(
        num_scalar_prefetch=1, grid=(len(row_starts), n // tm),
        in_specs=[pl.BlockSpec((tm, d), lambda j, i, offs: (i, 0)),
                  pl.BlockSpec((pl.Element(1), pl.Element(tn), pl.Element(d)),
                               lambda j, i, offs: (layer, offs[j] * SUBLANES, 0))],
        out_specs=pl.BlockSpec((tm, tn), lambda j, i, offs: (i, j)),
        scratch_shapes=[pltpu.VMEM(s, t) for s, t in scratch])
    return pl.pallas_call(
        _in_proj_kernel, grid_spec=grid_spec,
        out_shape=jax.ShapeDtypeStruct((n, len(row_starts) * tn), F32),
        compiler_params=_params(("arbitrary", "arbitrary"), blocks, scratch),
        name="in_proj")(offs, xn, w_t)


def _in_proj_narrow_kernel(xn_ref, a_ref, b_ref, o_ref, w16_ref):
    @pl.when(pl.program_id(0) == 0)
    def _():
        pad = lambda w: jnp.zeros((LANES - w.shape[0], w.shape[1]), F32)
        a, b = a_ref[0], b_ref[0]
        w16_ref[...] = jnp.concatenate([a, pad(a), b, pad(b)], axis=0).astype(BF16)

    o_ref[...] = _dot_nt(xn_ref[...], w16_ref[...])


def _in_proj_narrow(xn, w_t, a_start, a_rows, b_start, b_rows, layer, *, tm):
    n, d = xn.shape
    assert a_start % SUBLANES == 0 and b_start % SUBLANES == 0 and a_rows % SUBLANES == 0 and b_rows % SUBLANES == 0
    element = lambda rows, start: pl.BlockSpec((pl.Element(1), pl.Element(rows), pl.Element(d)),
                                               lambda i: (layer, start, 0))
    blocks = [((tm, d), BF16), ((a_rows, d), F32), ((b_rows, d), F32), ((tm, 2 * LANES), F32)]
    scratch = [((2 * LANES, d), BF16)]
    return pl.pallas_call(
        _in_proj_narrow_kernel,
        grid=(n // tm,),
        in_specs=[pl.BlockSpec((tm, d), lambda i: (i, 0)), element(a_rows, a_start), element(b_rows, b_start)],
        out_specs=pl.BlockSpec((tm, 2 * LANES), lambda i: (i, 0)),
        out_shape=jax.ShapeDtypeStruct((n, 2 * LANES), F32),
        scratch_shapes=[pltpu.VMEM(s, t) for s, t in scratch],
        compiler_params=_params(("arbitrary",), blocks, scratch),
        name="in_proj_narrow")(xn, w_t, w_t)


def _mlp_up_kernel(xn_ref, w_ref, o_ref, w16_ref):
    @pl.when(pl.program_id(1) == 0)
    def _():
        w16_ref[...] = w_ref[...].astype(BF16)

    y = jnp.maximum(_dot(xn_ref[...], w16_ref[...]), 0.0)
    o_ref[...] = (y * y).astype(o_ref.dtype)


def _mlp_up(xn, w, layer, *, tm, tn):
    n, d = xn.shape
    f = w.shape[-1]
    blocks = [((tm, d), BF16), ((d, tn), F32), ((tm, tn), BF16)]
    scratch = [((d, tn), BF16)]
    return pl.pallas_call(
        _mlp_up_kernel,
        grid=(f // tn, n // tm),
        in_specs=[pl.BlockSpec((tm, d), lambda j, i: (i, 0)),
                  pl.BlockSpec((None, d, tn), lambda j, i: (layer, 0, j))],
        out_specs=pl.BlockSpec((tm, tn), lambda j, i: (i, j)),
        out_shape=jax.ShapeDtypeStruct((n, f), BF16),
        scratch_shapes=[pltpu.VMEM(s, t) for s, t in scratch],
        compiler_params=_params(("arbitrary", "arbitrary"), blocks, scratch),
        name="mlp_up")(xn, w)


def _matmul_norm_res_kernel(a_ref, w_ref, x_ref, g_ref, gn_ref, o_ref, xn_ref, acc_ref):
    k = pl.program_id(1)

    @pl.when(k == 0)
    def _():
        acc_ref[...] = jnp.zeros_like(acc_ref)

    acc_ref[...] += _dot(a_ref[...], w_ref[...])

    @pl.when(k == pl.num_programs(1) - 1)
    def _():
        y = acc_ref[...]
        var = jnp.mean(y * y, axis=-1, keepdims=True)
        x_new = x_ref[...] + (y * lax.rsqrt(var + EPS)) * g_ref[...]
        o_ref[...] = x_new
        xn_ref[...] = _rmsnorm_bf16(x_new, gn_ref[...])


def _matmul_norm_res(a, w, x, g, layer, g_next, layer_next, *, tm, tk, name):
    n, kdim = a.shape
    d = x.shape[1]
    blocks = [((tm, tk), BF16), ((tk, d), BF16), ((tm, d), F32), ((1, d), F32), ((1, d), F32), ((tm, d), F32),
              ((tm, d), BF16)]
    scratch = [((tm, d), F32)]
    return pl.pallas_call(
        _matmul_norm_res_kernel,
        grid=(n // tm, kdim // tk),
        in_specs=[pl.BlockSpec((tm, tk), lambda i, k: (i, k)),
                  pl.BlockSpec((None, tk, d), lambda i, k: (layer, k, 0)),
                  pl.BlockSpec((tm, d), lambda i, k: (i, 0)),
                  pl.BlockSpec((None, 1, d), lambda i, k: (layer, 0, 0)),
                  pl.BlockSpec((None, 1, d), lambda i, k: (layer_next, 0, 0))],
        out_specs=[pl.BlockSpec((tm, d), lambda i, k: (i, 0)), pl.BlockSpec((tm, d), lambda i, k: (i, 0))],
        out_shape=[jax.ShapeDtypeStruct((n, d), F32), jax.ShapeDtypeStruct((n, d), BF16)],
        scratch_shapes=[pltpu.VMEM(s, t) for s, t in scratch],
        compiler_params=_params(("parallel", "arbitrary"), blocks, scratch),
        name=name)(a, w, x, g, g_next)


def _merge_kernel(oa_ref, ob_ref, oc_ref, ga_ref, gb_ref, gc_ref, wa_ref, wb_ref, wc_ref, o_ref):
    m = jax.nn.sigmoid(ga_ref[...]) * _dot(oa_ref[...], wa_ref[...])
    m = m + jax.nn.sigmoid(gb_ref[...]) * _dot(ob_ref[...], wb_ref[...])
    m = m + jax.nn.sigmoid(gc_ref[...]) * _dot(oc_ref[...], wc_ref[...])
    o_ref[...] = m.astype(o_ref.dtype)


def _merge(oa, ob, oc, proj, lay, wa, wb, wc, layer, *, tm, tn):
    n = oa.shape[0]
    d = wa.shape[-1]
    col = {k: lay[k][0] // tn for k in ("gate_a", "gate_b", "gate_c")}
    o_specs = [pl.BlockSpec((tm, o.shape[1]), lambda i, j: (i, 0)) for o in (oa, ob, oc)]
    g_specs = [pl.BlockSpec((tm, tn), functools.partial(lambda i, j, c: (i, c + j), c=col[k]))
               for k in ("gate_a", "gate_b", "gate_c")]
    w_specs = [pl.BlockSpec((None, w.shape[1], tn), lambda i, j: (layer, 0, j)) for w in (wa, wb, wc)]
    blocks = ([((tm, o.shape[1]), BF16) for o in (oa, ob, oc)] + [((tm, tn), F32)] * 3
              + [((w.shape[1], tn), BF16) for w in (wa, wb, wc)] + [((tm, tn), BF16)])
    return pl.pallas_call(
        _merge_kernel,
        grid=(n // tm, d // tn),
        in_specs=o_specs + g_specs + w_specs,
        out_specs=pl.BlockSpec((tm, tn), lambda i, j: (i, j)),
        out_shape=jax.ShapeDtypeStruct((n, d), BF16),
        compiler_params=_params(("parallel", "parallel"), blocks),
        name="gated_merge")(oa, ob, oc, proj, proj, proj, wa, wb, wc)


def _fox_gate_kernel(ff_ref, b_ref, lf_ref, c_ref, ct_ref):
    t = ff_ref.shape[0]
    lf_ref[...] = _log_sigmoid(ff_ref[...] + b_ref[...])
    tri = _tril01(LANES).astype(BF16)
    carry = jnp.zeros((1, LANES), F32)
    for blk in range(t // LANES):
        rows = slice(blk * LANES, (blk + 1) * LANES)
        cb = _dot01(tri, lf_ref[rows, :]) + carry
        c_ref[rows, :] = cb
        ct_ref[:, rows] = cb.T
        carry = cb[LANES - 1:LANES, :]


def _fox_gate(proj, lay, bias, layer, bx, tx):
    n = proj.shape[0]
    col = lay["ff"][0] // LANES
    blocks = [((tx, LANES), F32)] * 4 + [((LANES, tx), F32)]
    return pl.pallas_call(
        _fox_gate_kernel,
        grid=(bx,),
        in_specs=[pl.BlockSpec((tx, LANES), lambda b: (b, col)),
                  pl.BlockSpec((None, 1, LANES), lambda b: (layer, 0, 0))],
        out_specs=[pl.BlockSpec((tx, LANES), lambda b: (b, 0)),
                   pl.BlockSpec((tx, LANES), lambda b: (b, 0)),
                   pl.BlockSpec((None, LANES, tx), lambda b: (b, 0, 0))],
        out_shape=[jax.ShapeDtypeStruct((n, LANES), F32), jax.ShapeDtypeStruct((n, LANES), F32),
                   jax.ShapeDtypeStruct((bx, LANES, tx), F32)],
        compiler_params=_params(("parallel",), blocks),
        name="fox_gate")(proj, bias)


def _fox_attn_kernel(q_ref, k_ref, v_ref, ct_ref, cr_ref, o_ref, m_ref, l_ref, acc_ref, *, heads, scale):
    i, j = pl.program_id(1), pl.program_id(2)
    tq, tk = q_ref.shape[0], k_ref.shape[0]

    @pl.when(j == 0)
    def _():
        m_ref[...] = jnp.full_like(m_ref, NEG_BIG)
        l_ref[...] = jnp.zeros_like(l_ref)
        acc_ref[...] = jnp.zeros_like(acc_ref)

    @pl.when(j <= i)
    def _():
        q_pos = i * tq + lax.broadcasted_iota(jnp.int32, (tq, tk), 0)
        k_pos = j * tk + lax.broadcasted_iota(jnp.int32, (tq, tk), 1)
        causal = q_pos >= k_pos
        for h in range(heads):
            hs = slice(h * HEAD_DIM, (h + 1) * HEAD_DIM)
            s = _dot_nt(q_ref[:, hs].astype(BF16), k_ref[:, hs].astype(BF16)) * scale
            s = s + ct_ref[:, h:h + 1] - cr_ref[h:h + 1, :]
            s = jnp.where(causal, s, NEG_BIG)
            m_prev = m_ref[:, h:h + 1]
            m_new = jnp.maximum(m_prev, jnp.max(s, axis=-1, keepdims=True))
            alpha = jnp.exp(m_prev - m_new)
            p = jnp.exp(s - m_new)
            l_ref[:, h:h + 1] = alpha * l_ref[:, h:h + 1] + jnp.sum(p, axis=-1, keepdims=True)
            acc_ref[:, hs] = alpha * acc_ref[:, hs] + _dot(p.astype(BF16), v_ref[:, hs].astype(BF16))
            m_ref[:, h:h + 1] = m_new

    @pl.when(j == i)
    def _():
        for h in range(heads):
            hs = slice(h * HEAD_DIM, (h + 1) * HEAD_DIM)
            o_ref[:, hs] = (acc_ref[:, hs] / l_ref[:, h:h + 1]).astype(o_ref.dtype)


def _fox_attn(proj, lay, c_tok, c_row, bx, tx, *, tq):
    n = proj.shape[0]
    fw = lay["fq"][1]
    heads = fw // HEAD_DIM
    nq = tx // tq
    qc, kc, vc = (lay[k][0] // fw for k in ("fq", "fk", "fv"))
    blocks = [((tq, fw), F32)] * 3 + [((tq, LANES), F32), ((LANES, tq), F32), ((tq, fw), BF16)]
    scratch = [((tq, LANES), F32), ((tq, LANES), F32), ((tq, fw), F32)]
    return pl.pallas_call(
        functools.partial(_fox_attn_kernel, heads=heads, scale=HEAD_DIM ** -0.5),
        grid=(bx, nq, nq),
        in_specs=[pl.BlockSpec((tq, fw), lambda b, i, j: (b * nq + i, qc)),
                  pl.BlockSpec((tq, fw), lambda b, i, j: (b * nq + jnp.minimum(j, i), kc)),
                  pl.BlockSpec((tq, fw), lambda b, i, j: (b * nq + jnp.minimum(j, i), vc)),
                  pl.BlockSpec((tq, LANES), lambda b, i, j: (b * nq + i, 0)),
                  pl.BlockSpec((None, LANES, tq), lambda b, i, j: (b, 0, jnp.minimum(j, i)))],
        out_specs=pl.BlockSpec((tq, fw), lambda b, i, j: (b * nq + i, 0)),
        out_shape=jax.ShapeDtypeStruct((n, fw), BF16),
        scratch_shapes=[pltpu.VMEM(s, t) for s, t in scratch],
        compiler_params=_params(("parallel", "parallel", "arbitrary"), blocks, scratch),
        name="fox_attn_prompt")(proj, proj, proj, c_tok, c_row)


def _page_suffix(x, lane, sub, heads):
    y = x
    sh = heads
    while sh < LANES:
        y = y + jnp.where(lane < LANES - sh, pltpu.roll(y, LANES - sh, axis=1), 0.0)
        sh *= 2
    t = jnp.where(lane < heads, y, 0.0)
    sh = heads
    while sh < LANES:
        t = t + pltpu.roll(t, sh, axis=1)
        sh *= 2
    z = t
    sh = 1
    while sh < SUBLANES:
        z = z + jnp.where(sub < SUBLANES - sh, pltpu.roll(z, SUBLANES - sh, axis=0), 0.0)
        sh *= 2
    return (y - x) + (z - t), z[0:1, :]


def _fox_decode_kernel(pt_ref, q_ref, kn_ref, vn_ref, ffp_ref, bp_ref, ffc_ref, bc_ref, *rest,
                       pages, heads, n_new, scale):
    del pt_ref
    k_refs, v_refs, lf_refs = rest[:pages], rest[pages:2 * pages], rest[2 * pages:3 * pages]
    o_ref, lfo_ref, m_ref, l_ref, acc_ref, carry_ref, rn_ref = rest[3 * pages:]
    step = pl.program_id(1)
    rows = q_ref.shape[0]
    lane = lax.broadcasted_iota(jnp.int32, (SUBLANES, LANES), 1)
    sub = lax.broadcasted_iota(jnp.int32, (SUBLANES, LANES), 0)
    r_i = lax.broadcasted_iota(jnp.int32, (rows, LANES), 0)
    c_i = lax.broadcasted_iota(jnp.int32, (rows, LANES), 1)
    head_ok = (r_i % heads) == (c_i % heads)
    qb = q_ref[...].astype(BF16)

    def update(scores, values):
        m_prev = m_ref[...]
        m_new = m_prev
        for s in scores:
            m_new = jnp.maximum(m_new, jnp.max(s, axis=-1, keepdims=True))
        alpha = jnp.exp(m_prev - m_new)
        l_new = alpha * l_ref[...]
        acc = alpha * acc_ref[...]
        for s, v in zip(scores, values):
            p = jnp.exp(s - m_new)
            l_new = l_new + jnp.sum(p, axis=-1, keepdims=True)
            acc = acc + _dot(p.astype(BF16), v)
        m_ref[...] = m_new
        l_ref[...] = l_new
        acc_ref[...] = acc

    @pl.when(step == 0)
    def _():
        m_ref[...] = jnp.full_like(m_ref, NEG_BIG)
        l_ref[...] = jnp.zeros_like(l_ref)
        acc_ref[...] = jnp.zeros_like(acc_ref)
        new_ok = (sub == 0) & (lane < n_new * heads)
        lf = jnp.where(new_ok, _log_sigmoid(ffp_ref[...] + bp_ref[...]), 0.0)
        lfo_ref[...] = lf
        gate, total = _page_suffix(lf, lane, sub, heads)
        carry_ref[...] = total
        lfc = _log_sigmoid(ffc_ref[...] + bc_ref[...])
        run = jnp.zeros((heads, 1), F32)
        for t in range(n_new - 1, -1, -1):
            rn_ref[t * heads:(t + 1) * heads, :] = run
            run = run + lfc[t * heads:(t + 1) * heads, :]
        s = _dot_nt(qb, kn_ref[...].astype(BF16)) * scale + gate[0:1, :] - rn_ref[...]
        ok = head_ok & (c_i // heads <= r_i // heads) & (c_i < n_new * heads)
        update([jnp.where(ok, s, NEG_BIG)], [vn_ref[...].astype(BF16)])

    carry = carry_ref[...]
    rn = rn_ref[...]
    scores = []
    for r in range(pages):
        gate, total = _page_suffix(lf_refs[r][...], lane, sub, heads)
        gate = gate + carry
        carry = carry + total
        s_page = _dot_nt(qb, k_refs[r][...].astype(BF16)) * scale
        scores.append([jnp.where(head_ok, s_page[:, c * LANES:(c + 1) * LANES] + gate[c:c + 1, :] - rn, NEG_BIG)
                       for c in range(SUBLANES)])
    carry_ref[...] = carry
    m_prev = m_ref[...]
    m_new = m_prev
    for page_scores in scores:
        for s in page_scores:
            m_new = jnp.maximum(m_new, jnp.max(s, axis=-1, keepdims=True))
    alpha = jnp.exp(m_prev - m_new)
    l_new = alpha * l_ref[...]
    acc = alpha * acc_ref[...]
    for r in range(pages):
        probs = [jnp.exp(s - m_new) for s in scores[r]]
        for p in probs:
            l_new = l_new + jnp.sum(p, axis=-1, keepdims=True)
        acc = acc + _dot(jnp.concatenate(probs, axis=-1).astype(BF16), v_refs[r][...].astype(BF16))
    m_ref[...] = m_new
    l_ref[...] = l_new
    acc_ref[...] = acc

    @pl.when(step == pl.num_programs(1) - 1)
    def _():
        o_ref[...] = (acc_ref[...] / l_ref[...]).astype(o_ref.dtype)


def _fox_decode(q, k_new, v_new, ff_page, bias_page, ff_col, bias_col, cache_k, cache_v, cache_lf, pages_flat,
                n_pages, *, heads, n_new):
    bs, rows, _ = q.shape
    page_rows = cache_k.shape[1]
    assert page_rows == SUBLANES * LANES and heads * (LANES // heads) == LANES and n_pages % DECODE_PAGES_PER_STEP == 0
    pages = DECODE_PAGES_PER_STEP
    n_steps = n_pages // pages

    def page_map(r):
        return lambda b, s, pt: (pt[b * n_pages + (n_pages - 1 - (s * pages + r))], 0, 0)

    in_specs = [pl.BlockSpec((None, rows, HEAD_DIM), lambda b, s, pt: (b, 0, 0)),
                pl.BlockSpec((None, LANES, HEAD_DIM), lambda b, s, pt: (b, 0, 0)),
                pl.BlockSpec((None, LANES, HEAD_DIM), lambda b, s, pt: (b, 0, 0)),
                pl.BlockSpec((None, SUBLANES, LANES), lambda b, s, pt: (b, 0, 0)),
                pl.BlockSpec((SUBLANES, LANES), lambda b, s, pt: (0, 0)),
                pl.BlockSpec((None, rows, 1), lambda b, s, pt: (b, 0, 0)),
                pl.BlockSpec((rows, 1), lambda b, s, pt: (0, 0))]
    in_specs += [pl.BlockSpec((None, page_rows, HEAD_DIM), page_map(r)) for r in range(pages)] * 2
    in_specs += [pl.BlockSpec((None, SUBLANES, LANES), page_map(r)) for r in range(pages)]
    blocks = [((page_rows, HEAD_DIM), F32)] * (2 * pages) + [((LANES, HEAD_DIM), F32)] * 4
    scratch = [((rows, 1), F32), ((rows, 1), F32), ((rows, HEAD_DIM), F32), ((1, LANES), F32), ((rows, 1), F32)]
    grid_spec = pltpu.PrefetchScalarGridSpec(
        num_scalar_prefetch=1, grid=(bs, n_steps), in_specs=in_specs,
        out_specs=[pl.BlockSpec((None, rows, HEAD_DIM), lambda b, s, pt: (b, 0, 0)),
                   pl.BlockSpec((None, SUBLANES, LANES), lambda b, s, pt: (b, 0, 0))],
        scratch_shapes=[pltpu.VMEM(s, t) for s, t in scratch])
    return pl.pallas_call(
        functools.partial(_fox_decode_kernel, pages=pages, heads=heads, n_new=n_new, scale=HEAD_DIM ** -0.5),
        grid_spec=grid_spec,
        out_shape=[jax.ShapeDtypeStruct((bs, rows, HEAD_DIM), BF16),
                   jax.ShapeDtypeStruct((bs, SUBLANES, LANES), F32)],
        compiler_params=_params(("parallel", "arbitrary"), blocks, scratch),
        name="fox_decode")(pages_flat, q, k_new, v_new, ff_page, bias_page, ff_col, bias_col,
                           *([cache_k] * pages), *([cache_v] * pages), *([cache_lf] * pages))


def _hi_lo(x):
    hi = x.astype(BF16)
    return hi, (x - hi.astype(F32)).astype(BF16)


def _hi_lo_lhs(a):
    hi = a.astype(BF16).astype(F32)
    a16 = jnp.concatenate([hi, a - hi], axis=1).astype(BF16)
    return jnp.concatenate([a16, a16], axis=1)


def _hi_lo_rhs(b):
    hi, lo = _hi_lo(b)
    return jnp.concatenate([hi, hi, lo, lo], axis=0)


def _unit_lower_solves(ms, rhss):
    c = ms[0].shape[0]
    assert 2 * c == LANES
    xs, ps, power = list(rhss), list(ms), 1
    while True:
        lhs = [_hi_lo_lhs(p) for p in ps]
        upd = [_dot(a, _hi_lo_rhs(x)) for a, x in zip(lhs, xs)]
        xs = [x - u for x, u in zip(xs, upd)] if power == 1 else [x + u for x, u in zip(xs, upd)]
        if 2 * power >= c:
            return xs
        ps = [_dot(a, _hi_lo_rhs(p)) for a, p in zip(lhs, ps)]
        power *= 2


def _gdn_prep_kernel(qkv_ref, ab_ref, cw_ref, al_ref, dt_ref, cb0_ref,
                     u_ref, w_ref, qe_ref, kdt_ref, attn_ref, egl_ref, cb_out_ref, xe_ref, *, heads, t_valid):
    c = pl.program_id(1)
    ch = qkv_ref.shape[0]
    width = heads * HEAD_DIM
    tail = CONV_W - 1
    base = SUBLANES
    last_chunk = (t_valid - 1) // ch
    last_len = t_valid - last_chunk * ch

    @pl.when(c == 0)
    def _():
        xe_ref[base - tail:base, :] = cb0_ref[...]

    xe_ref[base:base + ch, :] = qkv_ref[...]
    y = xe_ref[base - tail:base - tail + ch, :] * cw_ref[0:1, :]
    for i in range(1, CONV_W):
        y = y + xe_ref[base - tail + i:base - tail + i + ch, :] * cw_ref[i:i + 1, :]
    y = _silu(y)

    @pl.when(c == last_chunk)
    def _():
        cb_out_ref[...] = xe_ref[base + last_len - tail:base + last_len, :]

    xe_ref[base - tail:base, :] = xe_ref[base + ch - tail:base + ch, :]

    valid = (c * ch + lax.broadcasted_iota(jnp.int32, (ch, 1), 0)) < t_valid
    ab = ab_ref[...]
    g_all = jnp.where(valid, -jnp.exp(al_ref[...]) * _softplus(ab + dt_ref[...]), 0.0)
    beta_all = jnp.where(valid, jax.nn.sigmoid(ab), 0.0)
    cum = _dot01(_tril01(ch).astype(BF16), g_all)
    cum_t = cum.T
    lower = _tril01(ch)
    strict = _tril01(ch, strict=True)
    egl_ref[...] = jnp.exp(cum[ch - 1:ch, :])

    hr = range(heads)
    sl = lambda part, h: slice(part * width + h * HEAD_DIM, part * width + (h + 1) * HEAD_DIM)
    gc = [cum[:, h:h + 1] for h in hr]
    beta = [beta_all[:, heads + h:heads + h + 1] for h in hr]
    decay = [jnp.exp(jnp.where(lower, gc[h] - cum_t[h:h + 1, :], NEG_BIG)) for h in hr]
    q = [y[:, sl(0, h)] for h in hr]
    k = [y[:, sl(1, h)] for h in hr]
    q = [x * lax.rsqrt(jnp.sum(x * x, axis=-1, keepdims=True) + EPS) * (HEAD_DIM ** -0.5) for x in q]
    k = [x * lax.rsqrt(jnp.sum(x * x, axis=-1, keepdims=True) + EPS) for x in k]
    kb = [k[h] * beta[h] for h in hr]
    k16 = [x.astype(BF16) for x in k]
    eg = [jnp.exp(gc[h]) for h in hr]
    sols = _unit_lower_solves(
        [jnp.where(strict, _dot_nt(kb[h].astype(BF16), k16[h]) * decay[h], 0.0) for h in hr],
        [jnp.concatenate([y[:, sl(2, h)] * beta[h], kb[h] * eg[h]], axis=-1) for h in hr])
    for h in hr:
        sol = sols[h]
        u_ref[:, sl(0, h)] = sol[:, :HEAD_DIM]
        w_ref[:, sl(0, h)] = sol[:, HEAD_DIM:].astype(BF16)
        qe_ref[:, sl(0, h)] = (q[h] * eg[h]).astype(BF16)
        attn_ref[h] = (_dot_nt(q[h].astype(BF16), k16[h]) * decay[h]).astype(BF16)
        kdt_ref[h] = (k[h] * jnp.exp(cum[ch - 1:ch, h:h + 1] - gc[h])).T.astype(BF16)


def _gdn_scan_kernel(u_ref, w_ref, qe_ref, kdt_ref, attn_ref, egl_ref, gz_ref, gn_ref, s0_ref,
                     o_ref, s_out_ref, st_ref, *, heads):
    c = pl.program_id(0)
    bx = u_ref.shape[0]

    @pl.when(c == 0)
    def _():
        st_ref[...] = s0_ref[...]

    chains = [(b, h) for b in range(bx) for h in range(heads)]
    hs = lambda h: slice(h * HEAD_DIM, (h + 1) * HEAD_DIM)
    s_old = [st_ref[b, h] for b, h in chains]
    s16 = [s.astype(BF16) for s in s_old]
    ws = [_dot(w_ref[b, :, hs(h)], s) for (b, h), s in zip(chains, s16)]
    qs = [_dot(qe_ref[b, :, hs(h)], s) for (b, h), s in zip(chains, s16)]
    vn16 = [(u_ref[b, :, hs(h)] - x).astype(BF16) for (b, h), x in zip(chains, ws)]
    for i, (b, h) in enumerate(chains):
        o = qs[i] + _dot(attn_ref[b, h], vn16[i])
        st_ref[b, h] = s_old[i] * egl_ref[b, :, h:h + 1] + _dot(kdt_ref[b, h], vn16[i])
        on = (o * lax.rsqrt(jnp.mean(o * o, axis=-1, keepdims=True) + EPS)) * gn_ref[...]
        o_ref[b, :, hs(h)] = (on * _silu(gz_ref[b, :, hs(h)])).astype(o_ref.dtype)

    @pl.when(c == pl.num_programs(0) - 1)
    def _():
        s_out_ref[...] = st_ref[...]


def _gdn(proj, narrow, lay, conv_w, a_log, dt_bias, gnorm, s0, cb0, layer, bx, tx, t_valid):
    n, wp = proj.shape
    gw = lay["gz"][1]
    heads = gw // HEAD_DIM
    ch = CHUNK
    nc = tx // ch
    qkv_c, ab_c, gz_c = lay["gqkv"][0] // (3 * gw), lay["gab"][0] // LANES, lay["gz"][0] // gw
    blocks = [((ch, 3 * gw), F32), ((ch, LANES), F32), ((CONV_W, 3 * gw), F32), ((CONV_W - 1, 3 * gw), F32),
              ((ch, gw), F32), ((ch, gw), BF16), ((ch, gw), BF16), ((heads, HEAD_DIM, LANES), BF16),
              ((heads, ch, LANES), BF16), ((CONV_W - 1, 3 * gw), F32)]
    scratch = [((SUBLANES + ch, 3 * gw), F32)]
    u, w16, qe16, kdt, attn, egl, cb_out = pl.pallas_call(
        functools.partial(_gdn_prep_kernel, heads=heads, t_valid=t_valid),
        grid=(bx, nc),
        in_specs=[pl.BlockSpec((ch, 3 * gw), lambda b, c: (b * nc + c, qkv_c)),
                  pl.BlockSpec((ch, LANES), lambda b, c: (b * nc + c, ab_c)),
                  pl.BlockSpec((None, CONV_W, 3 * gw), lambda b, c: (layer, 0, 0)),
                  pl.BlockSpec((None, 1, LANES), lambda b, c: (layer, 0, 0)),
                  pl.BlockSpec((None, 1, LANES), lambda b, c: (layer, 0, 0)),
                  pl.BlockSpec((None, CONV_W - 1, 3 * gw), lambda b, c: (b, 0, 0))],
        out_specs=[pl.BlockSpec((ch, gw), lambda b, c: (b * nc + c, 0)),
                   pl.BlockSpec((ch, gw), lambda b, c: (b * nc + c, 0)),
                   pl.BlockSpec((ch, gw), lambda b, c: (b * nc + c, 0)),
                   pl.BlockSpec((None, None, heads, HEAD_DIM, ch), lambda b, c: (b, c, 0, 0, 0)),
                   pl.BlockSpec((None, None, heads, ch, ch), lambda b, c: (b, c, 0, 0, 0)),
                   pl.BlockSpec((None, None, 1, LANES), lambda b, c: (b, c, 0, 0)),
                   pl.BlockSpec((None, CONV_W - 1, 3 * gw), lambda b, c: (b, 0, 0))],
        out_shape=[jax.ShapeDtypeStruct((n, gw), F32), jax.ShapeDtypeStruct((n, gw), BF16),
                   jax.ShapeDtypeStruct((n, gw), BF16),
                   jax.ShapeDtypeStruct((bx, nc, heads, HEAD_DIM, ch), BF16),
                   jax.ShapeDtypeStruct((bx, nc, heads, ch, ch), BF16),
                   jax.ShapeDtypeStruct((bx, nc, 1, LANES), F32),
                   jax.ShapeDtypeStruct((bx, CONV_W - 1, 3 * gw), F32)],
        scratch_shapes=[pltpu.VMEM(s, t) for s, t in scratch],
        compiler_params=_params(("parallel", "arbitrary"), blocks, scratch),
        name="gdn_prep")(proj, narrow, conv_w, a_log, dt_bias, cb0)

    state = (bx, heads, HEAD_DIM, HEAD_DIM)
    blocks = [((bx, ch, gw), F32), ((bx, ch, gw), BF16), ((bx, ch, gw), BF16), ((bx, heads, HEAD_DIM, LANES), BF16),
              ((bx, heads, ch, LANES), BF16), ((bx, ch, gw), F32), (state, F32), ((bx, ch, gw), BF16), (state, F32)]
    scratch = [(state, F32)]
    rows3 = lambda a: a.reshape(bx, tx, a.shape[-1])
    o, s_out = pl.pallas_call(
        functools.partial(_gdn_scan_kernel, heads=heads),
        grid=(nc,),
        in_specs=[pl.BlockSpec((bx, ch, gw), lambda c: (0, c, 0)),
                  pl.BlockSpec((bx, ch, gw), lambda c: (0, c, 0)),
                  pl.BlockSpec((bx, ch, gw), lambda c: (0, c, 0)),
                  pl.BlockSpec((bx, None, heads, HEAD_DIM, ch), lambda c: (0, c, 0, 0, 0)),
                  pl.BlockSpec((bx, None, heads, ch, ch), lambda c: (0, c, 0, 0, 0)),
                  pl.BlockSpec((bx, None, 1, LANES), lambda c: (0, c, 0, 0)),
                  pl.BlockSpec((bx, ch, gw), lambda c: (0, c, gz_c)),
                  pl.BlockSpec((None, 1, HEAD_DIM), lambda c: (layer, 0, 0)),
                  pl.BlockSpec(state, lambda c: (0, 0, 0, 0))],
        out_specs=[pl.BlockSpec((bx, ch, gw), lambda c: (0, c, 0)),
                   pl.BlockSpec(state, lambda c: (0, 0, 0, 0))],
        out_shape=[jax.ShapeDtypeStruct((bx, tx, gw), BF16), jax.ShapeDtypeStruct(state, F32)],
        scratch_shapes=[pltpu.VMEM(s, t) for s, t in scratch],
        compiler_params=_params(("arbitrary",), blocks, scratch),
        name="gdn_scan")(rows3(u), rows3(w16), rows3(qe16), kdt, attn, egl, rows3(proj), gnorm, s0)
    return o.reshape(n, gw), s_out, cb_out


def _hgrn_kernel(hq_ref, hf_ref, hi_ref, hg_ref, lb_ref, hn_ref, s0_ref, o_ref, s_out_ref, st_ref,
                 *, heads, t_valid):
    c = pl.program_id(1)
    ch = hq_ref.shape[0]
    nblk = ch // SUBLANES

    @pl.when(c == 0)
    def _():
        for h in range(heads):
            st_ref[h] = s0_ref[h].T

    tok = lax.broadcasted_iota(jnp.int32, (ch, 1), 0)
    valid = (c * ch + tok) < t_valid
    lb = lb_ref[...]
    hf = hf_ref[...]
    a = jnp.log(jnp.maximum(lb, LB_FLOOR))
    b = jnp.log1p(-lb) + _log_sigmoid(hf)
    log_f = jnp.maximum(a, b) + jnp.log1p(jnp.exp(-jnp.abs(a - b)))
    log_f = jnp.where(valid, log_f, 0.0)
    k_all = jnp.where(valid, (1.0 - lb) * jax.nn.sigmoid(-hf), 0.0)
    cum_all = _dot01(_tril01(ch).astype(BF16), log_f)
    q_all = _silu(hq_ref[...])

    ti = lax.broadcasted_iota(jnp.int32, (ch, ch), 0)
    tj = lax.broadcasted_iota(jnp.int32, (ch, ch), 1)
    sub3 = lax.broadcasted_iota(jnp.int32, (nblk, SUBLANES, HEAD_DIM), 1)

    hr = range(heads)
    hsl = [slice(h * HEAD_DIM, (h + 1) * HEAD_DIM) for h in hr]
    g = [cum_all[:, s] for s in hsl]
    q = [q_all[:, s] for s in hsl]
    k = [k_all[:, s] for s in hsl]
    v16 = [hi_ref[:, s].astype(BF16) for s in hsl]
    attn_t = [jnp.zeros((ch, ch), F32) for _ in hr]
    half = ch // 2
    while half >= SUBLANES:
        blk = 2 * half
        upper = (tok % blk) >= half
        same = (ti // blk) == (tj // blk)
        es = []
        for h in hr:
            g3 = g[h].reshape(ch // blk, blk, HEAD_DIM)
            es.append(jnp.exp(-jnp.abs(g3 - g3[:, half - 1:half, :])).reshape(ch, HEAD_DIM))
        prods = [_dot_nt(jnp.where(upper, 0.0, k[h] * es[h]).astype(BF16),
                         jnp.where(upper, q[h] * es[h], 0.0).astype(BF16)) for h in hr]
        attn_t = [a + jnp.where(same, p, 0.0) for a, p in zip(attn_t, prods)]
        half //= 2
    diag = (ti // SUBLANES) == (tj // SUBLANES)
    for h in hr:
        g3, q3, k3 = (z.reshape(nblk, SUBLANES, HEAD_DIM) for z in (g[h], q[h], k[h]))
        cols = []
        for i in range(SUBLANES):
            e = jnp.exp(jnp.where(sub3 <= i, g3[:, i:i + 1, :] - g3, NEG_BIG))
            w = jnp.sum(e * k3 * q3[:, i:i + 1, :], axis=-1, keepdims=True).reshape(ch, 1)
            cols.append(jnp.where((tj % SUBLANES) == i, w, 0.0))
        while len(cols) > 1:
            cols = [a + b for a, b in zip(cols[0::2], cols[1::2])]
        attn_t[h] = attn_t[h] + jnp.where(diag, cols[0], 0.0)
    st = [st_ref[h] for h in hr]
    o = [_dot_nt((q[h] * jnp.exp(g[h])).astype(BF16), st[h].astype(BF16)) + _dot(attn_t[h].T.astype(BF16), v16[h])
         for h in hr]
    for h in hr:
        gl = g[h][ch - 1:ch, :]
        kd = k[h] * jnp.exp(gl - g[h])
        st_ref[h] = st[h] * jnp.exp(gl) + _dot(hi_ref[:, hsl[h]].T.astype(BF16), kd.astype(BF16))
        on = (o[h] * lax.rsqrt(jnp.mean(o[h] * o[h], axis=-1, keepdims=True) + EPS)) * hn_ref[...]
        o_ref[:, hsl[h]] = (on * _silu(hg_ref[:, hsl[h]])).astype(o_ref.dtype)

    @pl.when(c == pl.num_programs(1) - 1)
    def _():
        for h in range(heads):
            s_out_ref[h] = st_ref[h].T


def _hgrn(proj, lay, lb, hnorm, s0, layer, bx, tx, t_valid):
    n = proj.shape[0]
    hw = lay["hq"][1]
    heads = hw // HEAD_DIM
    ch = CHUNK
    nc = tx // ch
    cols = [lay[k][0] // hw for k in ("hq", "hf", "hi", "hg")]
    blocks = [((ch, hw), F32)] * 4 + [((heads, HEAD_DIM, HEAD_DIM), F32)] * 2 + [((ch, hw), BF16)]
    scratch = [((heads, HEAD_DIM, HEAD_DIM), F32)]
    in_specs = [pl.BlockSpec((ch, hw), functools.partial(lambda b, c, col: (b * nc + c, col), col=col))
                for col in cols]
    in_specs += [pl.BlockSpec((None, 1, hw), lambda b, c: (layer, 0, 0)),
                 pl.BlockSpec((None, 1, HEAD_DIM), lambda b, c: (layer, 0, 0)),
                 pl.BlockSpec((None, heads, HEAD_DIM, HEAD_DIM), lambda b, c: (b, 0, 0, 0))]
    return pl.pallas_call(
        functools.partial(_hgrn_kernel, heads=heads, t_valid=t_valid),
        grid=(bx, nc),
        in_specs=in_specs,
        out_specs=[pl.BlockSpec((ch, hw), lambda b, c: (b * nc + c, 0)),
                   pl.BlockSpec((None, heads, HEAD_DIM, HEAD_DIM), lambda b, c: (b, 0, 0, 0))],
        out_shape=[jax.ShapeDtypeStruct((n, hw), BF16),
                   jax.ShapeDtypeStruct((bx, heads, HEAD_DIM, HEAD_DIM), F32)],
        scratch_shapes=[pltpu.VMEM(s, t) for s, t in scratch],
        compiler_params=_params(("parallel", "arbitrary"), blocks, scratch),
        name="hgrn2")(proj, proj, proj, proj, lb, hnorm, s0)


def _layout(d_model, fw, gw, hw):
    order = [("gate_a", d_model), ("gate_b", d_model), ("gate_c", d_model), ("fq", fw), ("fk", fw), ("fv", fw),
             ("gqkv", 3 * gw), ("gz", gw), ("hq", hw), ("hf", hw), ("hi", hw), ("hg", hw)]
    lay, start = {}, 0
    for name, width in order:
        lay[name] = (start, width)
        start += width
    lay["total"] = (0, start)
    lay["ff"], lay["gab"] = (0, LANES), (LANES, LANES)
    return lay


def _source_columns(d_model, fh, gh, fw, gw, hw):
    splits = [("fq", fw), ("fk", fw), ("fv", fw), ("ff", fh), ("gq", gw), ("gk", gw), ("gv", gw), ("ga", gh),
              ("gb", gh), ("gz", gw), ("hq", hw), ("hf", hw), ("hi", hw), ("hg", hw),
              ("gate_a", d_model), ("gate_b", d_model), ("gate_c", d_model)]
    src, start = {}, 0
    for name, width in splits:
        src[name] = start
        start += width
    src["end"] = start
    return src


def _in_proj_row_starts(lay, src, tn):
    starts = []
    for first, dst, width in (("gate_a", "gate_a", 3 * lay["gate_a"][1]), ("fq", "fq", 3 * lay["fq"][1]),
                              ("gq", "gqkv", lay["gqkv"][1]), ("gz", "gz", lay["total"][1] - lay["gz"][0])):
        assert lay[dst][0] == len(starts) * tn and width % tn == 0
        starts += [src[first] + k * tn for k in range(width // tn)]
    assert len(starts) * tn == lay["total"][1]
    return starts


def _pad_lanes(x, fill=0.0):
    return jnp.pad(x.astype(F32), ((0, 0), (0, LANES - x.shape[-1])), constant_values=fill)[:, None, :]


def _tile(n, candidates):
    for c in candidates:
        if n % c == 0:
            return c
    raise ValueError(f"no tile for {n} among {candidates}")


def kernel(x_prompt, x_sample, cache_fox_k, cache_fox_v, cache_fox_logf, state_gdn, state_gdn_conv, state_hgrn,
           page_table, norm_mix_pre, norm_mix_post, norm_mlp_pre, norm_mlp_post, w_in, fox_f_bias, gdn_conv_w,
           gdn_a_log, gdn_dt_bias, gdn_norm, hgrn_lower_bounds, hgrn_norm, w_fox_o, w_gdn_o, w_hgrn_o, w_out,
           w_up, w_down):
    bp, tp, d = x_prompt.shape
    bs, ts, _ = x_sample.shape
    depth = w_in.shape[0]
    fh, gh = fox_f_bias.shape[1], gdn_a_log.shape[1]
    fw, gw, hw = fh * HEAD_DIM, gh * HEAD_DIM, hgrn_lower_bounds.shape[1]
    hh = hw // HEAD_DIM
    n_pool, page = cache_fox_k.shape[1], cache_fox_k.shape[2]
    n_pages = page_table.shape[1]
    tsp = CHUNK
    assert tp % CHUNK == 0 and tp % LANES == 0 and ts <= tsp and ts >= CONV_W - 1
    assert page * fh == SUBLANES * LANES and 2 * gh <= LANES and fh <= LANES
    lay = _layout(d, fw, gw, hw)
    wp = lay["total"][1]
    src = _source_columns(d, fh, gh, fw, gw, hw)
    assert src["end"] == w_in.shape[-1] and src["gb"] == src["ga"] + gh
    row_starts = _in_proj_row_starts(lay, src, IN_PROJ_COL_TILE)

    w_in_t = jnp.swapaxes(w_in.astype(F32), 1, 2)
    w_u = w_up.astype(F32)
    w_fo, w_go, w_ho, w_o, w_d = (w.astype(BF16) for w in (w_fox_o, w_gdn_o, w_hgrn_o, w_out, w_down))
    g_mix_pre, g_mix_post, g_mlp_pre, g_mlp_post = (
        g.astype(F32)[:, None, :] for g in (norm_mix_pre, norm_mix_post, norm_mlp_pre, norm_mlp_post))
    f_bias = _pad_lanes(fox_f_bias)
    a_log, dt_bias = _pad_lanes(gdn_a_log), _pad_lanes(gdn_dt_bias)
    g_norm, h_norm = gdn_norm.astype(F32)[:, None, :], hgrn_norm.astype(F32)[:, None, :]
    conv_w = gdn_conv_w.astype(F32)
    lb_all = _lower_bounds(hgrn_lower_bounds)[:, None, :]

    ck = cache_fox_k.reshape(depth * n_pool, page * fh, HEAD_DIM)
    cv = cache_fox_v.reshape(depth * n_pool, page * fh, HEAD_DIM)
    clf = cache_fox_logf.astype(F32).reshape(depth * n_pool, SUBLANES, LANES)
    bias_page = jnp.tile(fox_f_bias.astype(F32), (1, LANES // fh))[:, None, :] * jnp.ones((1, SUBLANES, 1), F32)
    bias_col = jnp.tile(fox_f_bias.astype(F32), (1, ts))[:, :, None]

    zeros_state_p = jnp.zeros((bp, gh, HEAD_DIM, HEAD_DIM), F32)
    zeros_hstate_p = jnp.zeros((bp, hh, HEAD_DIM, HEAD_DIM), F32)
    zeros_conv_p = jnp.zeros((bp, CONV_W - 1, 3 * gw), F32)

    xp = x_prompt.reshape(bp * tp, d)
    xs = jnp.pad(x_sample, ((0, 0), (0, tsp - ts), (0, 0))).reshape(bs * tsp, d)

    tm_p = _tile(bp * tp, (1024, 512, 256, 128, 64))
    tm_s = _tile(bs * tsp, (512, 256, 128, 64))
    tq = _tile(tp, FOX_Q_TILES)

    def in_proj(xn, l, tm):
        proj = _in_proj(xn, w_in_t, row_starts, l, tm=tm, tn=IN_PROJ_COL_TILE)
        narrow = _in_proj_narrow(xn, w_in_t, src["ff"], fh, src["ga"], 2 * gh, l, tm=tm)
        return proj, narrow

    def dense_tail(x, oa, ob, oc, proj, l, tm):
        merged = _merge(oa, ob, oc, proj, lay, w_fo, w_go, w_ho, l, tm=min(tm, MERGE_ROW_TILE), tn=d)
        x, xn = _matmul_norm_res(merged, w_o, x, g_mix_post, l, g_mlp_pre, l, tm=min(tm, RESIDUAL_ROW_TILE),
                                 tk=_tile(d, RESIDUAL_K_TILES), name="out_proj_norm_res")
        u = _mlp_up(xn, w_u, l, tm=tm, tn=_tile(w_u.shape[-1], (1024, 512, 256)))
        return _matmul_norm_res(u, w_d, x, g_mlp_post, l, g_mix_pre, (l + 1) % depth,
                                tm=min(tm, RESIDUAL_ROW_TILE), tk=_tile(w_d.shape[1], RESIDUAL_K_TILES),
                                name="mlp_down_norm_res")

    xpn = _rmsnorm_cast(xp, g_mix_pre, 0, tm=tm_p)
    xsn = _rmsnorm_cast(xs, g_mix_pre, 0, tm=tm_s)
    outs = {k: [] for k in ("fkp", "fvp", "flp", "fks", "fvs", "fls", "gsp", "gss", "gcp", "gcs", "hsp", "hss")}
    for l in range(depth):
        proj, narrow = in_proj(xpn, l, tm_p)
        lf, c_tok, c_row = _fox_gate(narrow, lay, f_bias, l, bp, tp)
        oa = _fox_attn(proj, lay, c_tok, c_row, bp, tp, tq=tq)
        ob, gs, gc = _gdn(proj, narrow, lay, conv_w, a_log, dt_bias, g_norm, zeros_state_p, zeros_conv_p, l, bp, tp,
                          tp)
        oc, hs = _hgrn(proj, lay, lb_all, h_norm, zeros_hstate_p, l, bp, tp, tp)
        xp, xpn = dense_tail(xp, oa, ob, oc, proj, l, tm_p)
        outs["fkp"].append(proj[:, lay["fk"][0]:lay["fk"][0] + fw].reshape(bp, tp, fh, HEAD_DIM))
        outs["fvp"].append(proj[:, lay["fv"][0]:lay["fv"][0] + fw].reshape(bp, tp, fh, HEAD_DIM))
        outs["flp"].append(lf[:, :fh].reshape(bp, tp, fh))
        outs["gsp"].append(gs)
        outs["gcp"].append(gc)
        outs["hsp"].append(hs)

        proj, narrow = in_proj(xsn, l, tm_s)
        p3 = proj.reshape(bs, tsp, wp)[:, :ts]
        q_new, k_new, v_new = (p3[:, :, lay[k][0]:lay[k][0] + fw] for k in ("fq", "fk", "fv"))
        ff_new = narrow.reshape(bs, tsp, 2 * LANES)[:, :ts, lay["ff"][0]:lay["ff"][0] + fh]
        rows = ts * fh
        pad_rows = lambda z: jnp.pad(z.reshape(bs, rows, HEAD_DIM), ((0, 0), (0, LANES - rows), (0, 0)))
        ff_page = jnp.pad(ff_new.reshape(bs, 1, rows), ((0, 0), (0, SUBLANES - 1), (0, LANES - rows)))
        oa_s, lf_page = _fox_decode(
            q_new.reshape(bs, rows, HEAD_DIM), pad_rows(k_new), pad_rows(v_new), ff_page, bias_page[l],
            ff_new.reshape(bs, rows, 1), bias_col[l], ck, cv, clf,
            (page_table.astype(jnp.int32) + l * n_pool).reshape(-1), n_pages, heads=fh, n_new=ts)
        oa = jnp.pad(oa_s.reshape(bs, ts, fw), ((0, 0), (0, tsp - ts), (0, 0))).reshape(bs * tsp, fw)
        ob, gs, gc = _gdn(proj, narrow, lay, conv_w, a_log, dt_bias, g_norm, state_gdn[l].astype(F32),
                          state_gdn_conv[l].astype(F32), l, bs, tsp, ts)
        oc, hs = _hgrn(proj, lay, lb_all, h_norm, state_hgrn[l].astype(F32), l, bs, tsp, ts)
        xs, xsn = dense_tail(xs, oa, ob, oc, proj, l, tm_s)
        outs["fks"].append(k_new.reshape(bs, ts, fh, HEAD_DIM))
        outs["fvs"].append(v_new.reshape(bs, ts, fh, HEAD_DIM))
        outs["fls"].append(lf_page[:, 0, :rows].reshape(bs, ts, fh))
        outs["gss"].append(gs)
        outs["gcs"].append(gc)
        outs["hss"].append(hs)

    st = lambda k: jnp.stack(outs[k])
    return (xp.reshape(bp, tp, d), xs.reshape(bs, tsp, d)[:, :ts],
            st("fkp"), st("fvp"), st("flp"), st("fks"), st("fvs"), st("fls"),
            st("gsp"), st("gss"), st("gcp"), st("gcs"), st("hsp"), st("hss"))
```

```python
import functools

import jax
import jax.numpy as jnp
from jax import lax
from jax.experimental import pallas as pl
from jax.experimental.pallas import tpu as pltpu

F32, BF16 = jnp.float32, jnp.bfloat16
HEAD_DIM = 128
CONV_W = 4
CHUNK = 64
EPS = 1e-6
NEG_BIG = -1e30
LB_FLOOR = 1e-30
LANES = 128
SUBLANES = 8
V7X_VMEM_BYTES = 64 * 2**20
VMEM_RESERVE_BYTES = 6 * 2**20
KERNEL_TEMP_BYTES = 16 * 2**20
DECODE_PAGES_PER_STEP = 16
FOX_Q_TILES = (1024, 512, 256, 128)
IN_PROJ_COL_TILE = 1024
MERGE_ROW_TILE = 256
RESIDUAL_ROW_TILE = 512
RESIDUAL_K_TILES = (2048, 1024, 512, 256)
RESIDUAL_EPILOGUE_GROUPS = 4


def _nbytes(shape, dtype):
    n = 1
    for s in shape:
        n *= s
    return n * jnp.dtype(dtype).itemsize


def _params(semantics, blocks, scratch=()):
    need = 2 * sum(_nbytes(s, d) for s, d in blocks) + sum(_nbytes(s, d) for s, d in scratch) + KERNEL_TEMP_BYTES
    return pltpu.CompilerParams(dimension_semantics=semantics,
                                vmem_limit_bytes=min(need, V7X_VMEM_BYTES - VMEM_RESERVE_BYTES))


def _dot(a, b):
    return jnp.dot(a, b, preferred_element_type=F32)


def _dot_nt(a, b):
    return lax.dot_general(a, b, (((1,), (1,)), ((), ())), preferred_element_type=F32)


def _split3(x):
    x1 = x.astype(BF16)
    r = x - x1.astype(F32)
    x2 = r.astype(BF16)
    x3 = (r - x2.astype(F32)).astype(BF16)
    return x1, x2, x3


def _dot01(m01, x):
    x1, x2, x3 = _split3(x)
    return _dot(m01, x1) + (_dot(m01, x2) + _dot(m01, x3))


def _log_sigmoid(x):
    return -(jnp.maximum(-x, 0.0) + jnp.log1p(jnp.exp(-jnp.abs(x))))


def _softplus(x):
    return jnp.maximum(x, 0.0) + jnp.log1p(jnp.exp(-jnp.abs(x)))


def _silu(x):
    return x * jax.nn.sigmoid(x)


def _tril01(n, strict=False):
    r = lax.broadcasted_iota(jnp.int32, (n, n), 0)
    c = lax.broadcasted_iota(jnp.int32, (n, n), 1)
    return (r > c) if strict else (r >= c)


def _lower_bounds_kernel(x_ref, o_ref):
    x = x_ref[...]
    e = jnp.exp(x - jnp.max(x, axis=0, keepdims=True))
    p = e / jnp.sum(e, axis=0, keepdims=True)
    run = jnp.zeros_like(p[0:1])
    for l in range(x.shape[0]):
        run = run + p[l:l + 1]
        o_ref[l:l + 1, :] = run - p[0:1]


def _lower_bounds(x):
    return pl.pallas_call(_lower_bounds_kernel, out_shape=jax.ShapeDtypeStruct(x.shape, F32),
                          name="hgrn_lower_bounds")(x.astype(F32))


def _rmsnorm_bf16(x, g):
    var = jnp.mean(x * x, axis=-1, keepdims=True)
    return ((x * lax.rsqrt(var + EPS)) * g).astype(BF16)


def _rmsnorm_cast_kernel(x_ref, g_ref, o_ref):
    o_ref[...] = _rmsnorm_bf16(x_ref[...], g_ref[...])


def _rmsnorm_cast(x, g, layer, *, tm):
    n, d = x.shape
    blocks = [((tm, d), F32), ((1, d), F32), ((tm, d), BF16)]
    return pl.pallas_call(
        _rmsnorm_cast_kernel,
        grid=(n // tm,),
        in_specs=[pl.BlockSpec((tm, d), lambda i: (i, 0)), pl.BlockSpec((None, 1, d), lambda i: (layer, 0, 0))],
        out_specs=pl.BlockSpec((tm, d), lambda i: (i, 0)),
        out_shape=jax.ShapeDtypeStruct((n, d), BF16),
        compiler_params=_params(("parallel",), blocks),
        name="rmsnorm_cast")(x, g)


def _in_proj_kernel(offs_ref, xn_ref, w_ref, o_ref, w16_ref):
    del offs_ref

    @pl.when(pl.program_id(1) == 0)
    def _():
        w16_ref[...] = w_ref[0].astype(BF16)

    o_ref[...] = _dot_nt(xn_ref[...], w16_ref[...])


def _in_proj(xn, w_t, row_starts, layer, *, tm, tn):
    n, d = xn.shape
    assert all(s % SUBLANES == 0 for s in row_starts)
    offs = jnp.asarray([s // SUBLANES for s in row_starts], jnp.int32)
    blocks = [((tm, d), BF16), ((tn, d), F32), ((tm, tn), F32)]
    scratch = [((tn, d), BF16)]
    grid_spec = pltpu.PrefetchScalarGridSpec(
        num_scalar_prefetch=1, grid=(len(row_starts), n // tm),
        in_specs=[pl.BlockSpec((tm, d), lambda j, i, offs: (i, 0)),
                  pl.BlockSpec((pl.Element(1), pl.Element(tn), pl.Element(d)),
                               lambda j, i, offs: (layer, offs[j] * SUBLANES, 0))],
        out_specs=pl.BlockSpec((tm, tn), lambda j, i, offs: (i, j)),
        scratch_shapes=[pltpu.VMEM(s, t) for s, t in scratch])
    return pl.pallas_call(
        _in_proj_kernel, grid_spec=grid_spec,
        out_shape=jax.ShapeDtypeStruct((n, len(row_starts) * tn), F32),
        compiler_params=_params(("arbitrary", "arbitrary"), blocks, scratch),
        name="in_proj")(offs, xn, w_t)


def _in_proj_narrow_kernel(xn_ref, a_ref, b_ref, o_ref, w16_ref):
    @pl.when(pl.program_id(0) == 0)
    def _():
        pad = lambda w: jnp.zeros((LANES - w.shape[0], w.shape[1]), F32)
        a, b = a_ref[0], b_ref[0]
        w16_ref[...] = jnp.concatenate([a, pad(a), b, pad(b)], axis=0).astype(BF16)

    o_ref[...] = _dot_nt(xn_ref[...], w16_ref[...])


def _in_proj_narrow(xn, w_t, a_start, a_rows, b_start, b_rows, layer, *, tm):
    n, d = xn.shape
    assert a_start % SUBLANES == 0 and b_start % SUBLANES == 0 and a_rows % SUBLANES == 0 and b_rows % SUBLANES == 0
    element = lambda rows, start: pl.BlockSpec((pl.Element(1), pl.Element(rows), pl.Element(d)),
                                               lambda i: (layer, start, 0))
    blocks = [((tm, d), BF16), ((a_rows, d), F32), ((b_rows, d), F32), ((tm, 2 * LANES), F32)]
    scratch = [((2 * LANES, d), BF16)]
    return pl.pallas_call(
        _in_proj_narrow_kernel,
        grid=(n // tm,),
        in_specs=[pl.BlockSpec((tm, d), lambda i: (i, 0)), element(a_rows, a_start), element(b_rows, b_start)],
        out_specs=pl.BlockSpec((tm, 2 * LANES), lambda i: (i, 0)),
        out_shape=jax.ShapeDtypeStruct((n, 2 * LANES), F32),
        scratch_shapes=[pltpu.VMEM(s, t) for s, t in scratch],
        compiler_params=_params(("arbitrary",), blocks, scratch),
        name="in_proj_narrow")(xn, w_t, w_t)


def _mlp_up_kernel(xn_ref, w_ref, o_ref, w16_ref):
    @pl.when(pl.program_id(1) == 0)
    def _():
        w16_ref[...] = w_ref[...].astype(BF16)

    y = jnp.maximum(_dot(xn_ref[...], w16_ref[...]), 0.0)
    o_ref[...] = (y * y).astype(o_ref.dtype)


def _mlp_up(xn, w, layer, *, tm, tn):
    n, d = xn.shape
    f = w.shape[-1]
    blocks = [((tm, d), BF16), ((d, tn), F32), ((tm, tn), BF16)]
    scratch = [((d, tn), BF16)]
    return pl.pallas_call(
        _mlp_up_kernel,
        grid=(f // tn, n // tm),
        in_specs=[pl.BlockSpec((tm, d), lambda j, i: (i, 0)),
                  pl.BlockSpec((None, d, tn), lambda j, i: (layer, 0, j))],
        out_specs=pl.BlockSpec((tm, tn), lambda j, i: (i, j)),
        out_shape=jax.ShapeDtypeStruct((n, f), BF16),
        scratch_shapes=[pltpu.VMEM(s, t) for s, t in scratch],
        compiler_params=_params(("arbitrary", "arbitrary"), blocks, scratch),
        name="mlp_up")(xn, w)


def _matmul_norm_res_kernel(a_ref, w_ref, x_ref, g_ref, gn_ref, o_ref, xn_ref, acc_ref):
    k = pl.program_id(1)
    last = pl.num_programs(1) - 1
    tm = a_ref.shape[0]

    @pl.when(k == 0)
    def _():
        acc_ref[...] = jnp.zeros_like(acc_ref)

    @pl.when(k < last)
    def _():
        acc_ref[...] += _dot(a_ref[...], w_ref[...])

    @pl.when(k == last)
    def _():
        for r in range(RESIDUAL_EPILOGUE_GROUPS):
            rows = slice(r * (tm // RESIDUAL_EPILOGUE_GROUPS), (r + 1) * (tm // RESIDUAL_EPILOGUE_GROUPS))
            y = acc_ref[rows, :] + _dot(a_ref[rows, :], w_ref[...])
            var = jnp.mean(y * y, axis=-1, keepdims=True)
            x_new = x_ref[rows, :] + (y * lax.rsqrt(var + EPS)) * g_ref[...]
            o_ref[rows, :] = x_new
            xn_ref[rows, :] = _rmsnorm_bf16(x_new, gn_ref[...])


def _matmul_norm_res(a, w, x, g, layer, g_next, layer_next, *, tm, tk, name):
    n, kdim = a.shape
    d = x.shape[1]
    blocks = [((tm, tk), BF16), ((tk, d), BF16), ((tm, d), F32), ((1, d), F32), ((1, d), F32), ((tm, d), F32),
              ((tm, d), BF16)]
    scratch = [((tm, d), F32)]
    return pl.pallas_call(
        _matmul_norm_res_kernel,
        grid=(n // tm, kdim // tk),
        in_specs=[pl.BlockSpec((tm, tk), lambda i, k: (i, k)),
                  pl.BlockSpec((None, tk, d), lambda i, k: (layer, k, 0)),
                  pl.BlockSpec((tm, d), lambda i, k: (i, 0)),
                  pl.BlockSpec((None, 1, d), lambda i, k: (layer, 0, 0)),
                  pl.BlockSpec((None, 1, d), lambda i, k: (layer_next, 0, 0))],
        out_specs=[pl.BlockSpec((tm, d), lambda i, k: (i, 0)), pl.BlockSpec((tm, d), lambda i, k: (i, 0))],
        out_shape=[jax.ShapeDtypeStruct((n, d), F32), jax.ShapeDtypeStruct((n, d), BF16)],
        scratch_shapes=[pltpu.VMEM(s, t) for s, t in scratch],
        compiler_params=_params(("parallel", "arbitrary"), blocks, scratch),
        name=name)(a, w, x, g, g_next)


def _merge_kernel(oa_ref, ob_ref, oc_ref, ga_ref, gb_ref, gc_ref, wa_ref, wb_ref, wc_ref, o_ref):
    m = jax.nn.sigmoid(ga_ref[...]) * _dot(oa_ref[...], wa_ref[...])
    m = m + jax.nn.sigmoid(gb_ref[...]) * _dot(ob_ref[...], wb_ref[...])
    m = m + jax.nn.sigmoid(gc_ref[...]) * _dot(oc_ref[...], wc_ref[...])
    o_ref[...] = m.astype(o_ref.dtype)


def _merge(oa, ob, oc, proj, lay, wa, wb, wc, layer, *, tm, tn):
    n = oa.shape[0]
    d = wa.shape[-1]
    col = {k: lay[k][0] // tn for k in ("gate_a", "gate_b", "gate_c")}
    o_specs = [pl.BlockSpec((tm, o.shape[1]), lambda i, j: (i, 0)) for o in (oa, ob, oc)]
    g_specs = [pl.BlockSpec((tm, tn), functools.partial(lambda i, j, c: (i, c + j), c=col[k]))
               for k in ("gate_a", "gate_b", "gate_c")]
    w_specs = [pl.BlockSpec((None, w.shape[1], tn), lambda i, j: (layer, 0, j)) for w in (wa, wb, wc)]
    blocks = ([((tm, o.shape[1]), BF16) for o in (oa, ob, oc)] + [((tm, tn), F32)] * 3
              + [((w.shape[1], tn), BF16) for w in (wa, wb, wc)] + [((tm, tn), BF16)])
    return pl.pallas_call(
        _merge_kernel,
        grid=(n // tm, d // tn),
        in_specs=o_specs + g_specs + w_specs,
        out_specs=pl.BlockSpec((tm, tn), lambda i, j: (i, j)),
        out_shape=jax.ShapeDtypeStruct((n, d), BF16),
        compiler_params=_params(("parallel", "parallel"), blocks),
        name="gated_merge")(oa, ob, oc, proj, proj, proj, wa, wb, wc)


def _fox_gate_kernel(ff_ref, b_ref, lf_ref, c_ref, ct_ref):
    t = ff_ref.shape[0]
    lf_ref[...] = _log_sigmoid(ff_ref[...] + b_ref[...])
    tri = _tril01(LANES).astype(BF16)
    carry = jnp.zeros((1, LANES), F32)
    for blk in range(t // LANES):
        rows = slice(blk * LANES, (blk + 1) * LANES)
        cb = _dot01(tri, lf_ref[rows, :]) + carry
        c_ref[rows, :] = cb
        ct_ref[:, rows] = cb.T
        carry = cb[LANES - 1:LANES, :]


def _fox_gate(proj, lay, bias, layer, bx, tx):
    n = proj.shape[0]
    col = lay["ff"][0] // LANES
    blocks = [((tx, LANES), F32)] * 4 + [((LANES, tx), F32)]
    return pl.pallas_call(
        _fox_gate_kernel,
        grid=(bx,),
        in_specs=[pl.BlockSpec((tx, LANES), lambda b: (b, col)),
                  pl.BlockSpec((None, 1, LANES), lambda b: (layer, 0, 0))],
        out_specs=[pl.BlockSpec((tx, LANES), lambda b: (b, 0)),
                   pl.BlockSpec((tx, LANES), lambda b: (b, 0)),
                   pl.BlockSpec((None, LANES, tx), lambda b: (b, 0, 0))],
        out_shape=[jax.ShapeDtypeStruct((n, LANES), F32), jax.ShapeDtypeStruct((n, LANES), F32),
                   jax.ShapeDtypeStruct((bx, LANES, tx), F32)],
        compiler_params=_params(("parallel",), blocks),
        name="fox_gate")(proj, bias)


def _fox_attn_kernel(q_ref, k_ref, v_ref, ct_ref, cr_ref, o_ref, m_ref, l_ref, acc_ref, *, heads, scale):
    i, j = pl.program_id(1), pl.program_id(2)
    tq, tk = q_ref.shape[0], k_ref.shape[0]

    @pl.when(j == 0)
    def _():
        m_ref[...] = jnp.full_like(m_ref, NEG_BIG)
        l_ref[...] = jnp.zeros_like(l_ref)
        acc_ref[...] = jnp.zeros_like(acc_ref)

    @pl.when(j <= i)
    def _():
        q_pos = i * tq + lax.broadcasted_iota(jnp.int32, (tq, tk), 0)
        k_pos = j * tk + lax.broadcasted_iota(jnp.int32, (tq, tk), 1)
        causal = q_pos >= k_pos
        for h in range(heads):
            hs = slice(h * HEAD_DIM, (h + 1) * HEAD_DIM)
            s = _dot_nt(q_ref[:, hs].astype(BF16), k_ref[:, hs].astype(BF16)) * scale
            s = s + ct_ref[:, h:h + 1] - cr_ref[h:h + 1, :]
            s = jnp.where(causal, s, NEG_BIG)
            m_prev = m_ref[:, h:h + 1]
            m_new = jnp.maximum(m_prev, jnp.max(s, axis=-1, keepdims=True))
            alpha = jnp.exp(m_prev - m_new)
            p = jnp.exp(s - m_new)
            l_ref[:, h:h + 1] = alpha * l_ref[:, h:h + 1] + jnp.sum(p, axis=-1, keepdims=True)
            acc_ref[:, hs] = alpha * acc_ref[:, hs] + _dot(p.astype(BF16), v_ref[:, hs].astype(BF16))
            m_ref[:, h:h + 1] = m_new

    @pl.when(j == i)
    def _():
        for h in range(heads):
            hs = slice(h * HEAD_DIM, (h + 1) * HEAD_DIM)
            o_ref[:, hs] = (acc_ref[:, hs] / l_ref[:, h:h + 1]).astype(o_ref.dtype)


def _fox_attn(proj, lay, c_tok, c_row, bx, tx, *, tq):
    n = proj.shape[0]
    fw = lay["fq"][1]
    heads = fw // HEAD_DIM
    nq = tx // tq
    qc, kc, vc = (lay[k][0] // fw for k in ("fq", "fk", "fv"))
    blocks = [((tq, fw), F32)] * 3 + [((tq, LANES), F32), ((LANES, tq), F32), ((tq, fw), BF16)]
    scratch = [((tq, LANES), F32), ((tq, LANES), F32), ((tq, fw), F32)]
    return pl.pallas_call(
        functools.partial(_fox_attn_kernel, heads=heads, scale=HEAD_DIM ** -0.5),
        grid=(bx, nq, nq),
        in_specs=[pl.BlockSpec((tq, fw), lambda b, i, j: (b * nq + i, qc)),
                  pl.BlockSpec((tq, fw), lambda b, i, j: (b * nq + jnp.minimum(j, i), kc)),
                  pl.BlockSpec((tq, fw), lambda b, i, j: (b * nq + jnp.minimum(j, i), vc)),
                  pl.BlockSpec((tq, LANES), lambda b, i, j: (b * nq + i, 0)),
                  pl.BlockSpec((None, LANES, tq), lambda b, i, j: (b, 0, jnp.minimum(j, i)))],
        out_specs=pl.BlockSpec((tq, fw), lambda b, i, j: (b * nq + i, 0)),
        out_shape=jax.ShapeDtypeStruct((n, fw), BF16),
        scratch_shapes=[pltpu.VMEM(s, t) for s, t in scratch],
        compiler_params=_params(("parallel", "parallel", "arbitrary"), blocks, scratch),
        name="fox_attn_prompt")(proj, proj, proj, c_tok, c_row)


def _page_suffix(x, lane, sub, heads):
    y = x
    sh = heads
    while sh < LANES:
        y = y + jnp.where(lane < LANES - sh, pltpu.roll(y, LANES - sh, axis=1), 0.0)
        sh *= 2
    t = jnp.where(lane < heads, y, 0.0)
    sh = heads
    while sh < LANES:
        t = t + pltpu.roll(t, sh, axis=1)
        sh *= 2
    z = t
    sh = 1
    while sh < SUBLANES:
        z = z + jnp.where(sub < SUBLANES - sh, pltpu.roll(z, SUBLANES - sh, axis=0), 0.0)
        sh *= 2
    return (y - x) + (z - t), z[0:1, :]


def _fox_decode_kernel(pt_ref, q_ref, kn_ref, vn_ref, ffp_ref, bp_ref, ffc_ref, bc_ref, *rest,
                       pages, heads, n_new, scale):
    del pt_ref
    k_refs, v_refs, lf_refs = rest[:pages], rest[pages:2 * pages], rest[2 * pages:3 * pages]
    o_ref, lfo_ref, m_ref, l_ref, acc_ref, carry_ref, rn_ref = rest[3 * pages:]
    step = pl.program_id(1)
    rows = q_ref.shape[0]
    lane = lax.broadcasted_iota(jnp.int32, (SUBLANES, LANES), 1)
    sub = lax.broadcasted_iota(jnp.int32, (SUBLANES, LANES), 0)
    r_i = lax.broadcasted_iota(jnp.int32, (rows, LANES), 0)
    c_i = lax.broadcasted_iota(jnp.int32, (rows, LANES), 1)
    head_ok = (r_i % heads) == (c_i % heads)
    qb = q_ref[...].astype(BF16)

    def update(scores, values):
        m_prev = m_ref[...]
        m_new = m_prev
        for s in scores:
            m_new = jnp.maximum(m_new, jnp.max(s, axis=-1, keepdims=True))
        alpha = jnp.exp(m_prev - m_new)
        l_new = alpha * l_ref[...]
        acc = alpha * acc_ref[...]
        for s, v in zip(scores, values):
            p = jnp.exp(s - m_new)
            l_new = l_new + jnp.sum(p, axis=-1, keepdims=True)
            acc = acc + _dot(p.astype(BF16), v)
        m_ref[...] = m_new
        l_ref[...] = l_new
        acc_ref[...] = acc

    @pl.when(step == 0)
    def _():
        m_ref[...] = jnp.full_like(m_ref, NEG_BIG)
        l_ref[...] = jnp.zeros_like(l_ref)
        acc_ref[...] = jnp.zeros_like(acc_ref)
        new_ok = (sub == 0) & (lane < n_new * heads)
        lf = jnp.where(new_ok, _log_sigmoid(ffp_ref[...] + bp_ref[...]), 0.0)
        lfo_ref[...] = lf
        gate, total = _page_suffix(lf, lane, sub, heads)
        carry_ref[...] = total
        lfc = _log_sigmoid(ffc_ref[...] + bc_ref[...])
        run = jnp.zeros((heads, 1), F32)
        for t in range(n_new - 1, -1, -1):
            rn_ref[t * heads:(t + 1) * heads, :] = run
            run = run + lfc[t * heads:(t + 1) * heads, :]
        s = _dot_nt(qb, kn_ref[...].astype(BF16)) * scale + gate[0:1, :] - rn_ref[...]
        ok = head_ok & (c_i // heads <= r_i // heads) & (c_i < n_new * heads)
        update([jnp.where(ok, s, NEG_BIG)], [vn_ref[...].astype(BF16)])

    carry = carry_ref[...]
    rn = rn_ref[...]
    scores = []
    for r in range(pages):
        gate, total = _page_suffix(lf_refs[r][...], lane, sub, heads)
        gate = gate + carry
        carry = carry + total
        s_page = _dot_nt(qb, k_refs[r][...].astype(BF16)) * scale
        scores.append([jnp.where(head_ok, s_page[:, c * LANES:(c + 1) * LANES] + gate[c:c + 1, :] - rn, NEG_BIG)
                       for c in range(SUBLANES)])
    carry_ref[...] = carry
    m_prev = m_ref[...]
    m_new = m_prev
    for page_scores in scores:
        for s in page_scores:
            m_new = jnp.maximum(m_new, jnp.max(s, axis=-1, keepdims=True))
    alpha = jnp.exp(m_prev - m_new)
    l_new = alpha * l_ref[...]
    acc = alpha * acc_ref[...]
    for r in range(pages):
        probs = [jnp.exp(s - m_new) for s in scores[r]]
        for p in probs:
            l_new = l_new + jnp.sum(p, axis=-1, keepdims=True)
        acc = acc + _dot(jnp.concatenate(probs, axis=-1).astype(BF16), v_refs[r][...].astype(BF16))
    m_ref[...] = m_new
    l_ref[...] = l_new
    acc_ref[...] = acc

    @pl.when(step == pl.num_programs(1) - 1)
    def _():
        o_ref[...] = (acc_ref[...] / l_ref[...]).astype(o_ref.dtype)


def _fox_decode(q, k_new, v_new, ff_page, bias_page, ff_col, bias_col, cache_k, cache_v, cache_lf, pages_flat,
                n_pages, *, heads, n_new):
    bs, rows, _ = q.shape
    page_rows = cache_k.shape[1]
    assert page_rows == SUBLANES * LANES and heads * (LANES // heads) == LANES and n_pages % DECODE_PAGES_PER_STEP == 0
    pages = DECODE_PAGES_PER_STEP
    n_steps = n_pages // pages

    def page_map(r):
        return lambda b, s, pt: (pt[b * n_pages + (n_pages - 1 - (s * pages + r))], 0, 0)

    in_specs = [pl.BlockSpec((None, rows, HEAD_DIM), lambda b, s, pt: (b, 0, 0)),
                pl.BlockSpec((None, LANES, HEAD_DIM), lambda b, s, pt: (b, 0, 0)),
                pl.BlockSpec((None, LANES, HEAD_DIM), lambda b, s, pt: (b, 0, 0)),
                pl.BlockSpec((None, SUBLANES, LANES), lambda b, s, pt: (b, 0, 0)),
                pl.BlockSpec((SUBLANES, LANES), lambda b, s, pt: (0, 0)),
                pl.BlockSpec((None, rows, 1), lambda b, s, pt: (b, 0, 0)),
                pl.BlockSpec((rows, 1), lambda b, s, pt: (0, 0))]
    in_specs += [pl.BlockSpec((None, page_rows, HEAD_DIM), page_map(r)) for r in range(pages)] * 2
    in_specs += [pl.BlockSpec((None, SUBLANES, LANES), page_map(r)) for r in range(pages)]
    blocks = [((page_rows, HEAD_DIM), F32)] * (2 * pages) + [((LANES, HEAD_DIM), F32)] * 4
    scratch = [((rows, 1), F32), ((rows, 1), F32), ((rows, HEAD_DIM), F32), ((1, LANES), F32), ((rows, 1), F32)]
    grid_spec = pltpu.PrefetchScalarGridSpec(
        num_scalar_prefetch=1, grid=(bs, n_steps), in_specs=in_specs,
        out_specs=[pl.BlockSpec((None, rows, HEAD_DIM), lambda b, s, pt: (b, 0, 0)),
                   pl.BlockSpec((None, SUBLANES, LANES), lambda b, s, pt: (b, 0, 0))],
        scratch_shapes=[pltpu.VMEM(s, t) for s, t in scratch])
    return pl.pallas_call(
        functools.partial(_fox_decode_kernel, pages=pages, heads=heads, n_new=n_new, scale=HEAD_DIM ** -0.5),
        grid_spec=grid_spec,
        out_shape=[jax.ShapeDtypeStruct((bs, rows, HEAD_DIM), BF16),
                   jax.ShapeDtypeStruct((bs, SUBLANES, LANES), F32)],
        compiler_params=_params(("parallel", "arbitrary"), blocks, scratch),
        name="fox_decode")(pages_flat, q, k_new, v_new, ff_page, bias_page, ff_col, bias_col,
                           *([cache_k] * pages), *([cache_v] * pages), *([cache_lf] * pages))


def _hi_lo(x):
    hi = x.astype(BF16)
    return hi, (x - hi.astype(F32)).astype(BF16)


def _hi_lo_lhs(a):
    hi = a.astype(BF16).astype(F32)
    a16 = jnp.concatenate([hi, a - hi], axis=1).astype(BF16)
    return jnp.concatenate([a16, a16], axis=1)


def _hi_lo_rhs(b):
    hi, lo = _hi_lo(b)
    return jnp.concatenate([hi, hi, lo, lo], axis=0)


def _unit_lower_solves(ms, rhss):
    c = ms[0].shape[0]
    assert 2 * c == LANES
    xs, ps, power = list(rhss), list(ms), 1
    while True:
        lhs = [_hi_lo_lhs(p) for p in ps]
        upd = [_dot(a, _hi_lo_rhs(x)) for a, x in zip(lhs, xs)]
        xs = [x - u for x, u in zip(xs, upd)] if power == 1 else [x + u for x, u in zip(xs, upd)]
        if 2 * power >= c:
            return xs
        ps = [_dot(a, _hi_lo_rhs(p)) for a, p in zip(lhs, ps)]
        power *= 2


def _gdn_prep_kernel(qkv_ref, ab_ref, cw_ref, al_ref, dt_ref, cb0_ref,
                     u_ref, w_ref, qe_ref, kdt_ref, attn_ref, egl_ref, cb_out_ref, xe_ref, *, heads, t_valid):
    c = pl.program_id(1)
    ch = qkv_ref.shape[0]
    width = heads * HEAD_DIM
    tail = CONV_W - 1
    base = SUBLANES
    last_chunk = (t_valid - 1) // ch
    last_len = t_valid - last_chunk * ch

    @pl.when(c == 0)
    def _():
        xe_ref[base - tail:base, :] = cb0_ref[...]

    xe_ref[base:base + ch, :] = qkv_ref[...]
    y = xe_ref[base - tail:base - tail + ch, :] * cw_ref[0:1, :]
    for i in range(1, CONV_W):
        y = y + xe_ref[base - tail + i:base - tail + i + ch, :] * cw_ref[i:i + 1, :]
    y = _silu(y)

    @pl.when(c == last_chunk)
    def _():
        cb_out_ref[...] = xe_ref[base + last_len - tail:base + last_len, :]

    xe_ref[base - tail:base, :] = xe_ref[base + ch - tail:base + ch, :]

    valid = (c * ch + lax.broadcasted_iota(jnp.int32, (ch, 1), 0)) < t_valid
    ab = ab_ref[...]
    g_all = jnp.where(valid, -jnp.exp(al_ref[...]) * _softplus(ab + dt_ref[...]), 0.0)
    beta_all = jnp.where(valid, jax.nn.sigmoid(ab), 0.0)
    cum = _dot01(_tril01(ch).astype(BF16), g_all)
    cum_t = cum.T
    lower = _tril01(ch)
    strict = _tril01(ch, strict=True)
    egl_ref[...] = jnp.exp(cum[ch - 1:ch, :])

    hr = range(heads)
    sl = lambda part, h: slice(part * width + h * HEAD_DIM, part * width + (h + 1) * HEAD_DIM)
    gc = [cum[:, h:h + 1] for h in hr]
    beta = [beta_all[:, heads + h:heads + h + 1] for h in hr]
    decay = [jnp.exp(jnp.where(lower, gc[h] - cum_t[h:h + 1, :], NEG_BIG)) for h in hr]
    q = [y[:, sl(0, h)] for h in hr]
    k = [y[:, sl(1, h)] for h in hr]
    q = [x * lax.rsqrt(jnp.sum(x * x, axis=-1, keepdims=True) + EPS) * (HEAD_DIM ** -0.5) for x in q]
    k = [x * lax.rsqrt(jnp.sum(x * x, axis=-1, keepdims=True) + EPS) for x in k]
    kb = [k[h] * beta[h] for h in hr]
    k16 = [x.astype(BF16) for x in k]
    eg = [jnp.exp(gc[h]) for h in hr]
    sols = _unit_lower_solves(
        [jnp.where(strict, _dot_nt(kb[h].astype(BF16), k16[h]) * decay[h], 0.0) for h in hr],
        [jnp.concatenate([y[:, sl(2, h)] * beta[h], kb[h] * eg[h]], axis=-1) for h in hr])
    for h in hr:
        sol = sols[h]
        u_ref[:, sl(0, h)] = sol[:, :HEAD_DIM]
        w_ref[:, sl(0, h)] = sol[:, HEAD_DIM:].astype(BF16)
        qe_ref[:, sl(0, h)] = (q[h] * eg[h]).astype(BF16)
        attn_ref[h] = (_dot_nt(q[h].astype(BF16), k16[h]) * decay[h]).astype(BF16)
        kdt_ref[h] = (k[h] * jnp.exp(cum[ch - 1:ch, h:h + 1] - gc[h])).T.astype(BF16)


def _gdn_scan_kernel(u_ref, w_ref, qe_ref, kdt_ref, attn_ref, egl_ref, gz_ref, gn_ref, s0_ref,
                     o_ref, s_out_ref, st_ref, *, heads):
    c = pl.program_id(0)
    bx = u_ref.shape[0]

    @pl.when(c == 0)
    def _():
        st_ref[...] = s0_ref[...]

    chains = [(b, h) for b in range(bx) for h in range(heads)]
    hs = lambda h: slice(h * HEAD_DIM, (h + 1) * HEAD_DIM)
    s_old = [st_ref[b, h] for b, h in chains]
    s16 = [s.astype(BF16) for s in s_old]
    ws = [_dot(w_ref[b, :, hs(h)], s) for (b, h), s in zip(chains, s16)]
    qs = [_dot(qe_ref[b, :, hs(h)], s) for (b, h), s in zip(chains, s16)]
    vn16 = [(u_ref[b, :, hs(h)] - x).astype(BF16) for (b, h), x in zip(chains, ws)]
    for i, (b, h) in enumerate(chains):
        o = qs[i] + _dot(attn_ref[b, h], vn16[i])
        st_ref[b, h] = s_old[i] * egl_ref[b, :, h:h + 1] + _dot(kdt_ref[b, h], vn16[i])
        on = (o * lax.rsqrt(jnp.mean(o * o, axis=-1, keepdims=True) + EPS)) * gn_ref[...]
        o_ref[b, :, hs(h)] = (on * _silu(gz_ref[b, :, hs(h)])).astype(o_ref.dtype)

    @pl.when(c == pl.num_programs(0) - 1)
    def _():
        s_out_ref[...] = st_ref[...]


def _gdn(proj, narrow, lay, conv_w, a_log, dt_bias, gnorm, s0, cb0, layer, bx, tx, t_valid):
    n, wp = proj.shape
    gw = lay["gz"][1]
    heads = gw // HEAD_DIM
    ch = CHUNK
    nc = tx // ch
    qkv_c, ab_c, gz_c = lay["gqkv"][0] // (3 * gw), lay["gab"][0] // LANES, lay["gz"][0] // gw
    blocks = [((ch, 3 * gw), F32), ((ch, LANES), F32), ((CONV_W, 3 * gw), F32), ((CONV_W - 1, 3 * gw), F32),
              ((ch, gw), F32), ((ch, gw), BF16), ((ch, gw), BF16), ((heads, HEAD_DIM, LANES), BF16),
              ((heads, ch, LANES), BF16), ((CONV_W - 1, 3 * gw), F32)]
    scratch = [((SUBLANES + ch, 3 * gw), F32)]
    u, w16, qe16, kdt, attn, egl, cb_out = pl.pallas_call(
        functools.partial(_gdn_prep_kernel, heads=heads, t_valid=t_valid),
        grid=(bx, nc),
        in_specs=[pl.BlockSpec((ch, 3 * gw), lambda b, c: (b * nc + c, qkv_c)),
                  pl.BlockSpec((ch, LANES), lambda b, c: (b * nc + c, ab_c)),
                  pl.BlockSpec((None, CONV_W, 3 * gw), lambda b, c: (layer, 0, 0)),
                  pl.BlockSpec((None, 1, LANES), lambda b, c: (layer, 0, 0)),
                  pl.BlockSpec((None, 1, LANES), lambda b, c: (layer, 0, 0)),
                  pl.BlockSpec((None, CONV_W - 1, 3 * gw), lambda b, c: (b, 0, 0))],
        out_specs=[pl.BlockSpec((ch, gw), lambda b, c: (b * nc + c, 0)),
                   pl.BlockSpec((ch, gw), lambda b, c: (b * nc + c, 0)),
                   pl.BlockSpec((ch, gw), lambda b, c: (b * nc + c, 0)),
                   pl.BlockSpec((None, None, heads, HEAD_DIM, ch), lambda b, c: (b, c, 0, 0, 0)),
                   pl.BlockSpec((None, None, heads, ch, ch), lambda b, c: (b, c, 0, 0, 0)),
                   pl.BlockSpec((None, None, 1, LANES), lambda b, c: (b, c, 0, 0)),
                   pl.BlockSpec((None, CONV_W - 1, 3 * gw), lambda b, c: (b, 0, 0))],
        out_shape=[jax.ShapeDtypeStruct((n, gw), F32), jax.ShapeDtypeStruct((n, gw), BF16),
                   jax.ShapeDtypeStruct((n, gw), BF16),
                   jax.ShapeDtypeStruct((bx, nc, heads, HEAD_DIM, ch), BF16),
                   jax.ShapeDtypeStruct((bx, nc, heads, ch, ch), BF16),
                   jax.ShapeDtypeStruct((bx, nc, 1, LANES), F32),
                   jax.ShapeDtypeStruct((bx, CONV_W - 1, 3 * gw), F32)],
        scratch_shapes=[pltpu.VMEM(s, t) for s, t in scratch],
        compiler_params=_params(("parallel", "arbitrary"), blocks, scratch),
        name="gdn_prep")(proj, narrow, conv_w, a_log, dt_bias, cb0)

    state = (bx, heads, HEAD_DIM, HEAD_DIM)
    blocks = [((bx, ch, gw), F32), ((bx, ch, gw), BF16), ((bx, ch, gw), BF16), ((bx, heads, HEAD_DIM, LANES), BF16),
              ((bx, heads, ch, LANES), BF16), ((bx, ch, gw), F32), (state, F32), ((bx, ch, gw), BF16), (state, F32)]
    scratch = [(state, F32)]
    rows3 = lambda a: a.reshape(bx, tx, a.shape[-1])
    o, s_out = pl.pallas_call(
        functools.partial(_gdn_scan_kernel, heads=heads),
        grid=(nc,),
        in_specs=[pl.BlockSpec((bx, ch, gw), lambda c: (0, c, 0)),
                  pl.BlockSpec((bx, ch, gw), lambda c: (0, c, 0)),
                  pl.BlockSpec((bx, ch, gw), lambda c: (0, c, 0)),
                  pl.BlockSpec((bx, None, heads, HEAD_DIM, ch), lambda c: (0, c, 0, 0, 0)),
                  pl.BlockSpec((bx, None, heads, ch, ch), lambda c: (0, c, 0, 0, 0)),
                  pl.BlockSpec((bx, None, 1, LANES), lambda c: (0, c, 0, 0)),
                  pl.BlockSpec((bx, ch, gw), lambda c: (0, c, gz_c)),
                  pl.BlockSpec((None, 1, HEAD_DIM), lambda c: (layer, 0, 0)),
                  pl.BlockSpec(state, lambda c: (0, 0, 0, 0))],
        out_specs=[pl.BlockSpec((bx, ch, gw), lambda c: (0, c, 0)),
                   pl.BlockSpec(state, lambda c: (0, 0, 0, 0))],
        out_shape=[jax.ShapeDtypeStruct((bx, tx, gw), BF16), jax.ShapeDtypeStruct(state, F32)],
        scratch_shapes=[pltpu.VMEM(s, t) for s, t in scratch],
        compiler_params=_params(("arbitrary",), blocks, scratch),
        name="gdn_scan")(rows3(u), rows3(w16), rows3(qe16), kdt, attn, egl, rows3(proj), gnorm, s0)
    return o.reshape(n, gw), s_out, cb_out


def _hgrn_kernel(hq_ref, hf_ref, hi_ref, hg_ref, lb_ref, hn_ref, s0_ref, o_ref, s_out_ref, st_ref,
                 *, heads, t_valid):
    c = pl.program_id(1)
    ch = hq_ref.shape[0]
    nblk = ch // SUBLANES

    @pl.when(c == 0)
    def _():
        for h in range(heads):
            st_ref[h] = s0_ref[h].T

    tok = lax.broadcasted_iota(jnp.int32, (ch, 1), 0)
    valid = (c * ch + tok) < t_valid
    lb = lb_ref[...]
    hf = hf_ref[...]
    a = jnp.log(jnp.maximum(lb, LB_FLOOR))
    b = jnp.log1p(-lb) + _log_sigmoid(hf)
    log_f = jnp.maximum(a, b) + jnp.log1p(jnp.exp(-jnp.abs(a - b)))
    log_f = jnp.where(valid, log_f, 0.0)
    k_all = jnp.where(valid, (1.0 - lb) * jax.nn.sigmoid(-hf), 0.0)
    cum_all = _dot01(_tril01(ch).astype(BF16), log_f)
    q_all = _silu(hq_ref[...])

    ti = lax.broadcasted_iota(jnp.int32, (ch, ch), 0)
    tj = lax.broadcasted_iota(jnp.int32, (ch, ch), 1)
    sub3 = lax.broadcasted_iota(jnp.int32, (nblk, SUBLANES, HEAD_DIM), 1)

    hr = range(heads)
    hsl = [slice(h * HEAD_DIM, (h + 1) * HEAD_DIM) for h in hr]
    g = [cum_all[:, s] for s in hsl]
    q = [q_all[:, s] for s in hsl]
    k = [k_all[:, s] for s in hsl]
    v16 = [hi_ref[:, s].astype(BF16) for s in hsl]
    attn_t = [jnp.zeros((ch, ch), F32) for _ in hr]
    half = ch // 2
    while half >= SUBLANES:
        blk = 2 * half
        upper = (tok % blk) >= half
        same = (ti // blk) == (tj // blk)
        es = []
        for h in hr:
            g3 = g[h].reshape(ch // blk, blk, HEAD_DIM)
            es.append(jnp.exp(-jnp.abs(g3 - g3[:, half - 1:half, :])).reshape(ch, HEAD_DIM))
        prods = [_dot_nt(jnp.where(upper, 0.0, k[h] * es[h]).astype(BF16),
                         jnp.where(upper, q[h] * es[h], 0.0).astype(BF16)) for h in hr]
        attn_t = [a + jnp.where(same, p, 0.0) for a, p in zip(attn_t, prods)]
        half //= 2
    diag = (ti // SUBLANES) == (tj // SUBLANES)
    for h in hr:
        g3, q3, k3 = (z.reshape(nblk, SUBLANES, HEAD_DIM) for z in (g[h], q[h], k[h]))
        cols = []
        for i in range(SUBLANES):
            e = jnp.exp(jnp.where(sub3 <= i, g3[:, i:i + 1, :] - g3, NEG_BIG))
            w = jnp.sum(e * k3 * q3[:, i:i + 1, :], axis=-1, keepdims=True).reshape(ch, 1)
            cols.append(jnp.where((tj % SUBLANES) == i, w, 0.0))
        while len(cols) > 1:
            cols = [a + b for a, b in zip(cols[0::2], cols[1::2])]
        attn_t[h] = attn_t[h] + jnp.where(diag, cols[0], 0.0)
    st = [st_ref[h] for h in hr]
    o = [_dot_nt((q[h] * jnp.exp(g[h])).astype(BF16), st[h].astype(BF16)) + _dot(attn_t[h].T.astype(BF16), v16[h])
         for h in hr]
    for h in hr:
        gl = g[h][ch - 1:ch, :]
        kd = k[h] * jnp.exp(gl - g[h])
        st_ref[h] = st[h] * jnp.exp(gl) + _dot(hi_ref[:, hsl[h]].T.astype(BF16), kd.astype(BF16))
        on = (o[h] * lax.rsqrt(jnp.mean(o[h] * o[h], axis=-1, keepdims=True) + EPS)) * hn_ref[...]
        o_ref[:, hsl[h]] = (on * _silu(hg_ref[:, hsl[h]])).astype(o_ref.dtype)

    @pl.when(c == pl.num_programs(1) - 1)
    def _():
        for h in range(heads):
            s_out_ref[h] = st_ref[h].T


def _hgrn(proj, lay, lb, hnorm, s0, layer, bx, tx, t_valid):
    n = proj.shape[0]
    hw = lay["hq"][1]
    heads = hw // HEAD_DIM
    ch = CHUNK
    nc = tx // ch
    cols = [lay[k][0] // hw for k in ("hq", "hf", "hi", "hg")]
    blocks = [((ch, hw), F32)] * 4 + [((heads, HEAD_DIM, HEAD_DIM), F32)] * 2 + [((ch, hw), BF16)]
    scratch = [((heads, HEAD_DIM, HEAD_DIM), F32)]
    in_specs = [pl.BlockSpec((ch, hw), functools.partial(lambda b, c, col: (b * nc + c, col), col=col))
                for col in cols]
    in_specs += [pl.BlockSpec((None, 1, hw), lambda b, c: (layer, 0, 0)),
                 pl.BlockSpec((None, 1, HEAD_DIM), lambda b, c: (layer, 0, 0)),
                 pl.BlockSpec((None, heads, HEAD_DIM, HEAD_DIM), lambda b, c: (b, 0, 0, 0))]
    return pl.pallas_call(
        functools.partial(_hgrn_kernel, heads=heads, t_valid=t_valid),
        grid=(bx, nc),
        in_specs=in_specs,
        out_specs=[pl.BlockSpec((ch, hw), lambda b, c: (b * nc + c, 0)),
                   pl.BlockSpec((None, heads, HEAD_DIM, HEAD_DIM), lambda b, c: (b, 0, 0, 0))],
        out_shape=[jax.ShapeDtypeStruct((n, hw), BF16),
                   jax.ShapeDtypeStruct((bx, heads, HEAD_DIM, HEAD_DIM), F32)],
        scratch_shapes=[pltpu.VMEM(s, t) for s, t in scratch],
        compiler_params=_params(("parallel", "arbitrary"), blocks, scratch),
        name="hgrn2")(proj, proj, proj, proj, lb, hnorm, s0)


def _layout(d_model, fw, gw, hw):
    order = [("gate_a", d_model), ("gate_b", d_model), ("gate_c", d_model), ("fq", fw), ("fk", fw), ("fv", fw),
             ("gqkv", 3 * gw), ("gz", gw), ("hq", hw), ("hf", hw), ("hi", hw), ("hg", hw)]
    lay, start = {}, 0
    for name, width in order:
        lay[name] = (start, width)
        start += width
    lay["total"] = (0, start)
    lay["ff"], lay["gab"] = (0, LANES), (LANES, LANES)
    return lay


def _source_columns(d_model, fh, gh, fw, gw, hw):
    splits = [("fq", fw), ("fk", fw), ("fv", fw), ("ff", fh), ("gq", gw), ("gk", gw), ("gv", gw), ("ga", gh),
              ("gb", gh), ("gz", gw), ("hq", hw), ("hf", hw), ("hi", hw), ("hg", hw),
              ("gate_a", d_model), ("gate_b", d_model), ("gate_c", d_model)]
    src, start = {}, 0
    for name, width in splits:
        src[name] = start
        start += width
    src["end"] = start
    return src


def _in_proj_row_starts(lay, src, tn):
    starts = []
    for first, dst, width in (("gate_a", "gate_a", 3 * lay["gate_a"][1]), ("fq", "fq", 3 * lay["fq"][1]),
                              ("gq", "gqkv", lay["gqkv"][1]), ("gz", "gz", lay["total"][1] - lay["gz"][0])):
        assert lay[dst][0] == len(starts) * tn and width % tn == 0
        starts += [src[first] + k * tn for k in range(width // tn)]
    assert len(starts) * tn == lay["total"][1]
    return starts


def _pad_lanes(x, fill=0.0):
    return jnp.pad(x.astype(F32), ((0, 0), (0, LANES - x.shape[-1])), constant_values=fill)[:, None, :]


def _tile(n, candidates):
    for c in candidates:
        if n % c == 0:
            return c
    raise ValueError(f"no tile for {n} among {candidates}")


def kernel(x_prompt, x_sample, cache_fox_k, cache_fox_v, cache_fox_logf, state_gdn, state_gdn_conv, state_hgrn,
           page_table, norm_mix_pre, norm_mix_post, norm_mlp_pre, norm_mlp_post, w_in, fox_f_bias, gdn_conv_w,
           gdn_a_log, gdn_dt_bias, gdn_norm, hgrn_lower_bounds, hgrn_norm, w_fox_o, w_gdn_o, w_hgrn_o, w_out,
           w_up, w_down):
    bp, tp, d = x_prompt.shape
    bs, ts, _ = x_sample.shape
    depth = w_in.shape[0]
    fh, gh = fox_f_bias.shape[1], gdn_a_log.shape[1]
    fw, gw, hw = fh * HEAD_DIM, gh * HEAD_DIM, hgrn_lower_bounds.shape[1]
    hh = hw // HEAD_DIM
    n_pool, page = cache_fox_k.shape[1], cache_fox_k.shape[2]
    n_pages = page_table.shape[1]
    tsp = CHUNK
    assert tp % CHUNK == 0 and tp % LANES == 0 and ts <= tsp and ts >= CONV_W - 1
    assert page * fh == SUBLANES * LANES and 2 * gh <= LANES and fh <= LANES
    lay = _layout(d, fw, gw, hw)
    wp = lay["total"][1]
    src = _source_columns(d, fh, gh, fw, gw, hw)
    assert src["end"] == w_in.shape[-1] and src["gb"] == src["ga"] + gh
    row_starts = _in_proj_row_starts(lay, src, IN_PROJ_COL_TILE)

    w_in_t = jnp.swapaxes(w_in.astype(F32), 1, 2)
    w_u = w_up.astype(F32)
    w_fo, w_go, w_ho, w_o, w_d = (w.astype(BF16) for w in (w_fox_o, w_gdn_o, w_hgrn_o, w_out, w_down))
    g_mix_pre, g_mix_post, g_mlp_pre, g_mlp_post = (
        g.astype(F32)[:, None, :] for g in (norm_mix_pre, norm_mix_post, norm_mlp_pre, norm_mlp_post))
    f_bias = _pad_lanes(fox_f_bias)
    a_log, dt_bias = _pad_lanes(gdn_a_log), _pad_lanes(gdn_dt_bias)
    g_norm, h_norm = gdn_norm.astype(F32)[:, None, :], hgrn_norm.astype(F32)[:, None, :]
    conv_w = gdn_conv_w.astype(F32)
    lb_all = _lower_bounds(hgrn_lower_bounds)[:, None, :]

    ck = cache_fox_k.reshape(depth * n_pool, page * fh, HEAD_DIM)
    cv = cache_fox_v.reshape(depth * n_pool, page * fh, HEAD_DIM)
    clf = cache_fox_logf.astype(F32).reshape(depth * n_pool, SUBLANES, LANES)
    bias_page = jnp.tile(fox_f_bias.astype(F32), (1, LANES // fh))[:, None, :] * jnp.ones((1, SUBLANES, 1), F32)
    bias_col = jnp.tile(fox_f_bias.astype(F32), (1, ts))[:, :, None]

    zeros_state_p = jnp.zeros((bp, gh, HEAD_DIM, HEAD_DIM), F32)
    zeros_hstate_p = jnp.zeros((bp, hh, HEAD_DIM, HEAD_DIM), F32)
    zeros_conv_p = jnp.zeros((bp, CONV_W - 1, 3 * gw), F32)

    xp = x_prompt.reshape(bp * tp, d)
    xs = jnp.pad(x_sample, ((0, 0), (0, tsp - ts), (0, 0))).reshape(bs * tsp, d)

    tm_p = _tile(bp * tp, (1024, 512, 256, 128, 64))
    tm_s = _tile(bs * tsp, (512, 256, 128, 64))
    tq = _tile(tp, FOX_Q_TILES)

    def in_proj(xn, l, tm):
        proj = _in_proj(xn, w_in_t, row_starts, l, tm=tm, tn=IN_PROJ_COL_TILE)
        narrow = _in_proj_narrow(xn, w_in_t, src["ff"], fh, src["ga"], 2 * gh, l, tm=tm)
        return proj, narrow

    def dense_tail(x, oa, ob, oc, proj, l, tm):
        merged = _merge(oa, ob, oc, proj, lay, w_fo, w_go, w_ho, l, tm=min(tm, MERGE_ROW_TILE), tn=d)
        x, xn = _matmul_norm_res(merged, w_o, x, g_mix_post, l, g_mlp_pre, l, tm=min(tm, RESIDUAL_ROW_TILE),
                                 tk=_tile(d, RESIDUAL_K_TILES), name="out_proj_norm_res")
        u = _mlp_up(xn, w_u, l, tm=tm, tn=_tile(w_u.shape[-1], (1024, 512, 256)))
        return _matmul_norm_res(u, w_d, x, g_mlp_post, l, g_mix_pre, (l + 1) % depth,
                                tm=min(tm, RESIDUAL_ROW_TILE), tk=_tile(w_d.shape[1], RESIDUAL_K_TILES),
                                name="mlp_down_norm_res")

    xpn = _rmsnorm_cast(xp, g_mix_pre, 0, tm=tm_p)
    xsn = _rmsnorm_cast(xs, g_mix_pre, 0, tm=tm_s)
    outs = {k: [] for k in ("fkp", "fvp", "flp", "fks", "fvs", "fls", "gsp", "gss", "gcp", "gcs", "hsp", "hss")}
    for l in range(depth):
        proj, narrow = in_proj(xpn, l, tm_p)
        lf, c_tok, c_row = _fox_gate(narrow, lay, f_bias, l, bp, tp)
        oa = _fox_attn(proj, lay, c_tok, c_row, bp, tp, tq=tq)
        ob, gs, gc = _gdn(proj, narrow, lay, conv_w, a_log, dt_bias, g_norm, zeros_state_p, zeros_conv_p, l, bp, tp,
                          tp)
        oc, hs = _hgrn(proj, lay, lb_all, h_norm, zeros_hstate_p, l, bp, tp, tp)
        xp, xpn = dense_tail(xp, oa, ob, oc, proj, l, tm_p)
        outs["fkp"].append(proj[:, lay["fk"][0]:lay["fk"][0] + fw].reshape(bp, tp, fh, HEAD_DIM))
        outs["fvp"].append(proj[:, lay["fv"][0]:lay["fv"][0] + fw].reshape(bp, tp, fh, HEAD_DIM))
        outs["flp"].append(lf[:, :fh].reshape(bp, tp, fh))
        outs["gsp"].append(gs)
        outs["gcp"].append(gc)
        outs["hsp"].append(hs)

        proj, narrow = in_proj(xsn, l, tm_s)
        p3 = proj.reshape(bs, tsp, wp)[:, :ts]
        q_new, k_new, v_new = (p3[:, :, lay[k][0]:lay[k][0] + fw] for k in ("fq", "fk", "fv"))
        ff_new = narrow.reshape(bs, tsp, 2 * LANES)[:, :ts, lay["ff"][0]:lay["ff"][0] + fh]
        rows = ts * fh
        pad_rows = lambda z: jnp.pad(z.reshape(bs, rows, HEAD_DIM), ((0, 0), (0, LANES - rows), (0, 0)))
        ff_page = jnp.pad(ff_new.reshape(bs, 1, rows), ((0, 0), (0, SUBLANES - 1), (0, LANES - rows)))
        oa_s, lf_page = _fox_decode(
            q_new.reshape(bs, rows, HEAD_DIM), pad_rows(k_new), pad_rows(v_new), ff_page, bias_page[l],
            ff_new.reshape(bs, rows, 1), bias_col[l], ck, cv, clf,
            (page_table.astype(jnp.int32) + l * n_pool).reshape(-1), n_pages, heads=fh, n_new=ts)
        oa = jnp.pad(oa_s.reshape(bs, ts, fw), ((0, 0), (0, tsp - ts), (0, 0))).reshape(bs * tsp, fw)
        ob, gs, gc = _gdn(proj, narrow, lay, conv_w, a_log, dt_bias, g_norm, state_gdn[l].astype(F32),
                          state_gdn_conv[l].astype(F32), l, bs, tsp, ts)
        oc, hs = _hgrn(proj, lay, lb_all, h_norm, state_hgrn[l].astype(F32), l, bs, tsp, ts)
        xs, xsn = dense_tail(xs, oa, ob, oc, proj, l, tm_s)
        outs["fks"].append(k_new.reshape(bs, ts, fh, HEAD_DIM))
        outs["fvs"].append(v_new.reshape(bs, ts, fh, HEAD_DIM))
        outs["fls"].append(lf_page[:, 0, :rows].reshape(bs, ts, fh))
        outs["gss"].append(gs)
        outs["gcs"].append(gc)
        outs["hss"].append(hs)

    st = lambda k: jnp.stack(outs[k])
    return (xp.reshape(bp, tp, d), xs.reshape(bs, tsp, d)[:, :ts],
            st("fkp"), st("fvp"), st("flp"), st("fks"), st("fvs"), st("fls"),
            st("gsp"), st("gss"), st("gcp"), st("gcs"), st("hsp"), st("hss"))
```

```python
import functools

import jax
import jax.numpy as jnp
from jax import lax
from jax.experimental import pallas as pl
from jax.experimental.pallas import tpu as pltpu

F32, BF16 = jnp.float32, jnp.bfloat16
HEAD_DIM = 128
CONV_W = 4
CHUNK = 64
EPS = 1e-6
NEG_BIG = -1e30
LB_FLOOR = 1e-30
LANES = 128
SUBLANES = 8
V7X_VMEM_BYTES = 64 * 2**20
VMEM_RESERVE_BYTES = 6 * 2**20
KERNEL_TEMP_BYTES = 16 * 2**20
DECODE_PAGES_PER_STEP = 16
FOX_Q_TILES = (1024, 512, 256, 128)
IN_PROJ_COL_TILE = 1024
MERGE_ROW_TILE = 256
RESIDUAL_ROW_TILE = 512
RESIDUAL_K_TILES = (2048, 1024, 512, 256)
RESIDUAL_EPILOGUE_GROUPS = 4
SAMPLE_DENSE_ROWS = 16


def _nbytes(shape, dtype):
    n = 1
    for s in shape:
        n *= s
    return n * jnp.dtype(dtype).itemsize


def _params(semantics, blocks, scratch=()):
    need = 2 * sum(_nbytes(s, d) for s, d in blocks) + sum(_nbytes(s, d) for s, d in scratch) + KERNEL_TEMP_BYTES
    return pltpu.CompilerParams(dimension_semantics=semantics,
                                vmem_limit_bytes=min(need, V7X_VMEM_BYTES - VMEM_RESERVE_BYTES))


def _dot(a, b):
    return jnp.dot(a, b, preferred_element_type=F32)


def _dot_nt(a, b):
    return lax.dot_general(a, b, (((1,), (1,)), ((), ())), preferred_element_type=F32)


def _split3(x):
    x1 = x.astype(BF16)
    r = x - x1.astype(F32)
    x2 = r.astype(BF16)
    x3 = (r - x2.astype(F32)).astype(BF16)
    return x1, x2, x3


def _dot01(m01, x):
    x1, x2, x3 = _split3(x)
    return _dot(m01, x1) + (_dot(m01, x2) + _dot(m01, x3))


def _log_sigmoid(x):
    return -(jnp.maximum(-x, 0.0) + jnp.log1p(jnp.exp(-jnp.abs(x))))


def _softplus(x):
    return jnp.maximum(x, 0.0) + jnp.log1p(jnp.exp(-jnp.abs(x)))


def _silu(x):
    return x * jax.nn.sigmoid(x)


def _tril01(n, strict=False):
    r = lax.broadcasted_iota(jnp.int32, (n, n), 0)
    c = lax.broadcasted_iota(jnp.int32, (n, n), 1)
    return (r > c) if strict else (r >= c)


def _lower_bounds_kernel(x_ref, o_ref):
    x = x_ref[...]
    e = jnp.exp(x - jnp.max(x, axis=0, keepdims=True))
    p = e / jnp.sum(e, axis=0, keepdims=True)
    run = jnp.zeros_like(p[0:1])
    for l in range(x.shape[0]):
        run = run + p[l:l + 1]
        o_ref[l:l + 1, :] = run - p[0:1]


def _lower_bounds(x):
    return pl.pallas_call(_lower_bounds_kernel, out_shape=jax.ShapeDtypeStruct(x.shape, F32),
                          name="hgrn_lower_bounds")(x.astype(F32))


def _rmsnorm_bf16(x, g):
    var = jnp.mean(x * x, axis=-1, keepdims=True)
    return ((x * lax.rsqrt(var + EPS)) * g).astype(BF16)


def _rmsnorm_cast_kernel(x_ref, g_ref, o_ref):
    o_ref[...] = _rmsnorm_bf16(x_ref[...], g_ref[...])


def _rmsnorm_cast(x, g, layer, *, tm):
    n, d = x.shape
    blocks = [((tm, d), F32), ((1, d), F32), ((tm, d), BF16)]
    return pl.pallas_call(
        _rmsnorm_cast_kernel,
        grid=(n // tm,),
        in_specs=[pl.BlockSpec((tm, d), lambda i: (i, 0)), pl.BlockSpec((None, 1, d), lambda i: (layer, 0, 0))],
        out_specs=pl.BlockSpec((tm, d), lambda i: (i, 0)),
        out_shape=jax.ShapeDtypeStruct((n, d), BF16),
        compiler_params=_params(("parallel",), blocks),
        name="rmsnorm_cast")(x, g)


def _in_proj_kernel(offs_ref, xn_ref, w_ref, o_ref, w16_ref):
    del offs_ref

    @pl.when(pl.program_id(1) == 0)
    def _():
        w16_ref[...] = w_ref[0].astype(BF16)

    o_ref[...] = _dot_nt(xn_ref[...], w16_ref[...])


def _in_proj(xn, w_t, row_starts, layer, *, tm, tn):
    n, d = xn.shape
    assert all(s % SUBLANES == 0 for s in row_starts)
    offs = jnp.asarray([s // SUBLANES for s in row_starts], jnp.int32)
    blocks = [((tm, d), BF16), ((tn, d), F32), ((tm, tn), F32)]
    scratch = [((tn, d), BF16)]
    grid_spec = pltpu.PrefetchScalarGridSpec(
        num_scalar_prefetch=1, grid=(len(row_starts), n // tm),
        in_specs=[pl.BlockSpec((tm, d), lambda j, i, offs: (i, 0)),
                  pl.BlockSpec((pl.Element(1), pl.Element(tn), pl.Element(d)),
                               lambda j, i, offs: (layer, offs[j] * SUBLANES, 0))],
        out_specs=pl.BlockSpec((tm, tn), lambda j, i, offs: (i, j)),
        scratch_shapes=[pltpu.VMEM(s, t) for s, t in scratch])
    return pl.pallas_call(
        _in_proj_kernel, grid_spec=grid_spec,
        out_shape=jax.ShapeDtypeStruct((n, len(row_starts) * tn), F32),
        compiler_params=_params(("arbitrary", "arbitrary"), blocks, scratch),
        name="in_proj")(offs, xn, w_t)


def _in_proj_narrow_kernel(xn_ref, a_ref, b_ref, o_ref, w16_ref):
    @pl.when(pl.program_id(0) == 0)
    def _():
        pad = lambda w: jnp.zeros((LANES - w.shape[0], w.shape[1]), F32)
        a, b = a_ref[0], b_ref[0]
        w16_ref[...] = jnp.concatenate([a, pad(a), b, pad(b)], axis=0).astype(BF16)

    o_ref[...] = _dot_nt(xn_ref[...], w16_ref[...])


def _in_proj_narrow(xn, w_t, a_start, a_rows, b_start, b_rows, layer, *, tm):
    n, d = xn.shape
    assert a_start % SUBLANES == 0 and b_start % SUBLANES == 0 and a_rows % SUBLANES == 0 and b_rows % SUBLANES == 0
    element = lambda rows, start: pl.BlockSpec((pl.Element(1), pl.Element(rows), pl.Element(d)),
                                               lambda i: (layer, start, 0))
    blocks = [((tm, d), BF16), ((a_rows, d), F32), ((b_rows, d), F32), ((tm, 2 * LANES), F32)]
    scratch = [((2 * LANES, d), BF16)]
    return pl.pallas_call(
        _in_proj_narrow_kernel,
        grid=(n // tm,),
        in_specs=[pl.BlockSpec((tm, d), lambda i: (i, 0)), element(a_rows, a_start), element(b_rows, b_start)],
        out_specs=pl.BlockSpec((tm, 2 * LANES), lambda i: (i, 0)),
        out_shape=jax.ShapeDtypeStruct((n, 2 * LANES), F32),
        scratch_shapes=[pltpu.VMEM(s, t) for s, t in scratch],
        compiler_params=_params(("arbitrary",), blocks, scratch),
        name="in_proj_narrow")(xn, w_t, w_t)


def _mlp_up_kernel(xn_ref, w_ref, o_ref, w16_ref):
    @pl.when(pl.program_id(1) == 0)
    def _():
        w16_ref[...] = w_ref[...].astype(BF16)

    y = jnp.maximum(_dot(xn_ref[...], w16_ref[...]), 0.0)
    o_ref[...] = (y * y).astype(o_ref.dtype)


def _mlp_up(xn, w, layer, *, tm, tn):
    n, d = xn.shape
    f = w.shape[-1]
    blocks = [((tm, d), BF16), ((d, tn), F32), ((tm, tn), BF16)]
    scratch = [((d, tn), BF16)]
    return pl.pallas_call(
        _mlp_up_kernel,
        grid=(f // tn, n // tm),
        in_specs=[pl.BlockSpec((tm, d), lambda j, i: (i, 0)),
                  pl.BlockSpec((None, d, tn), lambda j, i: (layer, 0, j))],
        out_specs=pl.BlockSpec((tm, tn), lambda j, i: (i, j)),
        out_shape=jax.ShapeDtypeStruct((n, f), BF16),
        scratch_shapes=[pltpu.VMEM(s, t) for s, t in scratch],
        compiler_params=_params(("arbitrary", "arbitrary"), blocks, scratch),
        name="mlp_up")(xn, w)


def _matmul_norm_res_kernel(a_ref, w_ref, x_ref, g_ref, gn_ref, o_ref, xn_ref, acc_ref):
    k = pl.program_id(1)
    last = pl.num_programs(1) - 1
    tm = a_ref.shape[0]

    @pl.when(k == 0)
    def _():
        acc_ref[...] = jnp.zeros_like(acc_ref)

    @pl.when(k < last)
    def _():
        acc_ref[...] += _dot(a_ref[...], w_ref[...])

    @pl.when(k == last)
    def _():
        for r in range(RESIDUAL_EPILOGUE_GROUPS):
            rows = slice(r * (tm // RESIDUAL_EPILOGUE_GROUPS), (r + 1) * (tm // RESIDUAL_EPILOGUE_GROUPS))
            y = acc_ref[rows, :] + _dot(a_ref[rows, :], w_ref[...])
            var = jnp.mean(y * y, axis=-1, keepdims=True)
            x_new = x_ref[rows, :] + (y * lax.rsqrt(var + EPS)) * g_ref[...]
            o_ref[rows, :] = x_new
            xn_ref[rows, :] = _rmsnorm_bf16(x_new, gn_ref[...])


def _matmul_norm_res(a, w, x, g, layer, g_next, layer_next, *, tm, tk, name):
    n, kdim = a.shape
    d = x.shape[1]
    blocks = [((tm, tk), BF16), ((tk, d), BF16), ((tm, d), F32), ((1, d), F32), ((1, d), F32), ((tm, d), F32),
              ((tm, d), BF16)]
    scratch = [((tm, d), F32)]
    return pl.pallas_call(
        _matmul_norm_res_kernel,
        grid=(n // tm, kdim // tk),
        in_specs=[pl.BlockSpec((tm, tk), lambda i, k: (i, k)),
                  pl.BlockSpec((None, tk, d), lambda i, k: (layer, k, 0)),
                  pl.BlockSpec((tm, d), lambda i, k: (i, 0)),
                  pl.BlockSpec((None, 1, d), lambda i, k: (layer, 0, 0)),
                  pl.BlockSpec((None, 1, d), lambda i, k: (layer_next, 0, 0))],
        out_specs=[pl.BlockSpec((tm, d), lambda i, k: (i, 0)), pl.BlockSpec((tm, d), lambda i, k: (i, 0))],
        out_shape=[jax.ShapeDtypeStruct((n, d), F32), jax.ShapeDtypeStruct((n, d), BF16)],
        scratch_shapes=[pltpu.VMEM(s, t) for s, t in scratch],
        compiler_params=_params(("parallel", "arbitrary"), blocks, scratch),
        name=name)(a, w, x, g, g_next)


def _merge_kernel(oa_ref, ob_ref, oc_ref, ga_ref, gb_ref, gc_ref, wa_ref, wb_ref, wc_ref, o_ref):
    m = jax.nn.sigmoid(ga_ref[...]) * _dot(oa_ref[...], wa_ref[...])
    m = m + jax.nn.sigmoid(gb_ref[...]) * _dot(ob_ref[...], wb_ref[...])
    m = m + jax.nn.sigmoid(gc_ref[...]) * _dot(oc_ref[...], wc_ref[...])
    o_ref[...] = m.astype(o_ref.dtype)


def _merge(oa, ob, oc, proj, lay, wa, wb, wc, layer, *, tm, tn):
    n = oa.shape[0]
    d = wa.shape[-1]
    col = {k: lay[k][0] // tn for k in ("gate_a", "gate_b", "gate_c")}
    o_specs = [pl.BlockSpec((tm, o.shape[1]), lambda i, j: (i, 0)) for o in (oa, ob, oc)]
    g_specs = [pl.BlockSpec((tm, tn), functools.partial(lambda i, j, c: (i, c + j), c=col[k]))
               for k in ("gate_a", "gate_b", "gate_c")]
    w_specs = [pl.BlockSpec((None, w.shape[1], tn), lambda i, j: (layer, 0, j)) for w in (wa, wb, wc)]
    blocks = ([((tm, o.shape[1]), BF16) for o in (oa, ob, oc)] + [((tm, tn), F32)] * 3
              + [((w.shape[1], tn), BF16) for w in (wa, wb, wc)] + [((tm, tn), BF16)])
    return pl.pallas_call(
        _merge_kernel,
        grid=(n // tm, d // tn),
        in_specs=o_specs + g_specs + w_specs,
        out_specs=pl.BlockSpec((tm, tn), lambda i, j: (i, j)),
        out_shape=jax.ShapeDtypeStruct((n, d), BF16),
        compiler_params=_params(("parallel", "parallel"), blocks),
        name="gated_merge")(oa, ob, oc, proj, proj, proj, wa, wb, wc)


def _fox_gate_kernel(ff_ref, b_ref, lf_ref, c_ref, ct_ref):
    t = ff_ref.shape[0]
    lf_ref[...] = _log_sigmoid(ff_ref[...] + b_ref[...])
    tri = _tril01(LANES).astype(BF16)
    carry = jnp.zeros((1, LANES), F32)
    for blk in range(t // LANES):
        rows = slice(blk * LANES, (blk + 1) * LANES)
        cb = _dot01(tri, lf_ref[rows, :]) + carry
        c_ref[rows, :] = cb
        ct_ref[:, rows] = cb.T
        carry = cb[LANES - 1:LANES, :]


def _fox_gate(proj, lay, bias, layer, bx, tx):
    n = proj.shape[0]
    col = lay["ff"][0] // LANES
    blocks = [((tx, LANES), F32)] * 4 + [((LANES, tx), F32)]
    return pl.pallas_call(
        _fox_gate_kernel,
        grid=(bx,),
        in_specs=[pl.BlockSpec((tx, LANES), lambda b: (b, col)),
                  pl.BlockSpec((None, 1, LANES), lambda b: (layer, 0, 0))],
        out_specs=[pl.BlockSpec((tx, LANES), lambda b: (b, 0)),
                   pl.BlockSpec((tx, LANES), lambda b: (b, 0)),
                   pl.BlockSpec((None, LANES, tx), lambda b: (b, 0, 0))],
        out_shape=[jax.ShapeDtypeStruct((n, LANES), F32), jax.ShapeDtypeStruct((n, LANES), F32),
                   jax.ShapeDtypeStruct((bx, LANES, tx), F32)],
        compiler_params=_params(("parallel",), blocks),
        name="fox_gate")(proj, bias)


def _fox_attn_kernel(q_ref, k_ref, v_ref, ct_ref, cr_ref, o_ref, m_ref, l_ref, acc_ref, *, heads, scale):
    i, j = pl.program_id(1), pl.program_id(2)
    tq, tk = q_ref.shape[0], k_ref.shape[0]

    @pl.when(j == 0)
    def _():
        m_ref[...] = jnp.full_like(m_ref, NEG_BIG)
        l_ref[...] = jnp.zeros_like(l_ref)
        acc_ref[...] = jnp.zeros_like(acc_ref)

    @pl.when(j <= i)
    def _():
        q_pos = i * tq + lax.broadcasted_iota(jnp.int32, (tq, tk), 0)
        k_pos = j * tk + lax.broadcasted_iota(jnp.int32, (tq, tk), 1)
        causal = q_pos >= k_pos
        for h in range(heads):
            hs = slice(h * HEAD_DIM, (h + 1) * HEAD_DIM)
            s = _dot_nt(q_ref[:, hs].astype(BF16), k_ref[:, hs].astype(BF16)) * scale
            s = s + ct_ref[:, h:h + 1] - cr_ref[h:h + 1, :]
            s = jnp.where(causal, s, NEG_BIG)
            m_prev = m_ref[:, h:h + 1]
            m_new = jnp.maximum(m_prev, jnp.max(s, axis=-1, keepdims=True))
            alpha = jnp.exp(m_prev - m_new)
            p = jnp.exp(s - m_new)
            l_ref[:, h:h + 1] = alpha * l_ref[:, h:h + 1] + jnp.sum(p, axis=-1, keepdims=True)
            acc_ref[:, hs] = alpha * acc_ref[:, hs] + _dot(p.astype(BF16), v_ref[:, hs].astype(BF16))
            m_ref[:, h:h + 1] = m_new

    @pl.when(j == i)
    def _():
        for h in range(heads):
            hs = slice(h * HEAD_DIM, (h + 1) * HEAD_DIM)
            o_ref[:, hs] = (acc_ref[:, hs] / l_ref[:, h:h + 1]).astype(o_ref.dtype)


def _fox_attn(proj, lay, c_tok, c_row, bx, tx, *, tq):
    n = proj.shape[0]
    fw = lay["fq"][1]
    heads = fw // HEAD_DIM
    nq = tx // tq
    qc, kc, vc = (lay[k][0] // fw for k in ("fq", "fk", "fv"))
    blocks = [((tq, fw), F32)] * 3 + [((tq, LANES), F32), ((LANES, tq), F32), ((tq, fw), BF16)]
    scratch = [((tq, LANES), F32), ((tq, LANES), F32), ((tq, fw), F32)]
    return pl.pallas_call(
        functools.partial(_fox_attn_kernel, heads=heads, scale=HEAD_DIM ** -0.5),
        grid=(bx, nq, nq),
        in_specs=[pl.BlockSpec((tq, fw), lambda b, i, j: (b * nq + i, qc)),
                  pl.BlockSpec((tq, fw), lambda b, i, j: (b * nq + jnp.minimum(j, i), kc)),
                  pl.BlockSpec((tq, fw), lambda b, i, j: (b * nq + jnp.minimum(j, i), vc)),
                  pl.BlockSpec((tq, LANES), lambda b, i, j: (b * nq + i, 0)),
                  pl.BlockSpec((None, LANES, tq), lambda b, i, j: (b, 0, jnp.minimum(j, i)))],
        out_specs=pl.BlockSpec((tq, fw), lambda b, i, j: (b * nq + i, 0)),
        out_shape=jax.ShapeDtypeStruct((n, fw), BF16),
        scratch_shapes=[pltpu.VMEM(s, t) for s, t in scratch],
        compiler_params=_params(("parallel", "parallel", "arbitrary"), blocks, scratch),
        name="fox_attn_prompt")(proj, proj, proj, c_tok, c_row)


def _page_suffix(x, lane, sub, heads):
    y = x
    sh = heads
    while sh < LANES:
        y = y + jnp.where(lane < LANES - sh, pltpu.roll(y, LANES - sh, axis=1), 0.0)
        sh *= 2
    t = jnp.where(lane < heads, y, 0.0)
    sh = heads
    while sh < LANES:
        t = t + pltpu.roll(t, sh, axis=1)
        sh *= 2
    z = t
    sh = 1
    while sh < SUBLANES:
        z = z + jnp.where(sub < SUBLANES - sh, pltpu.roll(z, SUBLANES - sh, axis=0), 0.0)
        sh *= 2
    return (y - x) + (z - t), z[0:1, :]


def _fox_decode_kernel(pt_ref, q_ref, kn_ref, vn_ref, ffp_ref, bp_ref, ffc_ref, bc_ref, *rest,
                       pages, heads, n_new, scale):
    del pt_ref
    k_refs, v_refs, lf_refs = rest[:pages], rest[pages:2 * pages], rest[2 * pages:3 * pages]
    o_ref, lfo_ref, m_ref, l_ref, acc_ref, carry_ref, rn_ref = rest[3 * pages:]
    step = pl.program_id(1)
    rows = q_ref.shape[0]
    lane = lax.broadcasted_iota(jnp.int32, (SUBLANES, LANES), 1)
    sub = lax.broadcasted_iota(jnp.int32, (SUBLANES, LANES), 0)
    r_i = lax.broadcasted_iota(jnp.int32, (rows, LANES), 0)
    c_i = lax.broadcasted_iota(jnp.int32, (rows, LANES), 1)
    head_ok = (r_i % heads) == (c_i % heads)
    qb = q_ref[...].astype(BF16)

    def update(scores, values):
        m_prev = m_ref[...]
        m_new = m_prev
        for s in scores:
            m_new = jnp.maximum(m_new, jnp.max(s, axis=-1, keepdims=True))
        alpha = jnp.exp(m_prev - m_new)
        l_new = alpha * l_ref[...]
        acc = alpha * acc_ref[...]
        for s, v in zip(scores, values):
            p = jnp.exp(s - m_new)
            l_new = l_new + jnp.sum(p, axis=-1, keepdims=True)
            acc = acc + _dot(p.astype(BF16), v)
        m_ref[...] = m_new
        l_ref[...] = l_new
        acc_ref[...] = acc

    @pl.when(step == 0)
    def _():
        m_ref[...] = jnp.full_like(m_ref, NEG_BIG)
        l_ref[...] = jnp.zeros_like(l_ref)
        acc_ref[...] = jnp.zeros_like(acc_ref)
        new_ok = (sub == 0) & (lane < n_new * heads)
        lf = jnp.where(new_ok, _log_sigmoid(ffp_ref[...] + bp_ref[...]), 0.0)
        lfo_ref[...] = lf
        gate, total = _page_suffix(lf, lane, sub, heads)
        carry_ref[...] = total
        lfc = _log_sigmoid(ffc_ref[...] + bc_ref[...])
        run = jnp.zeros((heads, 1), F32)
        for t in range(n_new - 1, -1, -1):
            rn_ref[t * heads:(t + 1) * heads, :] = run
            run = run + lfc[t * heads:(t + 1) * heads, :]
        s = _dot_nt(qb, kn_ref[...].astype(BF16)) * scale + gate[0:1, :] - rn_ref[...]
        ok = head_ok & (c_i // heads <= r_i // heads) & (c_i < n_new * heads)
        update([jnp.where(ok, s, NEG_BIG)], [vn_ref[...].astype(BF16)])

    carry = carry_ref[...]
    rn = rn_ref[...]
    scores = []
    for r in range(pages):
        gate, total = _page_suffix(lf_refs[r][...], lane, sub, heads)
        gate = gate + carry
        carry = carry + total
        s_page = _dot_nt(qb, k_refs[r][...].astype(BF16)) * scale
        scores.append([jnp.where(head_ok, s_page[:, c * LANES:(c + 1) * LANES] + gate[c:c + 1, :] - rn, NEG_BIG)
                       for c in range(SUBLANES)])
    carry_ref[...] = carry
    m_prev = m_ref[...]
    m_new = m_prev
    for page_scores in scores:
        for s in page_scores:
            m_new = jnp.maximum(m_new, jnp.max(s, axis=-1, keepdims=True))
    alpha = jnp.exp(m_prev - m_new)
    l_new = alpha * l_ref[...]
    acc = alpha * acc_ref[...]
    for r in range(pages):
        probs = [jnp.exp(s - m_new) for s in scores[r]]
        for p in probs:
            l_new = l_new + jnp.sum(p, axis=-1, keepdims=True)
        acc = acc + _dot(jnp.concatenate(probs, axis=-1).astype(BF16), v_refs[r][...].astype(BF16))
    m_ref[...] = m_new
    l_ref[...] = l_new
    acc_ref[...] = acc

    @pl.when(step == pl.num_programs(1) - 1)
    def _():
        o_ref[...] = (acc_ref[...] / l_ref[...]).astype(o_ref.dtype)


def _fox_decode(q, k_new, v_new, ff_page, bias_page, ff_col, bias_col, cache_k, cache_v, cache_lf, pages_flat,
                n_pages, *, heads, n_new):
    bs, rows, _ = q.shape
    page_rows = cache_k.shape[1]
    assert page_rows == SUBLANES * LANES and heads * (LANES // heads) == LANES and n_pages % DECODE_PAGES_PER_STEP == 0
    pages = DECODE_PAGES_PER_STEP
    n_steps = n_pages // pages

    def page_map(r):
        return lambda b, s, pt: (pt[b * n_pages + (n_pages - 1 - (s * pages + r))], 0, 0)

    in_specs = [pl.BlockSpec((None, rows, HEAD_DIM), lambda b, s, pt: (b, 0, 0)),
                pl.BlockSpec((None, LANES, HEAD_DIM), lambda b, s, pt: (b, 0, 0)),
                pl.BlockSpec((None, LANES, HEAD_DIM), lambda b, s, pt: (b, 0, 0)),
                pl.BlockSpec((None, SUBLANES, LANES), lambda b, s, pt: (b, 0, 0)),
                pl.BlockSpec((SUBLANES, LANES), lambda b, s, pt: (0, 0)),
                pl.BlockSpec((None, rows, 1), lambda b, s, pt: (b, 0, 0)),
                pl.BlockSpec((rows, 1), lambda b, s, pt: (0, 0))]
    in_specs += [pl.BlockSpec((None, page_rows, HEAD_DIM), page_map(r)) for r in range(pages)] * 2
    in_specs += [pl.BlockSpec((None, SUBLANES, LANES), page_map(r)) for r in range(pages)]
    blocks = [((page_rows, HEAD_DIM), F32)] * (2 * pages) + [((LANES, HEAD_DIM), F32)] * 4
    scratch = [((rows, 1), F32), ((rows, 1), F32), ((rows, HEAD_DIM), F32), ((1, LANES), F32), ((rows, 1), F32)]
    grid_spec = pltpu.PrefetchScalarGridSpec(
        num_scalar_prefetch=1, grid=(bs, n_steps), in_specs=in_specs,
        out_specs=[pl.BlockSpec((None, rows, HEAD_DIM), lambda b, s, pt: (b, 0, 0)),
                   pl.BlockSpec((None, SUBLANES, LANES), lambda b, s, pt: (b, 0, 0))],
        scratch_shapes=[pltpu.VMEM(s, t) for s, t in scratch])
    return pl.pallas_call(
        functools.partial(_fox_decode_kernel, pages=pages, heads=heads, n_new=n_new, scale=HEAD_DIM ** -0.5),
        grid_spec=grid_spec,
        out_shape=[jax.ShapeDtypeStruct((bs, rows, HEAD_DIM), BF16),
                   jax.ShapeDtypeStruct((bs, SUBLANES, LANES), F32)],
        compiler_params=_params(("parallel", "arbitrary"), blocks, scratch),
        name="fox_decode")(pages_flat, q, k_new, v_new, ff_page, bias_page, ff_col, bias_col,
                           *([cache_k] * pages), *([cache_v] * pages), *([cache_lf] * pages))


def _hi_lo(x):
    hi = x.astype(BF16)
    return hi, (x - hi.astype(F32)).astype(BF16)


def _hi_lo_lhs(a):
    hi = a.astype(BF16).astype(F32)
    a16 = jnp.concatenate([hi, a - hi], axis=1).astype(BF16)
    return jnp.concatenate([a16, a16], axis=1)


def _hi_lo_rhs(b):
    hi, lo = _hi_lo(b)
    return jnp.concatenate([hi, hi, lo, lo], axis=0)


def _unit_lower_solves(ms, rhss):
    c = ms[0].shape[0]
    assert 2 * c == LANES
    xs, ps, power = list(rhss), list(ms), 1
    while True:
        lhs = [_hi_lo_lhs(p) for p in ps]
        upd = [_dot(a, _hi_lo_rhs(x)) for a, x in zip(lhs, xs)]
        xs = [x - u for x, u in zip(xs, upd)] if power == 1 else [x + u for x, u in zip(xs, upd)]
        if 2 * power >= c:
            return xs
        ps = [_dot(a, _hi_lo_rhs(p)) for a, p in zip(lhs, ps)]
        power *= 2


def _gdn_prep_kernel(qkv_ref, ab_ref, cw_ref, al_ref, dt_ref, cb0_ref,
                     u_ref, w_ref, qe_ref, kdt_ref, attn_ref, egl_ref, cb_out_ref, xe_ref, *, heads, t_valid):
    c = pl.program_id(1)
    ch = qkv_ref.shape[0]
    width = heads * HEAD_DIM
    tail = CONV_W - 1
    base = SUBLANES
    last_chunk = (t_valid - 1) // ch
    last_len = t_valid - last_chunk * ch

    @pl.when(c == 0)
    def _():
        xe_ref[base - tail:base, :] = cb0_ref[...]

    xe_ref[base:base + ch, :] = qkv_ref[...]
    y = xe_ref[base - tail:base - tail + ch, :] * cw_ref[0:1, :]
    for i in range(1, CONV_W):
        y = y + xe_ref[base - tail + i:base - tail + i + ch, :] * cw_ref[i:i + 1, :]
    y = _silu(y)

    @pl.when(c == last_chunk)
    def _():
        cb_out_ref[...] = xe_ref[base + last_len - tail:base + last_len, :]

    xe_ref[base - tail:base, :] = xe_ref[base + ch - tail:base + ch, :]

    valid = (c * ch + lax.broadcasted_iota(jnp.int32, (ch, 1), 0)) < t_valid
    ab = ab_ref[...]
    g_all = jnp.where(valid, -jnp.exp(al_ref[...]) * _softplus(ab + dt_ref[...]), 0.0)
    beta_all = jnp.where(valid, jax.nn.sigmoid(ab), 0.0)
    cum = _dot01(_tril01(ch).astype(BF16), g_all)
    cum_t = cum.T
    lower = _tril01(ch)
    strict = _tril01(ch, strict=True)
    egl_ref[...] = jnp.exp(cum[ch - 1:ch, :])

    hr = range(heads)
    sl = lambda part, h: slice(part * width + h * HEAD_DIM, part * width + (h + 1) * HEAD_DIM)
    gc = [cum[:, h:h + 1] for h in hr]
    beta = [beta_all[:, heads + h:heads + h + 1] for h in hr]
    decay = [jnp.exp(jnp.where(lower, gc[h] - cum_t[h:h + 1, :], NEG_BIG)) for h in hr]
    q = [y[:, sl(0, h)] for h in hr]
    k = [y[:, sl(1, h)] for h in hr]
    q = [x * lax.rsqrt(jnp.sum(x * x, axis=-1, keepdims=True) + EPS) * (HEAD_DIM ** -0.5) for x in q]
    k = [x * lax.rsqrt(jnp.sum(x * x, axis=-1, keepdims=True) + EPS) for x in k]
    kb = [k[h] * beta[h] for h in hr]
    k16 = [x.astype(BF16) for x in k]
    eg = [jnp.exp(gc[h]) for h in hr]
    sols = _unit_lower_solves(
        [jnp.where(strict, _dot_nt(kb[h].astype(BF16), k16[h]) * decay[h], 0.0) for h in hr],
        [jnp.concatenate([y[:, sl(2, h)] * beta[h], kb[h] * eg[h]], axis=-1) for h in hr])
    for h in hr:
        sol = sols[h]
        u_ref[:, sl(0, h)] = sol[:, :HEAD_DIM]
        w_ref[:, sl(0, h)] = sol[:, HEAD_DIM:].astype(BF16)
        qe_ref[:, sl(0, h)] = (q[h] * eg[h]).astype(BF16)
        attn_ref[h] = (_dot_nt(q[h].astype(BF16), k16[h]) * decay[h]).astype(BF16)
        kdt_ref[h] = (k[h] * jnp.exp(cum[ch - 1:ch, h:h + 1] - gc[h])).T.astype(BF16)


def _gdn_scan_kernel(u_ref, w_ref, qe_ref, kdt_ref, attn_ref, egl_ref, gz_ref, gn_ref, s0_ref,
                     o_ref, s_out_ref, st_ref, *, heads):
    c = pl.program_id(0)
    bx = u_ref.shape[0]

    @pl.when(c == 0)
    def _():
        st_ref[...] = s0_ref[...]

    chains = [(b, h) for b in range(bx) for h in range(heads)]
    hs = lambda h: slice(h * HEAD_DIM, (h + 1) * HEAD_DIM)
    s_old = [st_ref[b, h] for b, h in chains]
    s16 = [s.astype(BF16) for s in s_old]
    ws = [_dot(w_ref[b, :, hs(h)], s) for (b, h), s in zip(chains, s16)]
    qs = [_dot(qe_ref[b, :, hs(h)], s) for (b, h), s in zip(chains, s16)]
    vn16 = [(u_ref[b, :, hs(h)] - x).astype(BF16) for (b, h), x in zip(chains, ws)]
    for i, (b, h) in enumerate(chains):
        o = qs[i] + _dot(attn_ref[b, h], vn16[i])
        st_ref[b, h] = s_old[i] * egl_ref[b, :, h:h + 1] + _dot(kdt_ref[b, h], vn16[i])
        on = (o * lax.rsqrt(jnp.mean(o * o, axis=-1, keepdims=True) + EPS)) * gn_ref[...]
        o_ref[b, :, hs(h)] = (on * _silu(gz_ref[b, :, hs(h)])).astype(o_ref.dtype)

    @pl.when(c == pl.num_programs(0) - 1)
    def _():
        s_out_ref[...] = st_ref[...]


def _gdn(proj, narrow, lay, conv_w, a_log, dt_bias, gnorm, s0, cb0, layer, bx, tx, t_valid):
    n, wp = proj.shape
    gw = lay["gz"][1]
    heads = gw // HEAD_DIM
    ch = CHUNK
    nc = tx // ch
    qkv_c, ab_c, gz_c = lay["gqkv"][0] // (3 * gw), lay["gab"][0] // LANES, lay["gz"][0] // gw
    blocks = [((ch, 3 * gw), F32), ((ch, LANES), F32), ((CONV_W, 3 * gw), F32), ((CONV_W - 1, 3 * gw), F32),
              ((ch, gw), F32), ((ch, gw), BF16), ((ch, gw), BF16), ((heads, HEAD_DIM, LANES), BF16),
              ((heads, ch, LANES), BF16), ((CONV_W - 1, 3 * gw), F32)]
    scratch = [((SUBLANES + ch, 3 * gw), F32)]
    u, w16, qe16, kdt, attn, egl, cb_out = pl.pallas_call(
        functools.partial(_gdn_prep_kernel, heads=heads, t_valid=t_valid),
        grid=(bx, nc),
        in_specs=[pl.BlockSpec((ch, 3 * gw), lambda b, c: (b * nc + c, qkv_c)),
                  pl.BlockSpec((ch, LANES), lambda b, c: (b * nc + c, ab_c)),
                  pl.BlockSpec((None, CONV_W, 3 * gw), lambda b, c: (layer, 0, 0)),
                  pl.BlockSpec((None, 1, LANES), lambda b, c: (layer, 0, 0)),
                  pl.BlockSpec((None, 1, LANES), lambda b, c: (layer, 0, 0)),
                  pl.BlockSpec((None, CONV_W - 1, 3 * gw), lambda b, c: (b, 0, 0))],
        out_specs=[pl.BlockSpec((ch, gw), lambda b, c: (b * nc + c, 0)),
                   pl.BlockSpec((ch, gw), lambda b, c: (b * nc + c, 0)),
                   pl.BlockSpec((ch, gw), lambda b, c: (b * nc + c, 0)),
                   pl.BlockSpec((None, None, heads, HEAD_DIM, ch), lambda b, c: (b, c, 0, 0, 0)),
                   pl.BlockSpec((None, None, heads, ch, ch), lambda b, c: (b, c, 0, 0, 0)),
                   pl.BlockSpec((None, None, 1, LANES), lambda b, c: (b, c, 0, 0)),
                   pl.BlockSpec((None, CONV_W - 1, 3 * gw), lambda b, c: (b, 0, 0))],
        out_shape=[jax.ShapeDtypeStruct((n, gw), F32), jax.ShapeDtypeStruct((n, gw), BF16),
                   jax.ShapeDtypeStruct((n, gw), BF16),
                   jax.ShapeDtypeStruct((bx, nc, heads, HEAD_DIM, ch), BF16),
                   jax.ShapeDtypeStruct((bx, nc, heads, ch, ch), BF16),
                   jax.ShapeDtypeStruct((bx, nc, 1, LANES), F32),
                   jax.ShapeDtypeStruct((bx, CONV_W - 1, 3 * gw), F32)],
        scratch_shapes=[pltpu.VMEM(s, t) for s, t in scratch],
        compiler_params=_params(("parallel", "arbitrary"), blocks, scratch),
        name="gdn_prep")(proj, narrow, conv_w, a_log, dt_bias, cb0)

    state = (bx, heads, HEAD_DIM, HEAD_DIM)
    blocks = [((bx, ch, gw), F32), ((bx, ch, gw), BF16), ((bx, ch, gw), BF16), ((bx, heads, HEAD_DIM, LANES), BF16),
              ((bx, heads, ch, LANES), BF16), ((bx, ch, gw), F32), (state, F32), ((bx, ch, gw), BF16), (state, F32)]
    scratch = [(state, F32)]
    rows3 = lambda a: a.reshape(bx, tx, a.shape[-1])
    o, s_out = pl.pallas_call(
        functools.partial(_gdn_scan_kernel, heads=heads),
        grid=(nc,),
        in_specs=[pl.BlockSpec((bx, ch, gw), lambda c: (0, c, 0)),
                  pl.BlockSpec((bx, ch, gw), lambda c: (0, c, 0)),
                  pl.BlockSpec((bx, ch, gw), lambda c: (0, c, 0)),
                  pl.BlockSpec((bx, None, heads, HEAD_DIM, ch), lambda c: (0, c, 0, 0, 0)),
                  pl.BlockSpec((bx, None, heads, ch, ch), lambda c: (0, c, 0, 0, 0)),
                  pl.BlockSpec((bx, None, 1, LANES), lambda c: (0, c, 0, 0)),
                  pl.BlockSpec((bx, ch, gw), lambda c: (0, c, gz_c)),
                  pl.BlockSpec((None, 1, HEAD_DIM), lambda c: (layer, 0, 0)),
                  pl.BlockSpec(state, lambda c: (0, 0, 0, 0))],
        out_specs=[pl.BlockSpec((bx, ch, gw), lambda c: (0, c, 0)),
                   pl.BlockSpec(state, lambda c: (0, 0, 0, 0))],
        out_shape=[jax.ShapeDtypeStruct((bx, tx, gw), BF16), jax.ShapeDtypeStruct(state, F32)],
        scratch_shapes=[pltpu.VMEM(s, t) for s, t in scratch],
        compiler_params=_params(("arbitrary",), blocks, scratch),
        name="gdn_scan")(rows3(u), rows3(w16), rows3(qe16), kdt, attn, egl, rows3(proj), gnorm, s0)
    return o.reshape(n, gw), s_out, cb_out


def _hgrn_kernel(hq_ref, hf_ref, hi_ref, hg_ref, lb_ref, hn_ref, s0_ref, o_ref, s_out_ref, st_ref,
                 *, heads, t_valid):
    c = pl.program_id(1)
    ch = hq_ref.shape[0]
    nblk = ch // SUBLANES

    @pl.when(c == 0)
    def _():
        for h in range(heads):
            st_ref[h] = s0_ref[h].T

    tok = lax.broadcasted_iota(jnp.int32, (ch, 1), 0)
    valid = (c * ch + tok) < t_valid
    lb = lb_ref[...]
    hf = hf_ref[...]
    a = jnp.log(jnp.maximum(lb, LB_FLOOR))
    b = jnp.log1p(-lb) + _log_sigmoid(hf)
    log_f = jnp.maximum(a, b) + jnp.log1p(jnp.exp(-jnp.abs(a - b)))
    log_f = jnp.where(valid, log_f, 0.0)
    k_all = jnp.where(valid, (1.0 - lb) * jax.nn.sigmoid(-hf), 0.0)
    cum_all = _dot01(_tril01(ch).astype(BF16), log_f)
    q_all = _silu(hq_ref[...])

    ti = lax.broadcasted_iota(jnp.int32, (ch, ch), 0)
    tj = lax.broadcasted_iota(jnp.int32, (ch, ch), 1)
    sub3 = lax.broadcasted_iota(jnp.int32, (nblk, SUBLANES, HEAD_DIM), 1)

    hr = range(heads)
    hsl = [slice(h * HEAD_DIM, (h + 1) * HEAD_DIM) for h in hr]
    g = [cum_all[:, s] for s in hsl]
    q = [q_all[:, s] for s in hsl]
    k = [k_all[:, s] for s in hsl]
    v16 = [hi_ref[:, s].astype(BF16) for s in hsl]
    attn_t = [jnp.zeros((ch, ch), F32) for _ in hr]
    half = ch // 2
    while half >= SUBLANES:
        blk = 2 * half
        upper = (tok % blk) >= half
        same = (ti // blk) == (tj // blk)
        es = []
        for h in hr:
            g3 = g[h].reshape(ch // blk, blk, HEAD_DIM)
            es.append(jnp.exp(-jnp.abs(g3 - g3[:, half - 1:half, :])).reshape(ch, HEAD_DIM))
        prods = [_dot_nt(jnp.where(upper, 0.0, k[h] * es[h]).astype(BF16),
                         jnp.where(upper, q[h] * es[h], 0.0).astype(BF16)) for h in hr]
        attn_t = [a + jnp.where(same, p, 0.0) for a, p in zip(attn_t, prods)]
        half //= 2
    diag = (ti // SUBLANES) == (tj // SUBLANES)
    for h in hr:
        g3, q3, k3 = (z.reshape(nblk, SUBLANES, HEAD_DIM) for z in (g[h], q[h], k[h]))
        cols = []
        for i in range(SUBLANES):
            e = jnp.exp(jnp.where(sub3 <= i, g3[:, i:i + 1, :] - g3, NEG_BIG))
            w = jnp.sum(e * k3 * q3[:, i:i + 1, :], axis=-1, keepdims=True).reshape(ch, 1)
            cols.append(jnp.where((tj % SUBLANES) == i, w, 0.0))
        while len(cols) > 1:
            cols = [a + b for a, b in zip(cols[0::2], cols[1::2])]
        attn_t[h] = attn_t[h] + jnp.where(diag, cols[0], 0.0)
    st = [st_ref[h] for h in hr]
    o = [_dot_nt((q[h] * jnp.exp(g[h])).astype(BF16), st[h].astype(BF16)) + _dot(attn_t[h].T.astype(BF16), v16[h])
         for h in hr]
    for h in hr:
        gl = g[h][ch - 1:ch, :]
        kd = k[h] * jnp.exp(gl - g[h])
        st_ref[h] = st[h] * jnp.exp(gl) + _dot(hi_ref[:, hsl[h]].T.astype(BF16), kd.astype(BF16))
        on = (o[h] * lax.rsqrt(jnp.mean(o[h] * o[h], axis=-1, keepdims=True) + EPS)) * hn_ref[...]
        o_ref[:, hsl[h]] = (on * _silu(hg_ref[:, hsl[h]])).astype(o_ref.dtype)

    @pl.when(c == pl.num_programs(1) - 1)
    def _():
        for h in range(heads):
            s_out_ref[h] = st_ref[h].T


def _hgrn(proj, lay, lb, hnorm, s0, layer, bx, tx, t_valid):
    n = proj.shape[0]
    hw = lay["hq"][1]
    heads = hw // HEAD_DIM
    ch = CHUNK
    nc = tx // ch
    cols = [lay[k][0] // hw for k in ("hq", "hf", "hi", "hg")]
    blocks = [((ch, hw), F32)] * 4 + [((heads, HEAD_DIM, HEAD_DIM), F32)] * 2 + [((ch, hw), BF16)]
    scratch = [((heads, HEAD_DIM, HEAD_DIM), F32)]
    in_specs = [pl.BlockSpec((ch, hw), functools.partial(lambda b, c, col: (b * nc + c, col), col=col))
                for col in cols]
    in_specs += [pl.BlockSpec((None, 1, hw), lambda b, c: (layer, 0, 0)),
                 pl.BlockSpec((None, 1, HEAD_DIM), lambda b, c: (layer, 0, 0)),
                 pl.BlockSpec((None, heads, HEAD_DIM, HEAD_DIM), lambda b, c: (b, 0, 0, 0))]
    return pl.pallas_call(
        functools.partial(_hgrn_kernel, heads=heads, t_valid=t_valid),
        grid=(bx, nc),
        in_specs=in_specs,
        out_specs=[pl.BlockSpec((ch, hw), lambda b, c: (b * nc + c, 0)),
                   pl.BlockSpec((None, heads, HEAD_DIM, HEAD_DIM), lambda b, c: (b, 0, 0, 0))],
        out_shape=[jax.ShapeDtypeStruct((n, hw), BF16),
                   jax.ShapeDtypeStruct((bx, heads, HEAD_DIM, HEAD_DIM), F32)],
        scratch_shapes=[pltpu.VMEM(s, t) for s, t in scratch],
        compiler_params=_params(("parallel", "arbitrary"), blocks, scratch),
        name="hgrn2")(proj, proj, proj, proj, lb, hnorm, s0)


def _layout(d_model, fw, gw, hw):
    order = [("gate_a", d_model), ("gate_b", d_model), ("gate_c", d_model), ("fq", fw), ("fk", fw), ("fv", fw),
             ("gqkv", 3 * gw), ("gz", gw), ("hq", hw), ("hf", hw), ("hi", hw), ("hg", hw)]
    lay, start = {}, 0
    for name, width in order:
        lay[name] = (start, width)
        start += width
    lay["total"] = (0, start)
    lay["ff"], lay["gab"] = (0, LANES), (LANES, LANES)
    return lay


def _source_columns(d_model, fh, gh, fw, gw, hw):
    splits = [("fq", fw), ("fk", fw), ("fv", fw), ("ff", fh), ("gq", gw), ("gk", gw), ("gv", gw), ("ga", gh),
              ("gb", gh), ("gz", gw), ("hq", hw), ("hf", hw), ("hi", hw), ("hg", hw),
              ("gate_a", d_model), ("gate_b", d_model), ("gate_c", d_model)]
    src, start = {}, 0
    for name, width in splits:
        src[name] = start
        start += width
    src["end"] = start
    return src


def _in_proj_row_starts(lay, src, tn):
    starts = []
    for first, dst, width in (("gate_a", "gate_a", 3 * lay["gate_a"][1]), ("fq", "fq", 3 * lay["fq"][1]),
                              ("gq", "gqkv", lay["gqkv"][1]), ("gz", "gz", lay["total"][1] - lay["gz"][0])):
        assert lay[dst][0] == len(starts) * tn and width % tn == 0
        starts += [src[first] + k * tn for k in range(width // tn)]
    assert len(starts) * tn == lay["total"][1]
    return starts


def _pad_lanes(x, fill=0.0):
    return jnp.pad(x.astype(F32), ((0, 0), (0, LANES - x.shape[-1])), constant_values=fill)[:, None, :]


def _tile(n, candidates):
    for c in candidates:
        if n % c == 0:
            return c
    raise ValueError(f"no tile for {n} among {candidates}")


def kernel(x_prompt, x_sample, cache_fox_k, cache_fox_v, cache_fox_logf, state_gdn, state_gdn_conv, state_hgrn,
           page_table, norm_mix_pre, norm_mix_post, norm_mlp_pre, norm_mlp_post, w_in, fox_f_bias, gdn_conv_w,
           gdn_a_log, gdn_dt_bias, gdn_norm, hgrn_lower_bounds, hgrn_norm, w_fox_o, w_gdn_o, w_hgrn_o, w_out,
           w_up, w_down):
    bp, tp, d = x_prompt.shape
    bs, ts, _ = x_sample.shape
    depth = w_in.shape[0]
    fh, gh = fox_f_bias.shape[1], gdn_a_log.shape[1]
    fw, gw, hw = fh * HEAD_DIM, gh * HEAD_DIM, hgrn_lower_bounds.shape[1]
    hh = hw // HEAD_DIM
    n_pool, page = cache_fox_k.shape[1], cache_fox_k.shape[2]
    n_pages = page_table.shape[1]
    tsp = CHUNK
    tsd = SAMPLE_DENSE_ROWS
    assert tp % CHUNK == 0 and tp % LANES == 0 and ts <= tsd <= tsp and ts >= CONV_W - 1
    assert page * fh == SUBLANES * LANES and 2 * gh <= LANES and fh <= LANES
    lay = _layout(d, fw, gw, hw)
    wp = lay["total"][1]
    src = _source_columns(d, fh, gh, fw, gw, hw)
    assert src["end"] == w_in.shape[-1] and src["gb"] == src["ga"] + gh
    row_starts = _in_proj_row_starts(lay, src, IN_PROJ_COL_TILE)

    w_in_t = jnp.swapaxes(w_in.astype(F32), 1, 2)
    w_u = w_up.astype(F32)
    w_fo, w_go, w_ho, w_o, w_d = (w.astype(BF16) for w in (w_fox_o, w_gdn_o, w_hgrn_o, w_out, w_down))
    g_mix_pre, g_mix_post, g_mlp_pre, g_mlp_post = (
        g.astype(F32)[:, None, :] for g in (norm_mix_pre, norm_mix_post, norm_mlp_pre, norm_mlp_post))
    f_bias = _pad_lanes(fox_f_bias)
    a_log, dt_bias = _pad_lanes(gdn_a_log), _pad_lanes(gdn_dt_bias)
    g_norm, h_norm = gdn_norm.astype(F32)[:, None, :], hgrn_norm.astype(F32)[:, None, :]
    conv_w = gdn_conv_w.astype(F32)
    lb_all = _lower_bounds(hgrn_lower_bounds)[:, None, :]

    ck = cache_fox_k.reshape(depth * n_pool, page * fh, HEAD_DIM)
    cv = cache_fox_v.reshape(depth * n_pool, page * fh, HEAD_DIM)
    clf = cache_fox_logf.astype(F32).reshape(depth * n_pool, SUBLANES, LANES)
    bias_page = jnp.tile(fox_f_bias.astype(F32), (1, LANES // fh))[:, None, :] * jnp.ones((1, SUBLANES, 1), F32)
    bias_col = jnp.tile(fox_f_bias.astype(F32), (1, ts))[:, :, None]

    zeros_state_p = jnp.zeros((bp, gh, HEAD_DIM, HEAD_DIM), F32)
    zeros_hstate_p = jnp.zeros((bp, hh, HEAD_DIM, HEAD_DIM), F32)
    zeros_conv_p = jnp.zeros((bp, CONV_W - 1, 3 * gw), F32)

    xp = x_prompt.reshape(bp * tp, d)
    xs = jnp.pad(x_sample, ((0, 0), (0, tsd - ts), (0, 0))).reshape(bs * tsd, d)

    tm_p = _tile(bp * tp, (1024, 512, 256, 128, 64))
    tm_s = _tile(bs * tsd, (512, 256, 128, 64, 32, 16))
    tq = _tile(tp, FOX_Q_TILES)

    def in_proj(xn, l, tm):
        proj = _in_proj(xn, w_in_t, row_starts, l, tm=tm, tn=IN_PROJ_COL_TILE)
        narrow = _in_proj_narrow(xn, w_in_t, src["ff"], fh, src["ga"], 2 * gh, l, tm=tm)
        return proj, narrow

    def dense_tail(x, oa, ob, oc, proj, l, tm):
        merged = _merge(oa, ob, oc, proj, lay, w_fo, w_go, w_ho, l, tm=min(tm, MERGE_ROW_TILE), tn=d)
        x, xn = _matmul_norm_res(merged, w_o, x, g_mix_post, l, g_mlp_pre, l, tm=min(tm, RESIDUAL_ROW_TILE),
                                 tk=_tile(d, RESIDUAL_K_TILES), name="out_proj_norm_res")
        u = _mlp_up(xn, w_u, l, tm=tm, tn=_tile(w_u.shape[-1], (1024, 512, 256)))
        return _matmul_norm_res(u, w_d, x, g_mlp_post, l, g_mix_pre, (l + 1) % depth,
                                tm=min(tm, RESIDUAL_ROW_TILE), tk=_tile(w_d.shape[1], RESIDUAL_K_TILES),
                                name="mlp_down_norm_res")

    xpn = _rmsnorm_cast(xp, g_mix_pre, 0, tm=tm_p)
    xsn = _rmsnorm_cast(xs, g_mix_pre, 0, tm=tm_s)
    outs = {k: [] for k in ("fkp", "fvp", "flp", "fks", "fvs", "fls", "gsp", "gss", "gcp", "gcs", "hsp", "hss")}
    for l in range(depth):
        proj, narrow = in_proj(xpn, l, tm_p)
        lf, c_tok, c_row = _fox_gate(narrow, lay, f_bias, l, bp, tp)
        oa = _fox_attn(proj, lay, c_tok, c_row, bp, tp, tq=tq)
        ob, gs, gc = _gdn(proj, narrow, lay, conv_w, a_log, dt_bias, g_norm, zeros_state_p, zeros_conv_p, l, bp, tp,
                          tp)
        oc, hs = _hgrn(proj, lay, lb_all, h_norm, zeros_hstate_p, l, bp, tp, tp)
        xp, xpn = dense_tail(xp, oa, ob, oc, proj, l, tm_p)
        outs["fkp"].append(proj[:, lay["fk"][0]:lay["fk"][0] + fw].reshape(bp, tp, fh, HEAD_DIM))
        outs["fvp"].append(proj[:, lay["fv"][0]:lay["fv"][0] + fw].reshape(bp, tp, fh, HEAD_DIM))
        outs["flp"].append(lf[:, :fh].reshape(bp, tp, fh))
        outs["gsp"].append(gs)
        outs["gcp"].append(gc)
        outs["hsp"].append(hs)

        proj, narrow = in_proj(xsn, l, tm_s)
        p3 = proj.reshape(bs, tsd, wp)[:, :ts]
        q_new, k_new, v_new = (p3[:, :, lay[k][0]:lay[k][0] + fw] for k in ("fq", "fk", "fv"))
        ff_new = narrow.reshape(bs, tsd, 2 * LANES)[:, :ts, lay["ff"][0]:lay["ff"][0] + fh]
        to_chunk = lambda z: jnp.pad(z.reshape(bs, tsd, z.shape[-1]),
                                     ((0, 0), (0, tsp - tsd), (0, 0))).reshape(bs * tsp, z.shape[-1])
        from_chunk = lambda z: z.reshape(bs, tsp, z.shape[-1])[:, :tsd].reshape(bs * tsd, z.shape[-1])
        proj_c, narrow_c = to_chunk(proj), to_chunk(narrow)
        rows = ts * fh
        pad_rows = lambda z: jnp.pad(z.reshape(bs, rows, HEAD_DIM), ((0, 0), (0, LANES - rows), (0, 0)))
        ff_page = jnp.pad(ff_new.reshape(bs, 1, rows), ((0, 0), (0, SUBLANES - 1), (0, LANES - rows)))
        oa_s, lf_page = _fox_decode(
            q_new.reshape(bs, rows, HEAD_DIM), pad_rows(k_new), pad_rows(v_new), ff_page, bias_page[l],
            ff_new.reshape(bs, rows, 1), bias_col[l], ck, cv, clf,
            (page_table.astype(jnp.int32) + l * n_pool).reshape(-1), n_pages, heads=fh, n_new=ts)
        oa = jnp.pad(oa_s.reshape(bs, ts, fw), ((0, 0), (0, tsd - ts), (0, 0))).reshape(bs * tsd, fw)
        ob, gs, gc = _gdn(proj_c, narrow_c, lay, conv_w, a_log, dt_bias, g_norm, state_gdn[l].astype(F32),
                          state_gdn_conv[l].astype(F32), l, bs, tsp, ts)
        oc, hs = _hgrn(proj_c, lay, lb_all, h_norm, state_hgrn[l].astype(F32), l, bs, tsp, ts)
        xs, xsn = dense_tail(xs, oa, from_chunk(ob), from_chunk(oc), proj, l, tm_s)
        outs["fks"].append(k_new.reshape(bs, ts, fh, HEAD_DIM))
        outs["fvs"].append(v_new.reshape(bs, ts, fh, HEAD_DIM))
        outs["fls"].append(lf_page[:, 0, :rows].reshape(bs, ts, fh))
        outs["gss"].append(gs)
        outs["gcs"].append(gc)
        outs["hss"].append(hs)

    st = lambda k: jnp.stack(outs[k])
    return (xp.reshape(bp, tp, d), xs.reshape(bs, tsd, d)[:, :ts],
            st("fkp"), st("fvp"), st("flp"), st("fks"), st("fvs"), st("fls"),
            st("gsp"), st("gss"), st("gcp"), st("gcs"), st("hsp"), st("hss"))
```

```python
import functools

import jax
import jax.numpy as jnp
from jax import lax
from jax.experimental import pallas as pl
from jax.experimental.pallas import tpu as pltpu

F32, BF16 = jnp.float32, jnp.bfloat16
HEAD_DIM = 128
CONV_W = 4
CHUNK = 64
EPS = 1e-6
NEG_BIG = -1e30
LB_FLOOR = 1e-30
LANES = 128
SUBLANES = 8
V7X_VMEM_BYTES = 64 * 2**20
VMEM_RESERVE_BYTES = 6 * 2**20
KERNEL_TEMP_BYTES = 16 * 2**20
DECODE_PAGES_PER_STEP = 16
FOX_Q_TILES = (1024, 512, 256, 128)
IN_PROJ_COL_TILE = 1024
MERGE_ROW_TILE = 512
RESIDUAL_ROW_TILE = 512
RESIDUAL_K_TILES = (2048, 1024, 512, 256)
RESIDUAL_EPILOGUE_GROUPS = 4
SAMPLE_DENSE_ROWS = 16


def _nbytes(shape, dtype):
    n = 1
    for s in shape:
        n *= s
    return n * jnp.dtype(dtype).itemsize


def _params(semantics, blocks, scratch=()):
    need = 2 * sum(_nbytes(s, d) for s, d in blocks) + sum(_nbytes(s, d) for s, d in scratch) + KERNEL_TEMP_BYTES
    return pltpu.CompilerParams(dimension_semantics=semantics,
                                vmem_limit_bytes=min(need, V7X_VMEM_BYTES - VMEM_RESERVE_BYTES))


def _dot(a, b):
    return jnp.dot(a, b, preferred_element_type=F32)


def _dot_nt(a, b):
    return lax.dot_general(a, b, (((1,), (1,)), ((), ())), preferred_element_type=F32)


def _split3(x):
    x1 = x.astype(BF16)
    r = x - x1.astype(F32)
    x2 = r.astype(BF16)
    x3 = (r - x2.astype(F32)).astype(BF16)
    return x1, x2, x3


def _dot01(m01, x):
    x1, x2, x3 = _split3(x)
    return _dot(m01, x1) + (_dot(m01, x2) + _dot(m01, x3))


def _log_sigmoid(x):
    return -(jnp.maximum(-x, 0.0) + jnp.log1p(jnp.exp(-jnp.abs(x))))


def _softplus(x):
    return jnp.maximum(x, 0.0) + jnp.log1p(jnp.exp(-jnp.abs(x)))


def _silu(x):
    return x * jax.nn.sigmoid(x)


def _tril01(n, strict=False):
    r = lax.broadcasted_iota(jnp.int32, (n, n), 0)
    c = lax.broadcasted_iota(jnp.int32, (n, n), 1)
    return (r > c) if strict else (r >= c)


def _lower_bounds_kernel(x_ref, o_ref):
    x = x_ref[...]
    e = jnp.exp(x - jnp.max(x, axis=0, keepdims=True))
    p = e / jnp.sum(e, axis=0, keepdims=True)
    run = jnp.zeros_like(p[0:1])
    for l in range(x.shape[0]):
        run = run + p[l:l + 1]
        o_ref[l:l + 1, :] = run - p[0:1]


def _lower_bounds(x):
    return pl.pallas_call(_lower_bounds_kernel, out_shape=jax.ShapeDtypeStruct(x.shape, F32),
                          name="hgrn_lower_bounds")(x.astype(F32))


def _rmsnorm_bf16(x, g):
    var = jnp.mean(x * x, axis=-1, keepdims=True)
    return ((x * lax.rsqrt(var + EPS)) * g).astype(BF16)


def _rmsnorm_cast_kernel(x_ref, g_ref, o_ref):
    o_ref[...] = _rmsnorm_bf16(x_ref[...], g_ref[...])


def _rmsnorm_cast(x, g, layer, *, tm):
    n, d = x.shape
    blocks = [((tm, d), F32), ((1, d), F32), ((tm, d), BF16)]
    return pl.pallas_call(
        _rmsnorm_cast_kernel,
        grid=(n // tm,),
        in_specs=[pl.BlockSpec((tm, d), lambda i: (i, 0)), pl.BlockSpec((None, 1, d), lambda i: (layer, 0, 0))],
        out_specs=pl.BlockSpec((tm, d), lambda i: (i, 0)),
        out_shape=jax.ShapeDtypeStruct((n, d), BF16),
        compiler_params=_params(("parallel",), blocks),
        name="rmsnorm_cast")(x, g)


def _in_proj_kernel(offs_ref, xn_ref, w_ref, o_ref, w16_ref):
    del offs_ref

    @pl.when(pl.program_id(1) == 0)
    def _():
        w16_ref[...] = w_ref[0].astype(BF16)

    o_ref[...] = _dot_nt(xn_ref[...], w16_ref[...])


def _in_proj(xn, w_t, row_starts, layer, *, tm, tn):
    n, d = xn.shape
    assert all(s % SUBLANES == 0 for s in row_starts)
    offs = jnp.asarray([s // SUBLANES for s in row_starts], jnp.int32)
    blocks = [((tm, d), BF16), ((tn, d), F32), ((tm, tn), F32)]
    scratch = [((tn, d), BF16)]
    grid_spec = pltpu.PrefetchScalarGridSpec(
        num_scalar_prefetch=1, grid=(len(row_starts), n // tm),
        in_specs=[pl.BlockSpec((tm, d), lambda j, i, offs: (i, 0)),
                  pl.BlockSpec((pl.Element(1), pl.Element(tn), pl.Element(d)),
                               lambda j, i, offs: (layer, offs[j] * SUBLANES, 0))],
        out_specs=pl.BlockSpec((tm, tn), lambda j, i, offs: (i, j)),
        scratch_shapes=[pltpu.VMEM(s, t) for s, t in scratch])
    return pl.pallas_call(
        _in_proj_kernel, grid_spec=grid_spec,
        out_shape=jax.ShapeDtypeStruct((n, len(row_starts) * tn), F32),
        compiler_params=_params(("arbitrary", "arbitrary"), blocks, scratch),
        name="in_proj")(offs, xn, w_t)


def _in_proj_narrow_kernel(xn_ref, a_ref, b_ref, o_ref, w16_ref):
    @pl.when(pl.program_id(0) == 0)
    def _():
        pad = lambda w: jnp.zeros((LANES - w.shape[0], w.shape[1]), F32)
        a, b = a_ref[0], b_ref[0]
        w16_ref[...] = jnp.concatenate([a, pad(a), b, pad(b)], axis=0).astype(BF16)

    o_ref[...] = _dot_nt(xn_ref[...], w16_ref[...])


def _in_proj_narrow(xn, w_t, a_start, a_rows, b_start, b_rows, layer, *, tm):
    n, d = xn.shape
    assert a_start % SUBLANES == 0 and b_start % SUBLANES == 0 and a_rows % SUBLANES == 0 and b_rows % SUBLANES == 0
    element = lambda rows, start: pl.BlockSpec((pl.Element(1), pl.Element(rows), pl.Element(d)),
                                               lambda i: (layer, start, 0))
    blocks = [((tm, d), BF16), ((a_rows, d), F32), ((b_rows, d), F32), ((tm, 2 * LANES), F32)]
    scratch = [((2 * LANES, d), BF16)]
    return pl.pallas_call(
        _in_proj_narrow_kernel,
        grid=(n // tm,),
        in_specs=[pl.BlockSpec((tm, d), lambda i: (i, 0)), element(a_rows, a_start), element(b_rows, b_start)],
        out_specs=pl.BlockSpec((tm, 2 * LANES), lambda i: (i, 0)),
        out_shape=jax.ShapeDtypeStruct((n, 2 * LANES), F32),
        scratch_shapes=[pltpu.VMEM(s, t) for s, t in scratch],
        compiler_params=_params(("arbitrary",), blocks, scratch),
        name="in_proj_narrow")(xn, w_t, w_t)


def _mlp_up_kernel(xn_ref, w_ref, o_ref, w16_ref):
    @pl.when(pl.program_id(1) == 0)
    def _():
        w16_ref[...] = w_ref[...].astype(BF16)

    y = jnp.maximum(_dot(xn_ref[...], w16_ref[...]), 0.0)
    o_ref[...] = (y * y).astype(o_ref.dtype)


def _mlp_up(xn, w, layer, *, tm, tn):
    n, d = xn.shape
    f = w.shape[-1]
    blocks = [((tm, d), BF16), ((d, tn), F32), ((tm, tn), BF16)]
    scratch = [((d, tn), BF16)]
    return pl.pallas_call(
        _mlp_up_kernel,
        grid=(f // tn, n // tm),
        in_specs=[pl.BlockSpec((tm, d), lambda j, i: (i, 0)),
                  pl.BlockSpec((None, d, tn), lambda j, i: (layer, 0, j))],
        out_specs=pl.BlockSpec((tm, tn), lambda j, i: (i, j)),
        out_shape=jax.ShapeDtypeStruct((n, f), BF16),
        scratch_shapes=[pltpu.VMEM(s, t) for s, t in scratch],
        compiler_params=_params(("arbitrary", "arbitrary"), blocks, scratch),
        name="mlp_up")(xn, w)


def _matmul_norm_res_kernel(a_ref, w_ref, x_ref, g_ref, gn_ref, o_ref, xn_ref, acc_ref):
    k = pl.program_id(1)
    last = pl.num_programs(1) - 1
    tm = a_ref.shape[0]

    @pl.when(k == 0)
    def _():
        acc_ref[...] = jnp.zeros_like(acc_ref)

    @pl.when(k < last)
    def _():
        acc_ref[...] += _dot(a_ref[...], w_ref[...])

    @pl.when(k == last)
    def _():
        for r in range(RESIDUAL_EPILOGUE_GROUPS):
            rows = slice(r * (tm // RESIDUAL_EPILOGUE_GROUPS), (r + 1) * (tm // RESIDUAL_EPILOGUE_GROUPS))
            y = acc_ref[rows, :] + _dot(a_ref[rows, :], w_ref[...])
            var = jnp.mean(y * y, axis=-1, keepdims=True)
            x_new = x_ref[rows, :] + (y * lax.rsqrt(var + EPS)) * g_ref[...]
            o_ref[rows, :] = x_new
            xn_ref[rows, :] = _rmsnorm_bf16(x_new, gn_ref[...])


def _matmul_norm_res(a, w, x, g, layer, g_next, layer_next, *, tm, tk, name):
    n, kdim = a.shape
    d = x.shape[1]
    blocks = [((tm, tk), BF16), ((tk, d), BF16), ((tm, d), F32), ((1, d), F32), ((1, d), F32), ((tm, d), F32),
              ((tm, d), BF16)]
    scratch = [((tm, d), F32)]
    return pl.pallas_call(
        _matmul_norm_res_kernel,
        grid=(n // tm, kdim // tk),
        in_specs=[pl.BlockSpec((tm, tk), lambda i, k: (i, k)),
                  pl.BlockSpec((None, tk, d), lambda i, k: (layer, k, 0)),
                  pl.BlockSpec((tm, d), lambda i, k: (i, 0)),
                  pl.BlockSpec((None, 1, d), lambda i, k: (layer, 0, 0)),
                  pl.BlockSpec((None, 1, d), lambda i, k: (layer_next, 0, 0))],
        out_specs=[pl.BlockSpec((tm, d), lambda i, k: (i, 0)), pl.BlockSpec((tm, d), lambda i, k: (i, 0))],
        out_shape=[jax.ShapeDtypeStruct((n, d), F32), jax.ShapeDtypeStruct((n, d), BF16)],
        scratch_shapes=[pltpu.VMEM(s, t) for s, t in scratch],
        compiler_params=_params(("parallel", "arbitrary"), blocks, scratch),
        name=name)(a, w, x, g, g_next)


def _merge_kernel(oa_ref, ob_ref, oc_ref, ga_ref, gb_ref, gc_ref, wa_ref, wb_ref, wc_ref, o_ref):
    m = jax.nn.sigmoid(ga_ref[...]) * _dot(oa_ref[...], wa_ref[...])
    m = m + jax.nn.sigmoid(gb_ref[...]) * _dot(ob_ref[...], wb_ref[...])
    m = m + jax.nn.sigmoid(gc_ref[...]) * _dot(oc_ref[...], wc_ref[...])
    o_ref[...] = m.astype(o_ref.dtype)


def _merge(oa, ob, oc, proj, lay, wa, wb, wc, layer, *, tm, tn):
    n = oa.shape[0]
    d = wa.shape[-1]
    col = {k: lay[k][0] // tn for k in ("gate_a", "gate_b", "gate_c")}
    o_specs = [pl.BlockSpec((tm, o.shape[1]), lambda i, j: (i, 0)) for o in (oa, ob, oc)]
    g_specs = [pl.BlockSpec((tm, tn), functools.partial(lambda i, j, c: (i, c + j), c=col[k]))
               for k in ("gate_a", "gate_b", "gate_c")]
    w_mode = dict(pipeline_mode=pl.Buffered(1)) if tn == d else {}
    w_specs = [pl.BlockSpec((None, w.shape[1], tn), lambda i, j: (layer, 0, j), **w_mode) for w in (wa, wb, wc)]
    blocks = ([((tm, o.shape[1]), BF16) for o in (oa, ob, oc)] + [((tm, tn), F32)] * 3
              + [((w.shape[1], tn), BF16) for w in (wa, wb, wc)] + [((tm, tn), BF16)])
    return pl.pallas_call(
        _merge_kernel,
        grid=(n // tm, d // tn),
        in_specs=o_specs + g_specs + w_specs,
        out_specs=pl.BlockSpec((tm, tn), lambda i, j: (i, j)),
        out_shape=jax.ShapeDtypeStruct((n, d), BF16),
        compiler_params=_params(("parallel", "parallel"), blocks),
        name="gated_merge")(oa, ob, oc, proj, proj, proj, wa, wb, wc)


def _fox_gate_kernel(ff_ref, b_ref, lf_ref, c_ref, ct_ref):
    t = ff_ref.shape[0]
    lf_ref[...] = _log_sigmoid(ff_ref[...] + b_ref[...])
    tri = _tril01(LANES).astype(BF16)
    carry = jnp.zeros((1, LANES), F32)
    for blk in range(t // LANES):
        rows = slice(blk * LANES, (blk + 1) * LANES)
        cb = _dot01(tri, lf_ref[rows, :]) + carry
        c_ref[rows, :] = cb
        ct_ref[:, rows] = cb.T
        carry = cb[LANES - 1:LANES, :]


def _fox_gate(proj, lay, bias, layer, bx, tx):
    n = proj.shape[0]
    col = lay["ff"][0] // LANES
    blocks = [((tx, LANES), F32)] * 4 + [((LANES, tx), F32)]
    return pl.pallas_call(
        _fox_gate_kernel,
        grid=(bx,),
        in_specs=[pl.BlockSpec((tx, LANES), lambda b: (b, col)),
                  pl.BlockSpec((None, 1, LANES), lambda b: (layer, 0, 0))],
        out_specs=[pl.BlockSpec((tx, LANES), lambda b: (b, 0)),
                   pl.BlockSpec((tx, LANES), lambda b: (b, 0)),
                   pl.BlockSpec((None, LANES, tx), lambda b: (b, 0, 0))],
        out_shape=[jax.ShapeDtypeStruct((n, LANES), F32), jax.ShapeDtypeStruct((n, LANES), F32),
                   jax.ShapeDtypeStruct((bx, LANES, tx), F32)],
        compiler_params=_params(("parallel",), blocks),
        name="fox_gate")(proj, bias)


def _fox_attn_kernel(q_ref, k_ref, v_ref, ct_ref, cr_ref, o_ref, m_ref, l_ref, acc_ref, *, heads, scale):
    i, j = pl.program_id(1), pl.program_id(2)
    tq, tk = q_ref.shape[0], k_ref.shape[0]

    @pl.when(j == 0)
    def _():
        m_ref[...] = jnp.full_like(m_ref, NEG_BIG)
        l_ref[...] = jnp.zeros_like(l_ref)
        acc_ref[...] = jnp.zeros_like(acc_ref)

    @pl.when(j <= i)
    def _():
        q_pos = i * tq + lax.broadcasted_iota(jnp.int32, (tq, tk), 0)
        k_pos = j * tk + lax.broadcasted_iota(jnp.int32, (tq, tk), 1)
        causal = q_pos >= k_pos
        for h in range(heads):
            hs = slice(h * HEAD_DIM, (h + 1) * HEAD_DIM)
            s = _dot_nt(q_ref[:, hs].astype(BF16), k_ref[:, hs].astype(BF16)) * scale
            s = s + ct_ref[:, h:h + 1] - cr_ref[h:h + 1, :]
            s = jnp.where(causal, s, NEG_BIG)
            m_prev = m_ref[:, h:h + 1]
            m_new = jnp.maximum(m_prev, jnp.max(s, axis=-1, keepdims=True))
            alpha = jnp.exp(m_prev - m_new)
            p = jnp.exp(s - m_new)
            l_ref[:, h:h + 1] = alpha * l_ref[:, h:h + 1] + jnp.sum(p, axis=-1, keepdims=True)
            acc_ref[:, hs] = alpha * acc_ref[:, hs] + _dot(p.astype(BF16), v_ref[:, hs].astype(BF16))
            m_ref[:, h:h + 1] = m_new

    @pl.when(j == i)
    def _():
        for h in range(heads):
            hs = slice(h * HEAD_DIM, (h + 1) * HEAD_DIM)
            o_ref[:, hs] = (acc_ref[:, hs] / l_ref[:, h:h + 1]).astype(o_ref.dtype)


def _fox_attn(proj, lay, c_tok, c_row, bx, tx, *, tq):
    n = proj.shape[0]
    fw = lay["fq"][1]
    heads = fw // HEAD_DIM
    nq = tx // tq
    qc, kc, vc = (lay[k][0] // fw for k in ("fq", "fk", "fv"))
    blocks = [((tq, fw), F32)] * 3 + [((tq, LANES), F32), ((LANES, tq), F32), ((tq, fw), BF16)]
    scratch = [((tq, LANES), F32), ((tq, LANES), F32), ((tq, fw), F32)]
    return pl.pallas_call(
        functools.partial(_fox_attn_kernel, heads=heads, scale=HEAD_DIM ** -0.5),
        grid=(bx, nq, nq),
        in_specs=[pl.BlockSpec((tq, fw), lambda b, i, j: (b * nq + i, qc)),
                  pl.BlockSpec((tq, fw), lambda b, i, j: (b * nq + jnp.minimum(j, i), kc)),
                  pl.BlockSpec((tq, fw), lambda b, i, j: (b * nq + jnp.minimum(j, i), vc)),
                  pl.BlockSpec((tq, LANES), lambda b, i, j: (b * nq + i, 0)),
                  pl.BlockSpec((None, LANES, tq), lambda b, i, j: (b, 0, jnp.minimum(j, i)))],
        out_specs=pl.BlockSpec((tq, fw), lambda b, i, j: (b * nq + i, 0)),
        out_shape=jax.ShapeDtypeStruct((n, fw), BF16),
        scratch_shapes=[pltpu.VMEM(s, t) for s, t in scratch],
        compiler_params=_params(("parallel", "parallel", "arbitrary"), blocks, scratch),
        name="fox_attn_prompt")(proj, proj, proj, c_tok, c_row)


def _page_suffix(x, lane, sub, heads):
    y = x
    sh = heads
    while sh < LANES:
        y = y + jnp.where(lane < LANES - sh, pltpu.roll(y, LANES - sh, axis=1), 0.0)
        sh *= 2
    t = jnp.where(lane < heads, y, 0.0)
    sh = heads
    while sh < LANES:
        t = t + pltpu.roll(t, sh, axis=1)
        sh *= 2
    z = t
    sh = 1
    while sh < SUBLANES:
        z = z + jnp.where(sub < SUBLANES - sh, pltpu.roll(z, SUBLANES - sh, axis=0), 0.0)
        sh *= 2
    return (y - x) + (z - t), z[0:1, :]


def _fox_decode_kernel(pt_ref, q_ref, kn_ref, vn_ref, ffp_ref, bp_ref, ffc_ref, bc_ref, *rest,
                       pages, heads, n_new, scale):
    del pt_ref
    k_refs, v_refs, lf_refs = rest[:pages], rest[pages:2 * pages], rest[2 * pages:3 * pages]
    o_ref, lfo_ref, m_ref, l_ref, acc_ref, carry_ref, rn_ref = rest[3 * pages:]
    step = pl.program_id(1)
    rows = q_ref.shape[0]
    lane = lax.broadcasted_iota(jnp.int32, (SUBLANES, LANES), 1)
    sub = lax.broadcasted_iota(jnp.int32, (SUBLANES, LANES), 0)
    r_i = lax.broadcasted_iota(jnp.int32, (rows, LANES), 0)
    c_i = lax.broadcasted_iota(jnp.int32, (rows, LANES), 1)
    head_ok = (r_i % heads) == (c_i % heads)
    qb = q_ref[...].astype(BF16)

    def update(scores, values):
        m_prev = m_ref[...]
        m_new = m_prev
        for s in scores:
            m_new = jnp.maximum(m_new, jnp.max(s, axis=-1, keepdims=True))
        alpha = jnp.exp(m_prev - m_new)
        l_new = alpha * l_ref[...]
        acc = alpha * acc_ref[...]
        for s, v in zip(scores, values):
            p = jnp.exp(s - m_new)
            l_new = l_new + jnp.sum(p, axis=-1, keepdims=True)
            acc = acc + _dot(p.astype(BF16), v)
        m_ref[...] = m_new
        l_ref[...] = l_new
        acc_ref[...] = acc

    @pl.when(step == 0)
    def _():
        m_ref[...] = jnp.full_like(m_ref, NEG_BIG)
        l_ref[...] = jnp.zeros_like(l_ref)
        acc_ref[...] = jnp.zeros_like(acc_ref)
        new_ok = (sub == 0) & (lane < n_new * heads)
        lf = jnp.where(new_ok, _log_sigmoid(ffp_ref[...] + bp_ref[...]), 0.0)
        lfo_ref[...] = lf
        gate, total = _page_suffix(lf, lane, sub, heads)
        carry_ref[...] = total
        lfc = _log_sigmoid(ffc_ref[...] + bc_ref[...])
        run = jnp.zeros((heads, 1), F32)
        for t in range(n_new - 1, -1, -1):
            rn_ref[t * heads:(t + 1) * heads, :] = run
            run = run + lfc[t * heads:(t + 1) * heads, :]
        s = _dot_nt(qb, kn_ref[...].astype(BF16)) * scale + gate[0:1, :] - rn_ref[...]
        ok = head_ok & (c_i // heads <= r_i // heads) & (c_i < n_new * heads)
        update([jnp.where(ok, s, NEG_BIG)], [vn_ref[...].astype(BF16)])

    carry = carry_ref[...]
    rn = rn_ref[...]
    scores = []
    for r in range(pages):
        gate, total = _page_suffix(lf_refs[r][...], lane, sub, heads)
        gate = gate + carry
        carry = carry + total
        s_page = _dot_nt(qb, k_refs[r][...].astype(BF16)) * scale
        scores.append([jnp.where(head_ok, s_page[:, c * LANES:(c + 1) * LANES] + gate[c:c + 1, :] - rn, NEG_BIG)
                       for c in range(SUBLANES)])
    carry_ref[...] = carry
    m_prev = m_ref[...]
    m_new = m_prev
    for page_scores in scores:
        for s in page_scores:
            m_new = jnp.maximum(m_new, jnp.max(s, axis=-1, keepdims=True))
    alpha = jnp.exp(m_prev - m_new)
    l_new = alpha * l_ref[...]
    acc = alpha * acc_ref[...]
    for r in range(pages):
        probs = [jnp.exp(s - m_new) for s in scores[r]]
        for p in probs:
            l_new = l_new + jnp.sum(p, axis=-1, keepdims=True)
        acc = acc + _dot(jnp.concatenate(probs, axis=-1).astype(BF16), v_refs[r][...].astype(BF16))
    m_ref[...] = m_new
    l_ref[...] = l_new
    acc_ref[...] = acc

    @pl.when(step == pl.num_programs(1) - 1)
    def _():
        o_ref[...] = (acc_ref[...] / l_ref[...]).astype(o_ref.dtype)


def _fox_decode(q, k_new, v_new, ff_page, bias_page, ff_col, bias_col, cache_k, cache_v, cache_lf, pages_flat,
                n_pages, *, heads, n_new):
    bs, rows, _ = q.shape
    page_rows = cache_k.shape[1]
    assert page_rows == SUBLANES * LANES and heads * (LANES // heads) == LANES and n_pages % DECODE_PAGES_PER_STEP == 0
    pages = DECODE_PAGES_PER_STEP
    n_steps = n_pages // pages

    def page_map(r):
        return lambda b, s, pt: (pt[b * n_pages + (n_pages - 1 - (s * pages + r))], 0, 0)

    in_specs = [pl.BlockSpec((None, rows, HEAD_DIM), lambda b, s, pt: (b, 0, 0)),
                pl.BlockSpec((None, LANES, HEAD_DIM), lambda b, s, pt: (b, 0, 0)),
                pl.BlockSpec((None, LANES, HEAD_DIM), lambda b, s, pt: (b, 0, 0)),
                pl.BlockSpec((None, SUBLANES, LANES), lambda b, s, pt: (b, 0, 0)),
                pl.BlockSpec((SUBLANES, LANES), lambda b, s, pt: (0, 0)),
                pl.BlockSpec((None, rows, 1), lambda b, s, pt: (b, 0, 0)),
                pl.BlockSpec((rows, 1), lambda b, s, pt: (0, 0))]
    in_specs += [pl.BlockSpec((None, page_rows, HEAD_DIM), page_map(r)) for r in range(pages)] * 2
    in_specs += [pl.BlockSpec((None, SUBLANES, LANES), page_map(r)) for r in range(pages)]
    blocks = [((page_rows, HEAD_DIM), F32)] * (2 * pages) + [((LANES, HEAD_DIM), F32)] * 4
    scratch = [((rows, 1), F32), ((rows, 1), F32), ((rows, HEAD_DIM), F32), ((1, LANES), F32), ((rows, 1), F32)]
    grid_spec = pltpu.PrefetchScalarGridSpec(
        num_scalar_prefetch=1, grid=(bs, n_steps), in_specs=in_specs,
        out_specs=[pl.BlockSpec((None, rows, HEAD_DIM), lambda b, s, pt: (b, 0, 0)),
                   pl.BlockSpec((None, SUBLANES, LANES), lambda b, s, pt: (b, 0, 0))],
        scratch_shapes=[pltpu.VMEM(s, t) for s, t in scratch])
    return pl.pallas_call(
        functools.partial(_fox_decode_kernel, pages=pages, heads=heads, n_new=n_new, scale=HEAD_DIM ** -0.5),
        grid_spec=grid_spec,
        out_shape=[jax.ShapeDtypeStruct((bs, rows, HEAD_DIM), BF16),
                   jax.ShapeDtypeStruct((bs, SUBLANES, LANES), F32)],
        compiler_params=_params(("parallel", "arbitrary"), blocks, scratch),
        name="fox_decode")(pages_flat, q, k_new, v_new, ff_page, bias_page, ff_col, bias_col,
                           *([cache_k] * pages), *([cache_v] * pages), *([cache_lf] * pages))


def _hi_lo(x):
    hi = x.astype(BF16)
    return hi, (x - hi.astype(F32)).astype(BF16)


def _hi_lo_lhs(a):
    hi = a.astype(BF16).astype(F32)
    a16 = jnp.concatenate([hi, a - hi], axis=1).astype(BF16)
    return jnp.concatenate([a16, a16], axis=1)


def _hi_lo_rhs(b):
    hi, lo = _hi_lo(b)
    return jnp.concatenate([hi, hi, lo, lo], axis=0)


def _unit_lower_solves(ms, rhss):
    c = ms[0].shape[0]
    assert 2 * c == LANES
    xs, ps, power = list(rhss), list(ms), 1
    while True:
        lhs = [_hi_lo_lhs(p) for p in ps]
        upd = [_dot(a, _hi_lo_rhs(x)) for a, x in zip(lhs, xs)]
        xs = [x - u for x, u in zip(xs, upd)] if power == 1 else [x + u for x, u in zip(xs, upd)]
        if 2 * power >= c:
            return xs
        ps = [_dot(a, _hi_lo_rhs(p)) for a, p in zip(lhs, ps)]
        power *= 2


def _gdn_prep_kernel(qkv_ref, ab_ref, cw_ref, al_ref, dt_ref, cb0_ref,
                     u_ref, w_ref, qe_ref, kdt_ref, attn_ref, egl_ref, cb_out_ref, xe_ref, *, heads, t_valid):
    c = pl.program_id(1)
    ch = qkv_ref.shape[0]
    width = heads * HEAD_DIM
    tail = CONV_W - 1
    base = SUBLANES
    last_chunk = (t_valid - 1) // ch
    last_len = t_valid - last_chunk * ch

    @pl.when(c == 0)
    def _():
        xe_ref[base - tail:base, :] = cb0_ref[...]

    xe_ref[base:base + ch, :] = qkv_ref[...]
    y = xe_ref[base - tail:base - tail + ch, :] * cw_ref[0:1, :]
    for i in range(1, CONV_W):
        y = y + xe_ref[base - tail + i:base - tail + i + ch, :] * cw_ref[i:i + 1, :]
    y = _silu(y)

    @pl.when(c == last_chunk)
    def _():
        cb_out_ref[...] = xe_ref[base + last_len - tail:base + last_len, :]

    xe_ref[base - tail:base, :] = xe_ref[base + ch - tail:base + ch, :]

    valid = (c * ch + lax.broadcasted_iota(jnp.int32, (ch, 1), 0)) < t_valid
    ab = ab_ref[...]
    g_all = jnp.where(valid, -jnp.exp(al_ref[...]) * _softplus(ab + dt_ref[...]), 0.0)
    beta_all = jnp.where(valid, jax.nn.sigmoid(ab), 0.0)
    cum = _dot01(_tril01(ch).astype(BF16), g_all)
    cum_t = cum.T
    lower = _tril01(ch)
    strict = _tril01(ch, strict=True)
    egl_ref[...] = jnp.exp(cum[ch - 1:ch, :])

    hr = range(heads)
    sl = lambda part, h: slice(part * width + h * HEAD_DIM, part * width + (h + 1) * HEAD_DIM)
    gc = [cum[:, h:h + 1] for h in hr]
    beta = [beta_all[:, heads + h:heads + h + 1] for h in hr]
    decay = [jnp.exp(jnp.where(lower, gc[h] - cum_t[h:h + 1, :], NEG_BIG)) for h in hr]
    q = [y[:, sl(0, h)] for h in hr]
    k = [y[:, sl(1, h)] for h in hr]
    q = [x * lax.rsqrt(jnp.sum(x * x, axis=-1, keepdims=True) + EPS) * (HEAD_DIM ** -0.5) for x in q]
    k = [x * lax.rsqrt(jnp.sum(x * x, axis=-1, keepdims=True) + EPS) for x in k]
    kb = [k[h] * beta[h] for h in hr]
    k16 = [x.astype(BF16) for x in k]
    eg = [jnp.exp(gc[h]) for h in hr]
    sols = _unit_lower_solves(
        [jnp.where(strict, _dot_nt(kb[h].astype(BF16), k16[h]) * decay[h], 0.0) for h in hr],
        [jnp.concatenate([y[:, sl(2, h)] * beta[h], kb[h] * eg[h]], axis=-1) for h in hr])
    for h in hr:
        sol = sols[h]
        u_ref[:, sl(0, h)] = sol[:, :HEAD_DIM]
        w_ref[:, sl(0, h)] = sol[:, HEAD_DIM:].astype(BF16)
        qe_ref[:, sl(0, h)] = (q[h] * eg[h]).astype(BF16)
        attn_ref[h] = (_dot_nt(q[h].astype(BF16), k16[h]) * decay[h]).astype(BF16)
        kdt_ref[h] = (k[h] * jnp.exp(cum[ch - 1:ch, h:h + 1] - gc[h])).T.astype(BF16)


def _gdn_scan_kernel(u_ref, w_ref, qe_ref, kdt_ref, attn_ref, egl_ref, gz_ref, gn_ref, s0_ref,
                     o_ref, s_out_ref, st_ref, *, heads):
    c = pl.program_id(0)
    bx = u_ref.shape[0]

    @pl.when(c == 0)
    def _():
        st_ref[...] = s0_ref[...]

    chains = [(b, h) for b in range(bx) for h in range(heads)]
    hs = lambda h: slice(h * HEAD_DIM, (h + 1) * HEAD_DIM)
    s_old = [st_ref[b, h] for b, h in chains]
    s16 = [s.astype(BF16) for s in s_old]
    ws = [_dot(w_ref[b, :, hs(h)], s) for (b, h), s in zip(chains, s16)]
    qs = [_dot(qe_ref[b, :, hs(h)], s) for (b, h), s in zip(chains, s16)]
    vn16 = [(u_ref[b, :, hs(h)] - x).astype(BF16) for (b, h), x in zip(chains, ws)]
    for i, (b, h) in enumerate(chains):
        o = qs[i] + _dot(attn_ref[b, h], vn16[i])
        st_ref[b, h] = s_old[i] * egl_ref[b, :, h:h + 1] + _dot(kdt_ref[b, h], vn16[i])
        on = (o * lax.rsqrt(jnp.mean(o * o, axis=-1, keepdims=True) + EPS)) * gn_ref[...]
        o_ref[b, :, hs(h)] = (on * _silu(gz_ref[b, :, hs(h)])).astype(o_ref.dtype)

    @pl.when(c == pl.num_programs(0) - 1)
    def _():
        s_out_ref[...] = st_ref[...]


def _gdn(proj, narrow, lay, conv_w, a_log, dt_bias, gnorm, s0, cb0, layer, bx, tx, t_valid):
    n, wp = proj.shape
    gw = lay["gz"][1]
    heads = gw // HEAD_DIM
    ch = CHUNK
    nc = tx // ch
    qkv_c, ab_c, gz_c = lay["gqkv"][0] // (3 * gw), lay["gab"][0] // LANES, lay["gz"][0] // gw
    blocks = [((ch, 3 * gw), F32), ((ch, LANES), F32), ((CONV_W, 3 * gw), F32), ((CONV_W - 1, 3 * gw), F32),
              ((ch, gw), F32), ((ch, gw), BF16), ((ch, gw), BF16), ((heads, HEAD_DIM, LANES), BF16),
              ((heads, ch, LANES), BF16), ((CONV_W - 1, 3 * gw), F32)]
    scratch = [((SUBLANES + ch, 3 * gw), F32)]
    u, w16, qe16, kdt, attn, egl, cb_out = pl.pallas_call(
        functools.partial(_gdn_prep_kernel, heads=heads, t_valid=t_valid),
        grid=(bx, nc),
        in_specs=[pl.BlockSpec((ch, 3 * gw), lambda b, c: (b * nc + c, qkv_c)),
                  pl.BlockSpec((ch, LANES), lambda b, c: (b * nc + c, ab_c)),
                  pl.BlockSpec((None, CONV_W, 3 * gw), lambda b, c: (layer, 0, 0)),
                  pl.BlockSpec((None, 1, LANES), lambda b, c: (layer, 0, 0)),
                  pl.BlockSpec((None, 1, LANES), lambda b, c: (layer, 0, 0)),
                  pl.BlockSpec((None, CONV_W - 1, 3 * gw), lambda b, c: (b, 0, 0))],
        out_specs=[pl.BlockSpec((ch, gw), lambda b, c: (b * nc + c, 0)),
                   pl.BlockSpec((ch, gw), lambda b, c: (b * nc + c, 0)),
                   pl.BlockSpec((ch, gw), lambda b, c: (b * nc + c, 0)),
                   pl.BlockSpec((None, None, heads, HEAD_DIM, ch), lambda b, c: (b, c, 0, 0, 0)),
                   pl.BlockSpec((None, None, heads, ch, ch), lambda b, c: (b, c, 0, 0, 0)),
                   pl.BlockSpec((None, None, 1, LANES), lambda b, c: (b, c, 0, 0)),
                   pl.BlockSpec((None, CONV_W - 1, 3 * gw), lambda b, c: (b, 0, 0))],
        out_shape=[jax.ShapeDtypeStruct((n, gw), F32), jax.ShapeDtypeStruct((n, gw), BF16),
                   jax.ShapeDtypeStruct((n, gw), BF16),
                   jax.ShapeDtypeStruct((bx, nc, heads, HEAD_DIM, ch), BF16),
                   jax.ShapeDtypeStruct((bx, nc, heads, ch, ch), BF16),
                   jax.ShapeDtypeStruct((bx, nc, 1, LANES), F32),
                   jax.ShapeDtypeStruct((bx, CONV_W - 1, 3 * gw), F32)],
        scratch_shapes=[pltpu.VMEM(s, t) for s, t in scratch],
        compiler_params=_params(("parallel", "arbitrary"), blocks, scratch),
        name="gdn_prep")(proj, narrow, conv_w, a_log, dt_bias, cb0)

    state = (bx, heads, HEAD_DIM, HEAD_DIM)
    blocks = [((bx, ch, gw), F32), ((bx, ch, gw), BF16), ((bx, ch, gw), BF16), ((bx, heads, HEAD_DIM, LANES), BF16),
              ((bx, heads, ch, LANES), BF16), ((bx, ch, gw), F32), (state, F32), ((bx, ch, gw), BF16), (state, F32)]
    scratch = [(state, F32)]
    rows3 = lambda a: a.reshape(bx, tx, a.shape[-1])
    o, s_out = pl.pallas_call(
        functools.partial(_gdn_scan_kernel, heads=heads),
        grid=(nc,),
        in_specs=[pl.BlockSpec((bx, ch, gw), lambda c: (0, c, 0)),
                  pl.BlockSpec((bx, ch, gw), lambda c: (0, c, 0)),
                  pl.BlockSpec((bx, ch, gw), lambda c: (0, c, 0)),
                  pl.BlockSpec((bx, None, heads, HEAD_DIM, ch), lambda c: (0, c, 0, 0, 0)),
                  pl.BlockSpec((bx, None, heads, ch, ch), lambda c: (0, c, 0, 0, 0)),
                  pl.BlockSpec((bx, None, 1, LANES), lambda c: (0, c, 0, 0)),
                  pl.BlockSpec((bx, ch, gw), lambda c: (0, c, gz_c)),
                  pl.BlockSpec((None, 1, HEAD_DIM), lambda c: (layer, 0, 0)),
                  pl.BlockSpec(state, lambda c: (0, 0, 0, 0))],
        out_specs=[pl.BlockSpec((bx, ch, gw), lambda c: (0, c, 0)),
                   pl.BlockSpec(state, lambda c: (0, 0, 0, 0))],
        out_shape=[jax.ShapeDtypeStruct((bx, tx, gw), BF16), jax.ShapeDtypeStruct(state, F32)],
        scratch_shapes=[pltpu.VMEM(s, t) for s, t in scratch],
        compiler_params=_params(("arbitrary",), blocks, scratch),
        name="gdn_scan")(rows3(u), rows3(w16), rows3(qe16), kdt, attn, egl, rows3(proj), gnorm, s0)
    return o.reshape(n, gw), s_out, cb_out


def _hgrn_kernel(hq_ref, hf_ref, hi_ref, hg_ref, lb_ref, hn_ref, s0_ref, o_ref, s_out_ref, st_ref,
                 *, heads, t_valid):
    c = pl.program_id(1)
    ch = hq_ref.shape[0]
    nblk = ch // SUBLANES

    @pl.when(c == 0)
    def _():
        for h in range(heads):
            st_ref[h] = s0_ref[h].T

    tok = lax.broadcasted_iota(jnp.int32, (ch, 1), 0)
    valid = (c * ch + tok) < t_valid
    lb = lb_ref[...]
    hf = hf_ref[...]
    a = jnp.log(jnp.maximum(lb, LB_FLOOR))
    b = jnp.log1p(-lb) + _log_sigmoid(hf)
    log_f = jnp.maximum(a, b) + jnp.log1p(jnp.exp(-jnp.abs(a - b)))
    log_f = jnp.where(valid, log_f, 0.0)
    k_all = jnp.where(valid, (1.0 - lb) * jax.nn.sigmoid(-hf), 0.0)
    cum_all = _dot01(_tril01(ch).astype(BF16), log_f)
    q_all = _silu(hq_ref[...])

    ti = lax.broadcasted_iota(jnp.int32, (ch, ch), 0)
    tj = lax.broadcasted_iota(jnp.int32, (ch, ch), 1)
    sub3 = lax.broadcasted_iota(jnp.int32, (nblk, SUBLANES, HEAD_DIM), 1)

    hr = range(heads)
    hsl = [slice(h * HEAD_DIM, (h + 1) * HEAD_DIM) for h in hr]
    g = [cum_all[:, s] for s in hsl]
    q = [q_all[:, s] for s in hsl]
    k = [k_all[:, s] for s in hsl]
    v16 = [hi_ref[:, s].astype(BF16) for s in hsl]
    attn_t = [jnp.zeros((ch, ch), F32) for _ in hr]
    half = ch // 2
    while half >= SUBLANES:
        blk = 2 * half
        upper = (tok % blk) >= half
        same = (ti // blk) == (tj // blk)
        es = []
        for h in hr:
            g3 = g[h].reshape(ch // blk, blk, HEAD_DIM)
            es.append(jnp.exp(-jnp.abs(g3 - g3[:, half - 1:half, :])).reshape(ch, HEAD_DIM))
        prods = [_dot_nt(jnp.where(upper, 0.0, k[h] * es[h]).astype(BF16),
                         jnp.where(upper, q[h] * es[h], 0.0).astype(BF16)) for h in hr]
        attn_t = [a + jnp.where(same, p, 0.0) for a, p in zip(attn_t, prods)]
        half //= 2
    diag = (ti // SUBLANES) == (tj // SUBLANES)
    for h in hr:
        g3, q3, k3 = (z.reshape(nblk, SUBLANES, HEAD_DIM) for z in (g[h], q[h], k[h]))
        cols = []
        for i in range(SUBLANES):
            e = jnp.exp(jnp.where(sub3 <= i, g3[:, i:i + 1, :] - g3, NEG_BIG))
            w = jnp.sum(e * k3 * q3[:, i:i + 1, :], axis=-1, keepdims=True).reshape(ch, 1)
            cols.append(jnp.where((tj % SUBLANES) == i, w, 0.0))
        while len(cols) > 1:
            cols = [a + b for a, b in zip(cols[0::2], cols[1::2])]
        attn_t[h] = attn_t[h] + jnp.where(diag, cols[0], 0.0)
    st = [st_ref[h] for h in hr]
    o = [_dot_nt((q[h] * jnp.exp(g[h])).astype(BF16), st[h].astype(BF16)) + _dot(attn_t[h].T.astype(BF16), v16[h])
         for h in hr]
    for h in hr:
        gl = g[h][ch - 1:ch, :]
        kd = k[h] * jnp.exp(gl - g[h])
        st_ref[h] = st[h] * jnp.exp(gl) + _dot(hi_ref[:, hsl[h]].T.astype(BF16), kd.astype(BF16))
        on = (o[h] * lax.rsqrt(jnp.mean(o[h] * o[h], axis=-1, keepdims=True) + EPS)) * hn_ref[...]
        o_ref[:, hsl[h]] = (on * _silu(hg_ref[:, hsl[h]])).astype(o_ref.dtype)

    @pl.when(c == pl.num_programs(1) - 1)
    def _():
        for h in range(heads):
            s_out_ref[h] = st_ref[h].T


def _hgrn(proj, lay, lb, hnorm, s0, layer, bx, tx, t_valid):
    n = proj.shape[0]
    hw = lay["hq"][1]
    heads = hw // HEAD_DIM
    ch = CHUNK
    nc = tx // ch
    cols = [lay[k][0] // hw for k in ("hq", "hf", "hi", "hg")]
    blocks = [((ch, hw), F32)] * 4 + [((heads, HEAD_DIM, HEAD_DIM), F32)] * 2 + [((ch, hw), BF16)]
    scratch = [((heads, HEAD_DIM, HEAD_DIM), F32)]
    in_specs = [pl.BlockSpec((ch, hw), functools.partial(lambda b, c, col: (b * nc + c, col), col=col))
                for col in cols]
    in_specs += [pl.BlockSpec((None, 1, hw), lambda b, c: (layer, 0, 0)),
                 pl.BlockSpec((None, 1, HEAD_DIM), lambda b, c: (layer, 0, 0)),
                 pl.BlockSpec((None, heads, HEAD_DIM, HEAD_DIM), lambda b, c: (b, 0, 0, 0))]
    return pl.pallas_call(
        functools.partial(_hgrn_kernel, heads=heads, t_valid=t_valid),
        grid=(bx, nc),
        in_specs=in_specs,
        out_specs=[pl.BlockSpec((ch, hw), lambda b, c: (b * nc + c, 0)),
                   pl.BlockSpec((None, heads, HEAD_DIM, HEAD_DIM), lambda b, c: (b, 0, 0, 0))],
        out_shape=[jax.ShapeDtypeStruct((n, hw), BF16),
                   jax.ShapeDtypeStruct((bx, heads, HEAD_DIM, HEAD_DIM), F32)],
        scratch_shapes=[pltpu.VMEM(s, t) for s, t in scratch],
        compiler_params=_params(("parallel", "arbitrary"), blocks, scratch),
        name="hgrn2")(proj, proj, proj, proj, lb, hnorm, s0)


def _layout(d_model, fw, gw, hw):
    order = [("gate_a", d_model), ("gate_b", d_model), ("gate_c", d_model), ("fq", fw), ("fk", fw), ("fv", fw),
             ("gqkv", 3 * gw), ("gz", gw), ("hq", hw), ("hf", hw), ("hi", hw), ("hg", hw)]
    lay, start = {}, 0
    for name, width in order:
        lay[name] = (start, width)
        start += width
    lay["total"] = (0, start)
    lay["ff"], lay["gab"] = (0, LANES), (LANES, LANES)
    return lay


def _source_columns(d_model, fh, gh, fw, gw, hw):
    splits = [("fq", fw), ("fk", fw), ("fv", fw), ("ff", fh), ("gq", gw), ("gk", gw), ("gv", gw), ("ga", gh),
              ("gb", gh), ("gz", gw), ("hq", hw), ("hf", hw), ("hi", hw), ("hg", hw),
              ("gate_a", d_model), ("gate_b", d_model), ("gate_c", d_model)]
    src, start = {}, 0
    for name, width in splits:
        src[name] = start
        start += width
    src["end"] = start
    return src


def _in_proj_row_starts(lay, src, tn):
    starts = []
    for first, dst, width in (("gate_a", "gate_a", 3 * lay["gate_a"][1]), ("fq", "fq", 3 * lay["fq"][1]),
                              ("gq", "gqkv", lay["gqkv"][1]), ("gz", "gz", lay["total"][1] - lay["gz"][0])):
        assert lay[dst][0] == len(starts) * tn and width % tn == 0
        starts += [src[first] + k * tn for k in range(width // tn)]
    assert len(starts) * tn == lay["total"][1]
    return starts


def _pad_lanes(x, fill=0.0):
    return jnp.pad(x.astype(F32), ((0, 0), (0, LANES - x.shape[-1])), constant_values=fill)[:, None, :]


def _tile(n, candidates):
    for c in candidates:
        if n % c == 0:
            return c
    raise ValueError(f"no tile for {n} among {candidates}")


def kernel(x_prompt, x_sample, cache_fox_k, cache_fox_v, cache_fox_logf, state_gdn, state_gdn_conv, state_hgrn,
           page_table, norm_mix_pre, norm_mix_post, norm_mlp_pre, norm_mlp_post, w_in, fox_f_bias, gdn_conv_w,
           gdn_a_log, gdn_dt_bias, gdn_norm, hgrn_lower_bounds, hgrn_norm, w_fox_o, w_gdn_o, w_hgrn_o, w_out,
           w_up, w_down):
    bp, tp, d = x_prompt.shape
    bs, ts, _ = x_sample.shape
    depth = w_in.shape[0]
    fh, gh = fox_f_bias.shape[1], gdn_a_log.shape[1]
    fw, gw, hw = fh * HEAD_DIM, gh * HEAD_DIM, hgrn_lower_bounds.shape[1]
    hh = hw // HEAD_DIM
    n_pool, page = cache_fox_k.shape[1], cache_fox_k.shape[2]
    n_pages = page_table.shape[1]
    tsp = CHUNK
    tsd = SAMPLE_DENSE_ROWS
    assert tp % CHUNK == 0 and tp % LANES == 0 and ts <= tsd <= tsp and ts >= CONV_W - 1
    assert page * fh == SUBLANES * LANES and 2 * gh <= LANES and fh <= LANES
    lay = _layout(d, fw, gw, hw)
    wp = lay["total"][1]
    src = _source_columns(d, fh, gh, fw, gw, hw)
    assert src["end"] == w_in.shape[-1] and src["gb"] == src["ga"] + gh
    row_starts = _in_proj_row_starts(lay, src, IN_PROJ_COL_TILE)

    w_in_t = jnp.swapaxes(w_in.astype(F32), 1, 2)
    w_u = w_up.astype(F32)
    w_fo, w_go, w_ho, w_o, w_d = (w.astype(BF16) for w in (w_fox_o, w_gdn_o, w_hgrn_o, w_out, w_down))
    g_mix_pre, g_mix_post, g_mlp_pre, g_mlp_post = (
        g.astype(F32)[:, None, :] for g in (norm_mix_pre, norm_mix_post, norm_mlp_pre, norm_mlp_post))
    f_bias = _pad_lanes(fox_f_bias)
    a_log, dt_bias = _pad_lanes(gdn_a_log), _pad_lanes(gdn_dt_bias)
    g_norm, h_norm = gdn_norm.astype(F32)[:, None, :], hgrn_norm.astype(F32)[:, None, :]
    conv_w = gdn_conv_w.astype(F32)
    lb_all = _lower_bounds(hgrn_lower_bounds)[:, None, :]

    ck = cache_fox_k.reshape(depth * n_pool, page * fh, HEAD_DIM)
    cv = cache_fox_v.reshape(depth * n_pool, page * fh, HEAD_DIM)
    clf = cache_fox_logf.astype(F32).reshape(depth * n_pool, SUBLANES, LANES)
    bias_page = jnp.tile(fox_f_bias.astype(F32), (1, LANES // fh))[:, None, :] * jnp.ones((1, SUBLANES, 1), F32)
    bias_col = jnp.tile(fox_f_bias.astype(F32), (1, ts))[:, :, None]

    zeros_state_p = jnp.zeros((bp, gh, HEAD_DIM, HEAD_DIM), F32)
    zeros_hstate_p = jnp.zeros((bp, hh, HEAD_DIM, HEAD_DIM), F32)
    zeros_conv_p = jnp.zeros((bp, CONV_W - 1, 3 * gw), F32)

    xp = x_prompt.reshape(bp * tp, d)
    xs = jnp.pad(x_sample, ((0, 0), (0, tsd - ts), (0, 0))).reshape(bs * tsd, d)

    tm_p = _tile(bp * tp, (1024, 512, 256, 128, 64))
    tm_s = _tile(bs * tsd, (512, 256, 128, 64, 32, 16))
    tq = _tile(tp, FOX_Q_TILES)

    def in_proj(xn, l, tm):
        proj = _in_proj(xn, w_in_t, row_starts, l, tm=tm, tn=IN_PROJ_COL_TILE)
        narrow = _in_proj_narrow(xn, w_in_t, src["ff"], fh, src["ga"], 2 * gh, l, tm=tm)
        return proj, narrow

    def dense_tail(x, oa, ob, oc, proj, l, tm):
        merged = _merge(oa, ob, oc, proj, lay, w_fo, w_go, w_ho, l, tm=min(tm, MERGE_ROW_TILE), tn=d)
        x, xn = _matmul_norm_res(merged, w_o, x, g_mix_post, l, g_mlp_pre, l, tm=min(tm, RESIDUAL_ROW_TILE),
                                 tk=_tile(d, RESIDUAL_K_TILES), name="out_proj_norm_res")
        u = _mlp_up(xn, w_u, l, tm=tm, tn=_tile(w_u.shape[-1], (1024, 512, 256)))
        return _matmul_norm_res(u, w_d, x, g_mlp_post, l, g_mix_pre, (l + 1) % depth,
                                tm=min(tm, RESIDUAL_ROW_TILE), tk=_tile(w_d.shape[1], RESIDUAL_K_TILES),
                                name="mlp_down_norm_res")

    xpn = _rmsnorm_cast(xp, g_mix_pre, 0, tm=tm_p)
    xsn = _rmsnorm_cast(xs, g_mix_pre, 0, tm=tm_s)
    outs = {k: [] for k in ("fkp", "fvp", "flp", "fks", "fvs", "fls", "gsp", "gss", "gcp", "gcs", "hsp", "hss")}
    for l in range(depth):
        proj, narrow = in_proj(xpn, l, tm_p)
        lf, c_tok, c_row = _fox_gate(narrow, lay, f_bias, l, bp, tp)
        oa = _fox_attn(proj, lay, c_tok, c_row, bp, tp, tq=tq)
        ob, gs, gc = _gdn(proj, narrow, lay, conv_w, a_log, dt_bias, g_norm, zeros_state_p, zeros_conv_p, l, bp, tp,
                          tp)
        oc, hs = _hgrn(proj, lay, lb_all, h_norm, zeros_hstate_p, l, bp, tp, tp)
        xp, xpn = dense_tail(xp, oa, ob, oc, proj, l, tm_p)
        outs["fkp"].append(proj[:, lay["fk"][0]:lay["fk"][0] + fw].reshape(bp, tp, fh, HEAD_DIM))
        outs["fvp"].append(proj[:, lay["fv"][0]:lay["fv"][0] + fw].reshape(bp, tp, fh, HEAD_DIM))
        outs["flp"].append(lf[:, :fh].reshape(bp, tp, fh))
        outs["gsp"].append(gs)
        outs["gcp"].append(gc)
        outs["hsp"].append(hs)

        proj, narrow = in_proj(xsn, l, tm_s)
        p3 = proj.reshape(bs, tsd, wp)[:, :ts]
        q_new, k_new, v_new = (p3[:, :, lay[k][0]:lay[k][0] + fw] for k in ("fq", "fk", "fv"))
        ff_new = narrow.reshape(bs, tsd, 2 * LANES)[:, :ts, lay["ff"][0]:lay["ff"][0] + fh]
        to_chunk = lambda z: jnp.pad(z.reshape(bs, tsd, z.shape[-1]),
                                     ((0, 0), (0, tsp - tsd), (0, 0))).reshape(bs * tsp, z.shape[-1])
        from_chunk = lambda z: z.reshape(bs, tsp, z.shape[-1])[:, :tsd].reshape(bs * tsd, z.shape[-1])
        proj_c, narrow_c = to_chunk(proj), to_chunk(narrow)
        rows = ts * fh
        pad_rows = lambda z: jnp.pad(z.reshape(bs, rows, HEAD_DIM), ((0, 0), (0, LANES - rows), (0, 0)))
        ff_page = jnp.pad(ff_new.reshape(bs, 1, rows), ((0, 0), (0, SUBLANES - 1), (0, LANES - rows)))
        oa_s, lf_page = _fox_decode(
            q_new.reshape(bs, rows, HEAD_DIM), pad_rows(k_new), pad_rows(v_new), ff_page, bias_page[l],
            ff_new.reshape(bs, rows, 1), bias_col[l], ck, cv, clf,
            (page_table.astype(jnp.int32) + l * n_pool).reshape(-1), n_pages, heads=fh, n_new=ts)
        oa = jnp.pad(oa_s.reshape(bs, ts, fw), ((0, 0), (0, tsd - ts), (0, 0))).reshape(bs * tsd, fw)
        ob, gs, gc = _gdn(proj_c, narrow_c, lay, conv_w, a_log, dt_bias, g_norm, state_gdn[l].astype(F32),
                          state_gdn_conv[l].astype(F32), l, bs, tsp, ts)
        oc, hs = _hgrn(proj_c, lay, lb_all, h_norm, state_hgrn[l].astype(F32), l, bs, tsp, ts)
        xs, xsn = dense_tail(xs, oa, from_chunk(ob), from_chunk(oc), proj, l, tm_s)
        outs["fks"].append(k_new.reshape(bs, ts, fh, HEAD_DIM))
        outs["fvs"].append(v_new.reshape(bs, ts, fh, HEAD_DIM))
        outs["fls"].append(lf_page[:, 0, :rows].reshape(bs, ts, fh))
        outs["gss"].append(gs)
        outs["gcs"].append(gc)
        outs["hss"].append(hs)

    st = lambda k: jnp.stack(outs[k])
    return (xp.reshape(bp, tp, d), xs.reshape(bs, tsd, d)[:, :ts],
            st("fkp"), st("fvp"), st("flp"), st("fks"), st("fvs"), st("fls"),
            st("gsp"), st("gss"), st("gcp"), st("gcs"), st("hsp"), st("hss"))
```
